```python
import jax, jax.numpy as jnp
from jax import lax
import numpy as np

D_MODEL = 1024
BATCH = 8
SEQ = 2048
DEPTH = 2

CTX_LEN = 256
GRID_W = 64
N_Q_HEADS = 8
N_KV_HEADS = 2
HEAD_DIM = 64
ATTN_WIDTH = N_Q_HEADS * HEAD_DIM
KV_WIDTH = N_KV_HEADS * HEAD_DIM
Q_BLOCK = 128
ROPE_THETA = 10000.0
LRU_WIDTH = 512
LRU_BLOCKS = 8
LRU_BLOCK = LRU_WIDTH // LRU_BLOCKS
CONV_WIDTH = 4
LRU_C = 8.0
N_DIRS = 2
MIX_WIDTH = ATTN_WIDTH + LRU_WIDTH
IN_WIDTH = ATTN_WIDTH + 2 * KV_WIDTH + 2 * LRU_WIDTH
N_EXPERTS = 16
N_EXPERT_GROUPS = 4
EXPERTS_PER_GROUP = N_EXPERTS // N_EXPERT_GROUPS
TOP_K = 2
D_EXPERT = 512
N_MOD = 6
NORM_EPS = 1e-6

kernel_name = "hybrid_gqa_rglru_grouped_moe_dit"


def rms_norm(x, g):
    xf = x.astype(jnp.float32)
    y = xf * lax.rsqrt(jnp.mean(xf * xf, axis=-1, keepdims=True) + NORM_EPS)
    return (y * g.astype(jnp.float32)).astype(x.dtype)


def modulate(h, shift, scale):
    return h * (1 + scale) + shift


def axial_rope_tables(n_tokens):
    rows = n_tokens // GRID_W
    r, col = jnp.meshgrid(jnp.arange(rows), jnp.arange(GRID_W), indexing="ij")
    r = r.reshape(-1).astype(jnp.float32)
    col = col.reshape(-1).astype(jnp.float32)
    half = HEAD_DIM // 2
    inv = ROPE_THETA ** (-jnp.arange(0, half, 2, dtype=jnp.float32) / half)
    ang = jnp.concatenate([r[:, None] * inv, col[:, None] * inv], axis=-1)
    return jnp.cos(ang), jnp.sin(ang)


def apply_axial_rope(x, cos, sin):
    S = x.shape[1]
    quarter = HEAD_DIM // 4
    xr = x.reshape(x.shape[:-1] + (2, 2, quarter))
    x1, x2 = xr[..., 0, :], xr[..., 1, :]
    cs = cos.reshape(S, 1, 2, quarter).astype(x.dtype)
    sn = sin.reshape(S, 1, 2, quarter).astype(x.dtype)
    o1 = x1 * cs - x2 * sn
    o2 = x2 * cs + x1 * sn
    return jnp.stack([o1, o2], axis=-2).reshape(x.shape)


def blocked_attention(q, k, v):
    B, Tq = q.shape[:2]
    n_blk = Tq // Q_BLOCK
    grp = N_Q_HEADS // N_KV_HEADS
    scale = HEAD_DIM ** -0.5
    qb = q.reshape(B, n_blk, Q_BLOCK, N_KV_HEADS, grp, HEAD_DIM).transpose(1, 0, 2, 3, 4, 5)

    def one_block(q_blk):
        s = jnp.einsum("bqhgd,bkhd->bhgqk", q_blk, k).astype(jnp.float32) * scale
        p = jax.nn.softmax(s, axis=-1).astype(v.dtype)
        return jnp.einsum("bhgqk,bkhd->bqhgd", p, v)

    out = lax.map(one_block, qb)
    return out.transpose(1, 0, 2, 3, 4, 5).reshape(B, Tq, ATTN_WIDTH)


def centred_depthwise_conv(u, w, b):
    left = CONV_WIDTH // 2
    out = lax.conv_general_dilated(
        u, w[:, None, :].astype(u.dtype), window_strides=(1,),
        padding=[(left, CONV_WIDTH - 1 - left)],
        dimension_numbers=("NWC", "WIO", "NWC"), feature_group_count=u.shape[-1])
    return out + b


def linear_scan(a, b):
    def combine(left, right):
        a_l, b_l = left
        a_r, b_r = right
        return a_l * a_r, a_r * b_l + b_r
    _, h = lax.associative_scan(combine, (a, b), axis=1)
    return h


def rglru_direction(u, h0, w_a, b_a, w_x, b_x, lam, reverse):
    B, T, W = u.shape
    ub = u.reshape(B, T, LRU_BLOCKS, LRU_BLOCK)
    r = jax.nn.sigmoid(jnp.einsum("btnd,nde->btne", ub, w_a).reshape(B, T, W) + b_a)
    i = jax.nn.sigmoid(jnp.einsum("btnd,nde->btne", ub, w_x).reshape(B, T, W) + b_x)
    log_a = (-LRU_C * r.astype(jnp.float32)) * jax.nn.softplus(-lam.astype(jnp.float32))
    a = jnp.exp(log_a)
    b = jnp.sqrt(-jnp.expm1(2.0 * log_a)) * (i * u).astype(jnp.float32)
    if reverse:
        a, b = a[:, ::-1], b[:, ::-1]
    b = b.at[:, 0].add(a[:, 0] * h0)
    h = linear_scan(a, b)
    if reverse:
        h = h[:, ::-1]
    return h


def parallel_mixers(h, hc, p, cos, sin, last):
    B, S, _ = h.shape
    C = hc.shape[1]
    cuts = [ATTN_WIDTH, ATTN_WIDTH + KV_WIDTH, ATTN_WIDTH + 2 * KV_WIDTH,
            ATTN_WIDTH + 2 * KV_WIDTH + LRU_WIDTH]
    q, k, v, u, gb = jnp.split(h @ p["w_in"], cuts, axis=-1)
    qc, kc, vc, uc, gbc = jnp.split(hc @ p["w_in"], cuts, axis=-1)

    q = apply_axial_rope(rms_norm(q.reshape(B, S, N_Q_HEADS, HEAD_DIM), p["q_norm_g"]), cos, sin)
    k = apply_axial_rope(rms_norm(k.reshape(B, S, N_KV_HEADS, HEAD_DIM), p["k_norm_g"]), cos, sin)
    v = v.reshape(B, S, N_KV_HEADS, HEAD_DIM)
    kc = rms_norm(kc.reshape(B, C, N_KV_HEADS, HEAD_DIM), p["k_norm_g"])
    vc = vc.reshape(B, C, N_KV_HEADS, HEAD_DIM)
    k_all = jnp.concatenate([kc, k], axis=1)
    v_all = jnp.concatenate([vc, v], axis=1)
    attn = blocked_attention(q, k_all, v_all)

    u = centred_depthwise_conv(u, p["conv_w"], p["conv_b"])
    uc = centred_depthwise_conv(uc, p["conv_w"], p["conv_b"])
    zeros = jnp.zeros((B, LRU_WIDTH), jnp.float32)
    h_lat = 0.0
    h_ctx = 0.0
    for d, reverse in enumerate((False, True)):
        args = (p["lru_wa"][d], p["lru_ba"][d], p["lru_wx"][d], p["lru_bx"][d], p["lru_lambda"][d], reverse)
        hc_d = rglru_direction(uc, zeros, *args)
        h0 = hc_d[:, 0] if reverse else hc_d[:, -1]
        h_lat = h_lat + rglru_direction(u, h0, *args)
        h_ctx = h_ctx + hc_d
    rec = h_lat.astype(h.dtype) * jax.nn.gelu(gb, approximate=True)
    mix = jnp.concatenate([rms_norm(attn, p["attn_out_g"]), rms_norm(rec, p["lru_out_g"])], axis=-1)
    if last:
        return mix, None

    qc = rms_norm(qc.reshape(B, C, N_Q_HEADS, HEAD_DIM), p["q_norm_g"])
    attn_c = blocked_attention(qc, kc, vc)
    rec_c = h_ctx.astype(hc.dtype) * jax.nn.gelu(gbc, approximate=True)
    mix_c = jnp.concatenate([rms_norm(attn_c, p["attn_out_g"]), rms_norm(rec_c, p["lru_out_g"])], axis=-1)
    return mix, mix_c


def grouped_moe(h, router_w, router_b, w_gate, w_up, w_down):
    N = h.shape[0]
    scores = jax.nn.sigmoid(h.astype(jnp.float32) @ router_w.astype(jnp.float32))
    biased = scores + router_b.astype(jnp.float32)
    grouped = biased.reshape(N, N_EXPERT_GROUPS, EXPERTS_PER_GROUP)
    group_score = lax.top_k(grouped, TOP_K)[0].sum(-1)
    best_group = jnp.argmax(group_score, axis=-1)
    in_group = (jnp.arange(N_EXPERTS) // EXPERTS_PER_GROUP)[None, :] == best_group[:, None]
    masked = jnp.where(in_group, biased, -jnp.inf)
    _, top_idx = lax.top_k(masked, TOP_K)
    top_w = jnp.take_along_axis(scores, top_idx, axis=-1)
    top_w = top_w / jnp.sum(top_w, axis=-1, keepdims=True)
    combine = jnp.einsum("nk,nke->ne", top_w,
                         jax.nn.one_hot(top_idx, N_EXPERTS, dtype=jnp.float32)).astype(h.dtype)
    out = jnp.zeros_like(h)
    for e in range(N_EXPERTS):
        hid = jax.nn.silu(h @ w_gate[e]) * (h @ w_up[e])
        out = out + combine[:, e:e + 1] * (hid @ w_down[e])
    return out


def hybrid_layer(x, xc, mod, mod_c, p, router_w, router_b, cos, sin, last):
    B, S, D = x.shape
    C = xc.shape[1]
    sh1, sc1, g1, sh2, sc2, g2 = jnp.split(mod[:, None, :], N_MOD, axis=-1)
    csh1, csc1, cg1, csh2, csc2, cg2 = jnp.split(mod_c, N_MOD, axis=-1)

    h = modulate(rms_norm(x, p["norm1_g"]), sh1, sc1)
    hc = modulate(rms_norm(xc, p["norm1_g"]), csh1, csc1)
    mix, mix_c = parallel_mixers(h, hc, p, cos, sin, last)
    x = x + g1 * (mix @ p["w_out"])
    h2 = modulate(rms_norm(x, p["norm2_g"]), sh2, sc2)
    moe_w = (router_w, router_b, p["w_gate"], p["w_up"], p["w_down"])
    if last:
        y = grouped_moe(h2.reshape(B * S, D), *moe_w).reshape(B, S, D)
        return x + g2 * y, xc

    xc = xc + cg1 * (mix_c @ p["w_out"])
    hc2 = modulate(rms_norm(xc, p["norm2_g"]), csh2, csc2)
    tokens = jnp.concatenate([h2.reshape(B * S, D), hc2.reshape(B * C, D)], axis=0)
    y = grouped_moe(tokens, *moe_w)
    x = x + g2 * y[:B * S].reshape(B, S, D)
    xc = xc + cg2 * y[B * S:].reshape(B, C, D)
    return x, xc


def setup_inputs(seed: int = 0) -> dict:
    key = jax.random.key(seed)
    ks = jax.random.split(key, 32)
    f32 = jnp.float32
    D = D_MODEL

    def nrm(k, shape, scale):
        return jax.random.normal(k, shape, f32) * scale

    a_pow = jax.random.uniform(ks[16], (DEPTH, N_DIRS, LRU_WIDTH), f32, minval=0.9, maxval=0.999)
    a_base = a_pow ** (1.0 / LRU_C)
    lru_lambda = jnp.log(a_base) - jnp.log1p(-a_base)
    return {
        "x": nrm(ks[0], (BATCH, SEQ, D), 1.0),
        "c": nrm(ks[1], (BATCH, D), 1.0),
        "ctx": nrm(ks[2], (BATCH, CTX_LEN, D), 1.0),
        "c_ctx": nrm(ks[3], (D,), 1.0),
        "ada_w": nrm(ks[4], (DEPTH, D, N_MOD * D), 0.5 * D ** -0.5),
        "ada_b": nrm(ks[5], (DEPTH, N_MOD * D), 0.02),
        "norm1_g": 1.0 + nrm(ks[6], (DEPTH, D), 0.02),
        "w_in": nrm(ks[7], (DEPTH, D, IN_WIDTH), D ** -0.5),
        "q_norm_g": 1.0 + nrm(ks[8], (DEPTH, HEAD_DIM), 0.02),
        "k_norm_g": 1.0 + nrm(ks[9], (DEPTH, HEAD_DIM), 0.02),
        "conv_w": nrm(ks[10], (DEPTH, CONV_WIDTH, LRU_WIDTH), CONV_WIDTH ** -0.5),
        "conv_b": nrm(ks[11], (DEPTH, LRU_WIDTH), 0.02),
        "lru_wa": nrm(ks[12], (DEPTH, N_DIRS, LRU_BLOCKS, LRU_BLOCK, LRU_BLOCK), LRU_BLOCK ** -0.5),
        "lru_ba": nrm(ks[13], (DEPTH, N_DIRS, LRU_WIDTH), 0.1),
        "lru_wx": nrm(ks[14], (DEPTH, N_DIRS, LRU_BLOCKS, LRU_BLOCK, LRU_BLOCK), LRU_BLOCK ** -0.5),
        "lru_bx": nrm(ks[15], (DEPTH, N_DIRS, LRU_WIDTH), 0.1),
        "lru_lambda": lru_lambda,
        "attn_out_g": 1.0 + nrm(ks[17], (DEPTH, ATTN_WIDTH), 0.02),
        "lru_out_g": 1.0 + nrm(ks[18], (DEPTH, LRU_WIDTH), 0.02),
        "w_out": nrm(ks[19], (DEPTH, MIX_WIDTH, D), MIX_WIDTH ** -0.5),
        "norm2_g": 1.0 + nrm(ks[20], (DEPTH, D), 0.02),
        "router_w": nrm(ks[21], (D, N_EXPERTS), D ** -0.5),
        "router_b": nrm(ks[22], (N_EXPERTS,), 0.01),
        "exp_w_gate": nrm(ks[23], (DEPTH, N_EXPERTS, D, D_EXPERT), D ** -0.5),
        "exp_w_up": nrm(ks[24], (DEPTH, N_EXPERTS, D, D_EXPERT), D ** -0.5),
        "exp_w_down": nrm(ks[25], (DEPTH, N_EXPERTS, D_EXPERT, D), D_EXPERT ** -0.5),
        "final_g": 1.0 + nrm(ks[26], (D,), 0.02),
    }


def reference(x, c, ctx, c_ctx, ada_w, ada_b, norm1_g, w_in, q_norm_g, k_norm_g,
              conv_w, conv_b, lru_wa, lru_ba, lru_wx, lru_bx, lru_lambda,
              attn_out_g, lru_out_g, w_out, norm2_g, router_w, router_b,
              exp_w_gate, exp_w_up, exp_w_down, final_g):
    cos, sin = axial_rope_tables(x.shape[1])
    silu_c = jax.nn.silu(c)
    silu_cc = jax.nn.silu(c_ctx)
    xc = ctx
    for l in range(DEPTH):
        mod = silu_c @ ada_w[l] + ada_b[l]
        mod_c = silu_cc @ ada_w[l] + ada_b[l]
        p = {
            "norm1_g": norm1_g[l], "w_in": w_in[l], "q_norm_g": q_norm_g[l], "k_norm_g": k_norm_g[l],
            "conv_w": conv_w[l], "conv_b": conv_b[l], "lru_wa": lru_wa[l], "lru_ba": lru_ba[l],
            "lru_wx": lru_wx[l], "lru_bx": lru_bx[l], "lru_lambda": lru_lambda[l],
            "attn_out_g": attn_out_g[l], "lru_out_g": lru_out_g[l], "w_out": w_out[l],
            "norm2_g": norm2_g[l], "w_gate": exp_w_gate[l], "w_up": exp_w_up[l], "w_down": exp_w_down[l],
        }
        x, xc = hybrid_layer(x, xc, mod, mod_c, p, router_w, router_b, cos, sin, l == DEPTH - 1)
    return rms_norm(x, final_g)
```

```python
import functools

import jax
import jax.numpy as jnp
from jax import lax
from jax.experimental import pallas as pl
from jax.experimental.pallas import tpu as pltpu

F32 = jnp.float32
BF16 = jnp.bfloat16

HEAD_DIM = 64
N_Q_HEADS = 8
N_KV_HEADS = 2
ATTN_WIDTH = N_Q_HEADS * HEAD_DIM
KV_WIDTH = N_KV_HEADS * HEAD_DIM
LRU_WIDTH = 512
LRU_BLOCKS = 8
CONV_WIDTH = 4
CONV_LEFT = CONV_WIDTH // 2
LRU_C = 8.0
GRID_W = 64
ROPE_THETA = 10000.0
N_EXPERTS = 16
N_GROUPS = 4
GROUP_SIZE = N_EXPERTS // N_GROUPS
N_MOD = 6
NORM_EPS = 1e-6
ATTN_SCALE = HEAD_DIM ** -0.5

LANES = 128
SUBLANES = 8
ROW_TILE = 256
MOE_ROW_TILE = 1024
ROUTE_ROWS = 24
VMEM_LIMIT = 60000 * 1024


def _params(sem):
    return pltpu.CompilerParams(dimension_semantics=sem, vmem_limit_bytes=VMEM_LIMIT)


def _sigmoid(x):
    return 1.0 / (1.0 + jnp.exp(-x))


def _rms(x):
    return x * lax.rsqrt(jnp.mean(x * x, axis=-1, keepdims=True) + NORM_EPS)


def _split_bf16(x):
    hi = x.astype(BF16)
    lo = (x - hi.astype(F32)).astype(BF16)
    return hi, lo


def _ada_kernel(c_ref, w_ref, b_ref, o_ref):
    cv = c_ref[...]
    s = cv * _sigmoid(cv)
    o_ref[...] = jnp.dot(s.astype(BF16), w_ref[...].astype(BF16),
                         preferred_element_type=F32) + b_ref[...]


def _ada_mods(cvec, ada_w, ada_b):
    depth, d, n = ada_w.shape
    tn = 512
    rows = cvec.shape[0]
    return pl.pallas_call(
        _ada_kernel,
        grid=(depth, n // tn),
        in_specs=[
            pl.BlockSpec((rows, d), lambda l, j: (0, 0)),
            pl.BlockSpec((None, d, tn), lambda l, j: (l, 0, j)),
            pl.BlockSpec((None, 1, tn), lambda l, j: (l, 0, j)),
        ],
        out_specs=pl.BlockSpec((None, rows, tn), lambda l, j: (l, 0, j)),
        out_shape=jax.ShapeDtypeStruct((depth, rows, n), F32),
        compiler_params=_params(("parallel", "parallel")),
        name="ada_mods",
    )(cvec, ada_w, ada_b.reshape(depth, 1, n))


def _head_rms(t, g_mat, gain):
    hi, lo = _split_bf16(t * t)
    m = (jnp.dot(hi, g_mat, preferred_element_type=F32)
         + jnp.dot(lo, g_mat, preferred_element_type=F32))
    return t * lax.rsqrt(m + NORM_EPS) * gain


def _rope(t, cos_t, sin_t):
    width = t.shape[-1]
    reps = width // LANES
    cos_w = jnp.concatenate([cos_t] * reps, axis=1)
    sin_w = jnp.concatenate([sin_t] * reps, axis=1)
    lane = lax.broadcasted_iota(jnp.int32, t.shape, 1)
    quarter = HEAD_DIM // 4
    first = (lane % (2 * quarter)) < quarter
    partner = jnp.where(first, pltpu.roll(t, width - quarter, 1), pltpu.roll(t, quarter, 1))
    return t * cos_w + partner * sin_w


def _inproj_kernel(x_ref, mod_ref, n1_ref, w_ref, cos_ref, sin_ref, gq_ref, gk_ref, gm_ref,
                   q_ref, k_ref, v_ref, u_ref, gg_ref):
    x = x_ref[...]
    h = _rms(x) * n1_ref[...]
    h = h * (1.0 + mod_ref[1:2, :]) + mod_ref[0:1, :]
    y = jnp.dot(h.astype(BF16), w_ref[...], preferred_element_type=F32)
    kd = 2 * KV_WIDTH
    o_k, o_v, o_u, o_g = ATTN_WIDTH, ATTN_WIDTH + kd, ATTN_WIDTH + 2 * kd, ATTN_WIDTH + 2 * kd + LRU_WIDTH
    cos_t = cos_ref[...]
    sin_t = sin_ref[...]
    g_mat = gm_ref[...]
    q = _rope(_head_rms(y[:, :o_k], g_mat, gq_ref[...]), cos_t, sin_t)
    q_ref[...] = (q * ATTN_SCALE).astype(BF16)
    k = _rope(_head_rms(y[:, o_k:o_v], g_mat[:kd, :kd], gk_ref[...]), cos_t, sin_t)
    k_ref[...] = k.astype(BF16)
    v_ref[...] = y[:, o_v:o_u].astype(BF16)
    u_ref[...] = y[:, o_u:o_g]
    gb = y[:, o_g:]
    gg_ref[...] = 0.5 * gb * (1.0 + jnp.tanh(0.7978845608028654 * (gb + 0.044715 * gb * gb * gb)))


def _inproj(xa, mods, n1g, w_ext, cos_t, sin_t, gq, gk, g_mat):
    b, l, d = xa.shape
    nt = l // ROW_TILE
    n_ext = w_ext.shape[1]
    kd = 2 * KV_WIDTH
    row = lambda w: pl.BlockSpec((None, ROW_TILE, w), lambda i, t: (i, t, 0))
    const = lambda shape: pl.BlockSpec(shape, lambda i, t: (0,) * len(shape))
    return pl.pallas_call(
        _inproj_kernel,
        grid=(b, nt),
        in_specs=[
            row(d),
            pl.BlockSpec((None, N_MOD, d), lambda i, t: (jnp.where(t == 0, b, i), 0, 0)),
            const((1, d)),
            const((d, n_ext)),
            pl.BlockSpec((ROW_TILE, LANES), lambda i, t: (t, 0)),
            pl.BlockSpec((ROW_TILE, LANES), lambda i, t: (t, 0)),
            const((1, ATTN_WIDTH)),
            const((1, kd)),
            const((ATTN_WIDTH, ATTN_WIDTH)),
        ],
        out_specs=[row(ATTN_WIDTH), row(kd), row(kd), row(LRU_WIDTH), row(LRU_WIDTH)],
        out_shape=[
            jax.ShapeDtypeStruct((b, l, ATTN_WIDTH), BF16),
            jax.ShapeDtypeStruct((b, l, kd), BF16),
            jax.ShapeDtypeStruct((b, l, kd), BF16),
            jax.ShapeDtypeStruct((b, l, LRU_WIDTH), F32),
            jax.ShapeDtypeStruct((b, l, LRU_WIDTH), F32),
        ],
        compiler_params=_params(("parallel", "parallel")),
        name="in_proj",
    )(xa, mods, n1g, w_ext, cos_t, sin_t, gq, gk, g_mat)


def _attn_kernel(q_ref, k_ref, v_ref, g_ref, o_ref, *, n_ctx):
    t = pl.program_id(1)
    rows = q_ref.shape[0]
    n_all = k_ref.shape[0]

    def run(n_keys):
        lane = lax.broadcasted_iota(jnp.int32, (rows, LANES), 1)
        low = lane < HEAD_DIM
        outs = []
        for pair in range(N_Q_HEADS // 2):
            kv = (2 * pair) // (N_Q_HEADS // N_KV_HEADS)
            qp = q_ref[:, pair * LANES:(pair + 1) * LANES]
            zero = jnp.zeros_like(qp)
            q2 = jnp.concatenate([jnp.where(low, qp, zero), jnp.where(low, zero, qp)], axis=0)
            kd = k_ref[0:n_keys, kv * LANES:(kv + 1) * LANES]
            vd = v_ref[0:n_keys, kv * LANES:(kv + 1) * LANES]
            s = lax.dot_general(q2, kd, (((1,), (1,)), ((), ())), preferred_element_type=F32)
            p = jnp.exp(s - jnp.max(s, axis=-1, keepdims=True))
            den = jnp.sum(p, axis=-1, keepdims=True)
            o = jnp.dot(p.astype(BF16), vd, preferred_element_type=F32) * (1.0 / den)
            outs.append(jnp.where(low, o[:rows], o[rows:]))
        a = jnp.concatenate(outs, axis=1)
        o_ref[...] = (_rms(a) * g_ref[...]).astype(BF16)

    @pl.when(t == 0)
    def _():
        run(n_ctx)

    @pl.when(t > 0)
    def _():
        run(n_all)


def _attention(q, kd, vd, gain, n_ctx):
    b, l, w = q.shape
    nt = l // ROW_TILE
    kw = kd.shape[-1]
    return pl.pallas_call(
        functools.partial(_attn_kernel, n_ctx=n_ctx),
        grid=(b, nt),
        in_specs=[
            pl.BlockSpec((None, ROW_TILE, w), lambda i, t: (i, t, 0)),
            pl.BlockSpec((None, l, kw), lambda i, t: (i, 0, 0)),
            pl.BlockSpec((None, l, kw), lambda i, t: (i, 0, 0)),
            pl.BlockSpec((1, w), lambda i, t: (0, 0)),
        ],
        out_specs=pl.BlockSpec((None, ROW_TILE, w), lambda i, t: (i, t, 0)),
        out_shape=jax.ShapeDtypeStruct((b, l, w), BF16),
        compiler_params=_params(("parallel", "parallel")),
        name="attention",
    )(q, kd, vd, gain)


def _lru_kernel(u_ref, gg_ref, cw_ref, cb_ref, wg_ref, bg_ref, lam_ref, og_ref, o_ref,
                upad, a_f, b_f, a_r, b_r, *, n_ctx):
    l, w = u_ref.shape
    n_grp = w // LANES
    pad = SUBLANES
    zeros_pad = jnp.zeros((pad, w), F32)
    upad[0:pad, :] = zeros_pad
    upad[pad + l:2 * pad + l, :] = zeros_pad
    for r0 in range(0, l, ROW_TILE):
        upad[pad + r0:pad + r0 + ROW_TILE, :] = u_ref[r0:r0 + ROW_TILE, :]

    neg_lam = -lam_ref[...]
    softplus = jnp.maximum(neg_lam, 0.0) + jnp.log1p(jnp.exp(-jnp.abs(neg_lam)))
    cw = cw_ref[...]
    cb = cb_ref[...]

    for r0 in range(0, l, ROW_TILE):
        row = r0 + lax.broadcasted_iota(jnp.int32, (ROW_TILE, w), 0)
        is_lat = row >= n_ctx
        uc = jnp.zeros((ROW_TILE, w), F32) + cb
        for j in range(CONV_WIDTH):
            off = j - CONV_LEFT
            tap = upad[pad + r0 + off:pad + r0 + off + ROW_TILE, :]
            if off != 0:
                tap = jnp.where(((row + off) >= n_ctx) == is_lat, tap, 0.0)
            uc = uc + tap * cw[j:j + 1, :]
        z = jnp.dot(uc.astype(BF16), wg_ref[...], preferred_element_type=F32) + bg_ref[...]
        for d, (a_ref, b_ref) in enumerate(((a_f, b_f), (a_r, b_r))):
            base = 2 * d * w
            r_gate = _sigmoid(z[:, base:base + w])
            i_gate = _sigmoid(z[:, base + w:base + 2 * w])
            log_a = (-LRU_C * r_gate) * softplus[d:d + 1, :]
            a = jnp.exp(log_a)
            one_minus_a2 = -jnp.tanh(log_a) * (a * a + 1.0)
            bb = jnp.sqrt(one_minus_a2) * (i_gate * uc)
            for k in range(n_grp):
                a_ref[k, r0:r0 + ROW_TILE, :] = a[:, k * LANES:(k + 1) * LANES]
                b_ref[k, r0:r0 + ROW_TILE, :] = bb[:, k * LANES:(k + 1) * LANES]

    def scan_segment(base, n, h_in_f, h_in_r):
        def body(j, carry):
            hf, pf, hr, pr = carry
            idx_f = pl.ds(base + j, SUBLANES, stride=n)
            idx_r = pl.ds(base + (n - 1 - j), SUBLANES, stride=n)
            out = ([], [], [], [])
            for k in range(n_grp):
                a = a_f[k, idx_f, :]
                h = a * hf[k] + b_f[k, idx_f, :]
                p = a * pf[k]
                b_f[k, idx_f, :] = h
                a_f[k, idx_f, :] = p
                out[0].append(h)
                out[1].append(p)
                a = a_r[k, idx_r, :]
                h = a * hr[k] + b_r[k, idx_r, :]
                p = a * pr[k]
                b_r[k, idx_r, :] = h
                a_r[k, idx_r, :] = p
                out[2].append(h)
                out[3].append(p)
            return tuple(tuple(o) for o in out)

        zero = (jnp.zeros((SUBLANES, LANES), F32),) * n_grp
        one = (jnp.ones((SUBLANES, LANES), F32),) * n_grp
        hf, pf, hr, pr = lax.fori_loop(0, n, body, (zero, one, zero, one), unroll=2)
        hf, pf, hr, pr = (jnp.concatenate(t, axis=1) for t in (hf, pf, hr, pr))
        carry_f = [h_in_f]
        for c in range(SUBLANES):
            carry_f.append(hf[c:c + 1, :] + pf[c:c + 1, :] * carry_f[c])
        carry_r = [None] * SUBLANES + [h_in_r]
        for c in range(SUBLANES - 1, -1, -1):
            carry_r[c] = hr[c:c + 1, :] + pr[c:c + 1, :] * carry_r[c + 1]
        gain = og_ref[...]
        wide = lambda ref, rows: jnp.concatenate([ref[k, rows, :] for k in range(n_grp)], axis=1)
        for c in range(SUBLANES):
            rows = slice(base + c * n, base + (c + 1) * n)
            h = (wide(b_f, rows) + wide(a_f, rows) * carry_f[c]
                 + wide(b_r, rows) + wide(a_r, rows) * carry_r[c + 1])
            o_ref[rows, :] = (_rms(h * gg_ref[rows, :]) * gain).astype(BF16)
        return carry_f[SUBLANES], carry_r[0]

    zero_row = jnp.zeros((1, w), F32)
    ctx_f, ctx_r = scan_segment(0, n_ctx // SUBLANES, zero_row, zero_row)
    scan_segment(n_ctx, (l - n_ctx) // SUBLANES, ctx_f, ctx_r)


def _lru(u, gg, conv_w, conv_b, w_gate, b_gate, lam, out_g, n_ctx):
    b, l, w = u.shape
    const = lambda shape: pl.BlockSpec(shape, lambda i: (0,) * len(shape))
    seq = pl.BlockSpec((None, l, w), lambda i: (i, 0, 0))
    return pl.pallas_call(
        functools.partial(_lru_kernel, n_ctx=n_ctx),
        grid=(b,),
        in_specs=[seq, seq, const(conv_w.shape), const((1, w)), const(w_gate.shape),
                  const(b_gate.shape), const(lam.shape), const((1, w))],
        out_specs=seq,
        out_shape=jax.ShapeDtypeStruct((b, l, w), BF16),
        scratch_shapes=[pltpu.VMEM((l + 2 * SUBLANES, w), F32)] + [pltpu.VMEM((w // LANES, l, LANES), F32)] * 4,
        compiler_params=_params(("parallel",)),
        name="rg_lru",
    )(u, gg, conv_w, conv_b, w_gate, b_gate, lam, out_g)


def _route(scores, biased):
    group_score = []
    for g in range(N_GROUPS):
        v = biased[g * GROUP_SIZE:(g + 1) * GROUP_SIZE]
        best = None
        for i in range(GROUP_SIZE):
            for j in range(i + 1, GROUP_SIZE):
                pair = v[i] + v[j]
                best = pair if best is None else jnp.maximum(best, pair)
        group_score.append(best)
    gid = jnp.zeros_like(group_score[0], dtype=jnp.int32)
    gbest = group_score[0]
    for g in range(1, N_GROUPS):
        upd = group_score[g] > gbest
        gid = jnp.where(upd, g, gid)
        gbest = jnp.where(upd, group_score[g], gbest)

    def pick(rows, j):
        out = rows[j]
        for g in range(1, N_GROUPS):
            out = jnp.where(gid == g, rows[g * GROUP_SIZE + j], out)
        return out

    v = [pick(biased, j) for j in range(GROUP_SIZE)]
    s = [pick(scores, j) for j in range(GROUP_SIZE)]
    i1 = jnp.zeros_like(gid)
    m1 = v[0]
    for j in range(1, GROUP_SIZE):
        upd = v[j] > m1
        i1 = jnp.where(upd, j, i1)
        m1 = jnp.where(upd, v[j], m1)
    i2 = jnp.zeros_like(gid)
    m2 = jnp.full_like(m1, -jnp.inf)
    for j in range(GROUP_SIZE):
        upd = (i1 != j) & (v[j] > m2)
        i2 = jnp.where(upd, j, i2)
        m2 = jnp.where(upd, v[j], m2)
    w1 = s[0]
    w2 = s[0]
    for j in range(1, GROUP_SIZE):
        w1 = jnp.where(i1 == j, s[j], w1)
        w2 = jnp.where(i2 == j, s[j], w2)
    den = w1 + w2
    w1 = w1 / den
    w2 = w2 / den
    combine = []
    for e in range(N_EXPERTS):
        g, j = divmod(e, GROUP_SIZE)
        cj = jnp.where(i1 == j, w1, jnp.where(i2 == j, w2, 0.0))
        combine.append(jnp.where(gid == g, cj, 0.0))
    return combine, gid


def _outproj_kernel(x_ref, a_ref, r_ref, mod_ref, w_ref, n2_ref, rwh_ref, rwl_ref, rb_ref,
                    x1_ref, h2_ref, route_ref):
    half = a_ref.shape[-1]
    mix = (jnp.dot(a_ref[...], w_ref[0:half, :], preferred_element_type=F32)
           + jnp.dot(r_ref[...], w_ref[half:, :], preferred_element_type=F32))
    x1 = x_ref[...] + mod_ref[2:3, :] * mix
    x1_ref[...] = x1
    h2 = _rms(x1) * n2_ref[...]
    h2 = h2 * (1.0 + mod_ref[4:5, :]) + mod_ref[3:4, :]
    h2_ref[...] = h2.astype(BF16)
    hi, lo = _split_bf16(h2)
    nt = (((1,), (1,)), ((), ()))
    logits = (lax.dot_general(rwh_ref[...], hi, nt, preferred_element_type=F32)
              + lax.dot_general(rwh_ref[...], lo, nt, preferred_element_type=F32)
              + lax.dot_general(rwl_ref[...], hi, nt, preferred_element_type=F32))
    scores = _sigmoid(logits)
    biased = scores + rb_ref[...]
    combine, gid = _route([scores[e:e + 1, :] for e in range(N_EXPERTS)],
                          [biased[e:e + 1, :] for e in range(N_EXPERTS)])
    for e in range(N_EXPERTS):
        route_ref[e:e + 1, :] = combine[e]
    route_ref[N_EXPERTS:N_EXPERTS + 1, :] = gid.astype(F32)
    route_ref[N_EXPERTS + 1:, :] = jnp.zeros((ROUTE_ROWS - N_EXPERTS - 1, x_ref.shape[0]), F32)


def _outproj(xa, attn_n, rec_n, mods, w_out, n2g, rw_hi, rw_lo, rb):
    b, l, d = xa.shape
    nt = l // ROW_TILE
    half = attn_n.shape[-1]
    row = lambda w: pl.BlockSpec((None, ROW_TILE, w), lambda i, t: (i, t, 0))
    const = lambda shape: pl.BlockSpec(shape, lambda i, t: (0,) * len(shape))
    return pl.pallas_call(
        _outproj_kernel,
        grid=(b, nt),
        in_specs=[
            row(d), row(half), row(half),
            pl.BlockSpec((None, N_MOD, d), lambda i, t: (jnp.where(t == 0, b, i), 0, 0)),
            const(w_out.shape), const((1, d)), const(rw_hi.shape), const(rw_lo.shape), const(rb.shape),
        ],
        out_specs=[row(d), row(d),
                   pl.BlockSpec((None, None, ROUTE_ROWS, ROW_TILE), lambda i, t: (i, t, 0, 0))],
        out_shape=[
            jax.ShapeDtypeStruct((b, l, d), F32),
            jax.ShapeDtypeStruct((b, l, d), BF16),
            jax.ShapeDtypeStruct((b, nt, ROUTE_ROWS, ROW_TILE), F32),
        ],
        compiler_params=_params(("parallel", "parallel")),
        name="out_proj_router",
    )(xa, attn_n, rec_n, mods, w_out, n2g, rw_hi, rw_lo, rb)


def _moe_dense_kernel(h_ref, c_ref, wgu_ref, wd_ref, y_ref, acc_ref):
    e = pl.program_id(1)

    @pl.when(e == 0)
    def _():
        acc_ref[...] = jnp.zeros_like(acc_ref)

    d_exp = wd_ref.shape[0]
    gu = jnp.dot(h_ref[...], wgu_ref[...], preferred_element_type=F32)
    gate = gu[:, :d_exp]
    hid = gate * _sigmoid(gate) * gu[:, d_exp:]
    c = c_ref[...]
    lane = lax.broadcasted_iota(jnp.int32, c.shape, 1)
    ce = jnp.sum(jnp.where(lane == e, c, 0.0), axis=-1, keepdims=True)
    acc_ref[...] += ce * jnp.dot(hid.astype(BF16), wd_ref[...], preferred_element_type=F32)

    @pl.when(e == pl.num_programs(1) - 1)
    def _():
        y_ref[...] = acc_ref[...]


def _moe_dense(h2, combine, w_gu, w_d):
    n, d = h2.shape
    n_exp, _, two_de = w_gu.shape
    tm = MOE_ROW_TILE
    return pl.pallas_call(
        _moe_dense_kernel,
        grid=(n // tm, n_exp),
        in_specs=[
            pl.BlockSpec((tm, d), lambda i, e: (i, 0)),
            pl.BlockSpec((tm, combine.shape[1]), lambda i, e: (i, 0)),
            pl.BlockSpec((None, d, two_de), lambda i, e: (e, 0, 0)),
            pl.BlockSpec((None, two_de // 2, d), lambda i, e: (e, 0, 0)),
        ],
        out_specs=pl.BlockSpec((tm, d), lambda i, e: (i, 0)),
        out_shape=jax.ShapeDtypeStruct((n, d), F32),
        scratch_shapes=[pltpu.VMEM((tm, d), F32)],
        compiler_params=_params(("parallel", "arbitrary")),
        name="moe_dense",
    )(h2, combine, w_gu, w_d)


def _residual_kernel(x_ref, y_ref, mod_ref, o_ref):
    o_ref[...] = x_ref[...] + mod_ref[5:6, :] * y_ref[...]


def _final_kernel(x_ref, y_ref, mod_ref, g_ref, o_ref):
    o_ref[...] = _rms(x_ref[...] + mod_ref[5:6, :] * y_ref[...]) * g_ref[...]


def _residual(x1, y, mods):
    b, l, d = x1.shape
    row = pl.BlockSpec((None, ROW_TILE, d), lambda i, t: (i, t, 0))
    return pl.pallas_call(
        _residual_kernel,
        grid=(b, l // ROW_TILE),
        in_specs=[row, row,
                  pl.BlockSpec((None, N_MOD, d), lambda i, t: (jnp.where(t == 0, b, i), 0, 0))],
        out_specs=row,
        out_shape=jax.ShapeDtypeStruct((b, l, d), F32),
        compiler_params=_params(("parallel", "parallel")),
        name="moe_residual",
    )(x1, y, mods)


def _final(x1, y, mods, final_g, n_ctx):
    b, l, d = x1.shape
    skip = n_ctx // ROW_TILE
    row = pl.BlockSpec((None, ROW_TILE, d), lambda i, t: (i, t + skip, 0))
    return pl.pallas_call(
        _final_kernel,
        grid=(b, (l - n_ctx) // ROW_TILE),
        in_specs=[row, row,
                  pl.BlockSpec((None, N_MOD, d), lambda i, t: (i, 0, 0)),
                  pl.BlockSpec((1, d), lambda i, t: (0, 0))],
        out_specs=pl.BlockSpec((None, ROW_TILE, d), lambda i, t: (i, t, 0)),
        out_shape=jax.ShapeDtypeStruct((b, l - n_ctx, d), F32),
        compiler_params=_params(("parallel", "parallel")),
        name="final_norm",
    )(x1, y, mods, final_g)


def _rope_tables(n_ctx, n_lat):
    rows = n_lat // GRID_W
    r, col = jnp.meshgrid(jnp.arange(rows), jnp.arange(GRID_W), indexing="ij")
    r = r.reshape(-1).astype(F32)
    col = col.reshape(-1).astype(F32)
    half = HEAD_DIM // 2
    inv = ROPE_THETA ** (-jnp.arange(0, half, 2, dtype=F32) / half)
    ang_r = r[:, None] * inv
    ang_c = col[:, None] * inv
    cos_h = jnp.concatenate([jnp.cos(ang_r)] * 2 + [jnp.cos(ang_c)] * 2, axis=-1)
    sin_h = jnp.concatenate([-jnp.sin(ang_r), jnp.sin(ang_r), -jnp.sin(ang_c), jnp.sin(ang_c)], axis=-1)
    cos_t = jnp.concatenate([jnp.ones((n_ctx, HEAD_DIM), F32), cos_h], axis=0)
    sin_t = jnp.concatenate([jnp.zeros((n_ctx, HEAD_DIM), F32), sin_h], axis=0)
    reps = LANES // HEAD_DIM
    return jnp.tile(cos_t, (1, reps)), jnp.tile(sin_t, (1, reps))


def _block_diag(w):
    n, d, e = w.shape
    eye = jnp.eye(n, dtype=w.dtype)
    return (w[:, :, None, :] * eye[:, None, :, None]).reshape(n * d, n * e)


def kernel(x, c, ctx, c_ctx, ada_w, ada_b, norm1_g, w_in, q_norm_g, k_norm_g, conv_w, conv_b, lru_wa, lru_ba, lru_wx, lru_bx, lru_lambda, attn_out_g, lru_out_g, w_out, norm2_g, router_w, router_b, exp_w_gate, exp_w_up, exp_w_down, final_g):
    b, s, d = x.shape
    n_ctx = ctx.shape[1]
    l = n_ctx + s
    depth = ada_w.shape[0]
    nt = l // ROW_TILE
    assert n_ctx == ROW_TILE and s % ROW_TILE == 0 and MOE_ROW_TILE % ROW_TILE == 0

    xa = jnp.concatenate([ctx, x], axis=1)
    cvec = jnp.zeros((2 * SUBLANES, d), F32).at[:b].set(c).at[b].set(c_ctx)
    mods_all = _ada_mods(cvec, ada_w, ada_b)[:, :b + 1].reshape(depth, b + 1, N_MOD, d)
    cos_t, sin_t = _rope_tables(n_ctx, s)

    head_avg = _block_diag(jnp.full((N_Q_HEADS, HEAD_DIM, HEAD_DIM), 1.0 / HEAD_DIM, F32)).astype(BF16)
    rw_t = router_w.T
    rw_hi = rw_t.astype(BF16)
    rw_lo = (rw_t - rw_hi.astype(F32)).astype(BF16)
    rb = jnp.broadcast_to(router_b[:, None], (N_EXPERTS, ROW_TILE)).astype(F32)

    qw, kw = ATTN_WIDTH, KV_WIDTH
    out = None
    for li in range(depth):
        wi = w_in[li]
        k0, k1 = wi[:, qw:qw + HEAD_DIM], wi[:, qw + HEAD_DIM:qw + kw]
        v0, v1 = wi[:, qw + kw:qw + kw + HEAD_DIM], wi[:, qw + kw + HEAD_DIM:qw + 2 * kw]
        w_ext = jnp.concatenate([wi[:, :qw], k0, k0, k1, k1, v0, v0, v1, v1, wi[:, qw + 2 * kw:]],
                                axis=1).astype(BF16)
        gq = jnp.tile(q_norm_g[li], N_Q_HEADS)[None, :]
        gk = jnp.tile(k_norm_g[li], 2 * N_KV_HEADS)[None, :]
        mods = mods_all[li]
        q, kd, vd, u, gg = _inproj(xa, mods, norm1_g[li][None, :], w_ext, cos_t, sin_t, gq, gk, head_avg)
        attn_n = _attention(q, kd, vd, attn_out_g[li][None, :], n_ctx)

        w_gate = jnp.concatenate([_block_diag(lru_wa[li, 0]), _block_diag(lru_wx[li, 0]),
                                  _block_diag(lru_wa[li, 1]), _block_diag(lru_wx[li, 1])], axis=1).astype(BF16)
        b_gate = jnp.concatenate([lru_ba[li, 0], lru_bx[li, 0], lru_ba[li, 1], lru_bx[li, 1]])[None, :]
        rec_n = _lru(u, gg, conv_w[li], conv_b[li][None, :], w_gate, b_gate, lru_lambda[li],
                     lru_out_g[li][None, :], n_ctx)

        x1, h2, route = _outproj(xa, attn_n, rec_n, mods, w_out[li].astype(BF16), norm2_g[li][None, :],
                                 rw_hi, rw_lo, rb)
        combine = route.transpose(0, 1, 3, 2).reshape(b * l, ROUTE_ROWS)
        w_gu = jnp.concatenate([exp_w_gate[li], exp_w_up[li]], axis=-1).astype(BF16)
        y = _moe_dense(h2.reshape(b * l, d), combine, w_gu, exp_w_down[li].astype(BF16)).reshape(b, l, d)
        if li == depth - 1:
            out = _final(x1, y, mods, final_g[None, :], n_ctx)
        else:
            xa = _residual(x1, y, mods)
    return out
```

```python
import functools

import jax
import jax.numpy as jnp
from jax import lax
from jax.experimental import pallas as pl
from jax.experimental.pallas import tpu as pltpu

F32 = jnp.float32
BF16 = jnp.bfloat16

HEAD_DIM = 64
N_Q_HEADS = 8
N_KV_HEADS = 2
ATTN_WIDTH = N_Q_HEADS * HEAD_DIM
KV_WIDTH = N_KV_HEADS * HEAD_DIM
LRU_WIDTH = 512
LRU_BLOCKS = 8
CONV_WIDTH = 4
CONV_LEFT = CONV_WIDTH // 2
LRU_C = 8.0
GRID_W = 64
ROPE_THETA = 10000.0
N_EXPERTS = 16
N_GROUPS = 4
GROUP_SIZE = N_EXPERTS // N_GROUPS
N_MOD = 6
NORM_EPS = 1e-6
ATTN_SCALE = HEAD_DIM ** -0.5

LANES = 128
SUBLANES = 8
ROW_TILE = 256
MOE_ROW_TILE = 1024
ROUTE_ROWS = 24
VMEM_LIMIT = 60000 * 1024


def _params(sem):
    return pltpu.CompilerParams(dimension_semantics=sem, vmem_limit_bytes=VMEM_LIMIT)


def _sigmoid(x):
    return 1.0 / (1.0 + jnp.exp(-x))


def _rms(x):
    return x * lax.rsqrt(jnp.mean(x * x, axis=-1, keepdims=True) + NORM_EPS)


def _split_bf16(x):
    hi = x.astype(BF16)
    lo = (x - hi.astype(F32)).astype(BF16)
    return hi, lo


def _ada_kernel(c_ref, w_ref, b_ref, o_ref):
    cv = c_ref[...]
    s = cv * _sigmoid(cv)
    o_ref[...] = jnp.dot(s.astype(BF16), w_ref[...].astype(BF16),
                         preferred_element_type=F32) + b_ref[...]


def _ada_mods(cvec, ada_w, ada_b):
    depth, d, n = ada_w.shape
    tn = 512
    rows = cvec.shape[0]
    return pl.pallas_call(
        _ada_kernel,
        grid=(depth, n // tn),
        in_specs=[
            pl.BlockSpec((rows, d), lambda l, j: (0, 0)),
            pl.BlockSpec((None, d, tn), lambda l, j: (l, 0, j)),
            pl.BlockSpec((None, 1, tn), lambda l, j: (l, 0, j)),
        ],
        out_specs=pl.BlockSpec((None, rows, tn), lambda l, j: (l, 0, j)),
        out_shape=jax.ShapeDtypeStruct((depth, rows, n), F32),
        compiler_params=_params(("parallel", "parallel")),
        name="ada_mods",
    )(cvec, ada_w, ada_b.reshape(depth, 1, n))


def _head_rms(t, g_mat, gain):
    hi, lo = _split_bf16(t * t)
    m = (jnp.dot(hi, g_mat, preferred_element_type=F32)
         + jnp.dot(lo, g_mat, preferred_element_type=F32))
    return t * lax.rsqrt(m + NORM_EPS) * gain


def _rope(t, cos_t, sin_t):
    width = t.shape[-1]
    reps = width // LANES
    cos_w = jnp.concatenate([cos_t] * reps, axis=1)
    sin_w = jnp.concatenate([sin_t] * reps, axis=1)
    lane = lax.broadcasted_iota(jnp.int32, t.shape, 1)
    quarter = HEAD_DIM // 4
    first = (lane % (2 * quarter)) < quarter
    partner = jnp.where(first, pltpu.roll(t, width - quarter, 1), pltpu.roll(t, quarter, 1))
    return t * cos_w + partner * sin_w


def _inproj_kernel(x_ref, mod_ref, n1_ref, w_ref, cos_ref, sin_ref, gq_ref, gk_ref, gm_ref,
                   q_ref, k_ref, v_ref, u_ref, gg_ref):
    x = x_ref[...]
    h = _rms(x) * n1_ref[...]
    h = h * (1.0 + mod_ref[1:2, :]) + mod_ref[0:1, :]
    y = jnp.dot(h.astype(BF16), w_ref[...], preferred_element_type=F32)
    kd = 2 * KV_WIDTH
    o_k, o_v, o_u, o_g = ATTN_WIDTH, ATTN_WIDTH + kd, ATTN_WIDTH + 2 * kd, ATTN_WIDTH + 2 * kd + LRU_WIDTH
    cos_t = cos_ref[...]
    sin_t = sin_ref[...]
    g_mat = gm_ref[...]
    q = _rope(_head_rms(y[:, :o_k], g_mat, gq_ref[...]), cos_t, sin_t)
    q_ref[...] = (q * ATTN_SCALE).astype(BF16)
    k = _rope(_head_rms(y[:, o_k:o_v], g_mat[:kd, :kd], gk_ref[...]), cos_t, sin_t)
    k_ref[...] = k.astype(BF16)
    v_ref[...] = y[:, o_v:o_u].astype(BF16)
    u_ref[...] = y[:, o_u:o_g]
    gb = y[:, o_g:]
    gg_ref[...] = 0.5 * gb * (1.0 + jnp.tanh(0.7978845608028654 * (gb + 0.044715 * gb * gb * gb)))


def _inproj(xa, mods, n1g, w_ext, cos_t, sin_t, gq, gk, g_mat):
    b, l, d = xa.shape
    nt = l // ROW_TILE
    n_ext = w_ext.shape[1]
    kd = 2 * KV_WIDTH
    row = lambda w: pl.BlockSpec((None, ROW_TILE, w), lambda i, t: (i, t, 0))
    const = lambda shape: pl.BlockSpec(shape, lambda i, t: (0,) * len(shape))
    return pl.pallas_call(
        _inproj_kernel,
        grid=(b, nt),
        in_specs=[
            row(d),
            pl.BlockSpec((None, N_MOD, d), lambda i, t: (jnp.where(t == 0, b, i), 0, 0)),
            const((1, d)),
            const((d, n_ext)),
            pl.BlockSpec((ROW_TILE, LANES), lambda i, t: (t, 0)),
            pl.BlockSpec((ROW_TILE, LANES), lambda i, t: (t, 0)),
            const((1, ATTN_WIDTH)),
            const((1, kd)),
            const((ATTN_WIDTH, ATTN_WIDTH)),
        ],
        out_specs=[row(ATTN_WIDTH), row(kd), row(kd), row(LRU_WIDTH), row(LRU_WIDTH)],
        out_shape=[
            jax.ShapeDtypeStruct((b, l, ATTN_WIDTH), BF16),
            jax.ShapeDtypeStruct((b, l, kd), BF16),
            jax.ShapeDtypeStruct((b, l, kd), BF16),
            jax.ShapeDtypeStruct((b, l, LRU_WIDTH), F32),
            jax.ShapeDtypeStruct((b, l, LRU_WIDTH), F32),
        ],
        compiler_params=_params(("parallel", "parallel")),
        name="in_proj",
    )(xa, mods, n1g, w_ext, cos_t, sin_t, gq, gk, g_mat)


def _attn_kernel(q_ref, k_ref, v_ref, g_ref, o_ref, *, n_ctx):
    t = pl.program_id(1)
    rows = q_ref.shape[0]
    n_all = k_ref.shape[0]

    def run(n_keys):
        lane = lax.broadcasted_iota(jnp.int32, (rows, LANES), 1)
        low = lane < HEAD_DIM
        outs = []
        for pair in range(N_Q_HEADS // 2):
            kv = (2 * pair) // (N_Q_HEADS // N_KV_HEADS)
            qp = q_ref[:, pair * LANES:(pair + 1) * LANES]
            zero = jnp.zeros_like(qp)
            q2 = jnp.concatenate([jnp.where(low, qp, zero), jnp.where(low, zero, qp)], axis=0)
            kd = k_ref[0:n_keys, kv * LANES:(kv + 1) * LANES]
            vd = v_ref[0:n_keys, kv * LANES:(kv + 1) * LANES]
            s = lax.dot_general(q2, kd, (((1,), (1,)), ((), ())), preferred_element_type=F32)
            p = jnp.exp(s - jnp.max(s, axis=-1, keepdims=True))
            den = jnp.sum(p, axis=-1, keepdims=True)
            o = jnp.dot(p.astype(BF16), vd, preferred_element_type=F32) * (1.0 / den)
            outs.append(jnp.where(low, o[:rows], o[rows:]))
        a = jnp.concatenate(outs, axis=1)
        o_ref[...] = (_rms(a) * g_ref[...]).astype(BF16)

    @pl.when(t == 0)
    def _():
        run(n_ctx)

    @pl.when(t > 0)
    def _():
        run(n_all)


def _attention(q, kd, vd, gain, n_ctx):
    b, l, w = q.shape
    nt = l // ROW_TILE
    kw = kd.shape[-1]
    return pl.pallas_call(
        functools.partial(_attn_kernel, n_ctx=n_ctx),
        grid=(b, nt),
        in_specs=[
            pl.BlockSpec((None, ROW_TILE, w), lambda i, t: (i, t, 0)),
            pl.BlockSpec((None, l, kw), lambda i, t: (i, 0, 0)),
            pl.BlockSpec((None, l, kw), lambda i, t: (i, 0, 0)),
            pl.BlockSpec((1, w), lambda i, t: (0, 0)),
        ],
        out_specs=pl.BlockSpec((None, ROW_TILE, w), lambda i, t: (i, t, 0)),
        out_shape=jax.ShapeDtypeStruct((b, l, w), BF16),
        compiler_params=_params(("parallel", "parallel")),
        name="attention",
    )(q, kd, vd, gain)


def _lru_kernel(u_ref, gg_ref, cw_ref, cb_ref, wg_ref, bg_ref, lam_ref, og_ref, o_ref,
                upad, a_f, b_f, a_r, b_r, *, n_ctx):
    l, w = u_ref.shape
    pad = SUBLANES
    zeros_pad = jnp.zeros((pad, w), F32)
    upad[0:pad, :] = zeros_pad
    upad[pad + l:2 * pad + l, :] = zeros_pad
    for r0 in range(0, l, ROW_TILE):
        upad[pad + r0:pad + r0 + ROW_TILE, :] = u_ref[r0:r0 + ROW_TILE, :]

    neg_lam = -lam_ref[...]
    softplus = jnp.maximum(neg_lam, 0.0) + jnp.log1p(jnp.exp(-jnp.abs(neg_lam)))
    cw = cw_ref[...]
    cb = cb_ref[...]

    for r0 in range(0, l, ROW_TILE):
        row = r0 + lax.broadcasted_iota(jnp.int32, (ROW_TILE, w), 0)
        is_lat = row >= n_ctx
        near_boundary = r0 - CONV_WIDTH < n_ctx < r0 + ROW_TILE + CONV_WIDTH
        uc = jnp.zeros((ROW_TILE, w), F32) + cb
        for j in range(CONV_WIDTH):
            off = j - CONV_LEFT
            tap = upad[pad + r0 + off:pad + r0 + off + ROW_TILE, :]
            if off != 0 and near_boundary:
                tap = jnp.where(((row + off) >= n_ctx) == is_lat, tap, 0.0)
            uc = uc + tap * cw[j:j + 1, :]
        z = jnp.dot(uc.astype(BF16), wg_ref[...], preferred_element_type=F32) + bg_ref[...]
        for d, (a_ref, b_ref) in enumerate(((a_f, b_f), (a_r, b_r))):
            base = 2 * d * w
            r_gate = _sigmoid(z[:, base:base + w])
            i_gate = _sigmoid(z[:, base + w:base + 2 * w])
            log_a = (-LRU_C * r_gate) * softplus[d:d + 1, :]
            a = jnp.exp(log_a)
            one_minus_a2 = -jnp.tanh(log_a) * (a * a + 1.0)
            bb = jnp.sqrt(one_minus_a2) * (i_gate * uc)
            a_ref[r0:r0 + ROW_TILE, :] = a
            b_ref[r0:r0 + ROW_TILE, :] = bb

    sub = lax.broadcasted_iota(jnp.int32, (SUBLANES, w), 0)

    def tile_scan(a_ref, b_ref, blk, carry, reverse):
        rows = pl.ds(pl.multiple_of(blk * SUBLANES, SUBLANES), SUBLANES)
        a = a_ref[rows, :]
        b = b_ref[rows, :]
        s = 1
        while s < SUBLANES:
            keep = (sub < SUBLANES - s) if reverse else (sub >= s)
            shift = SUBLANES - s if reverse else s
            b = b + a * jnp.where(keep, pltpu.roll(b, shift, 0), 0.0)
            a = a * jnp.where(keep, pltpu.roll(a, shift, 0), 1.0)
            s *= 2
        h = b + a * carry
        b_ref[rows, :] = h
        last = h[0:1, :] if reverse else h[SUBLANES - 1:SUBLANES, :]
        return jnp.broadcast_to(last, h.shape)

    n_blk = l // SUBLANES
    c_blk = n_ctx // SUBLANES

    def ctx_body(i, carry):
        cf, cr = carry
        return tile_scan(a_f, b_f, i, cf, False), tile_scan(a_r, b_r, c_blk - 1 - i, cr, True)

    def lat_body(i, carry):
        cf, cr = carry
        return tile_scan(a_f, b_f, i, cf, False), tile_scan(a_r, b_r, n_blk + c_blk - 1 - i, cr, True)

    zero = jnp.zeros((SUBLANES, w), F32)
    carry = lax.fori_loop(0, c_blk, ctx_body, (zero, zero), unroll=2)
    lax.fori_loop(c_blk, n_blk, lat_body, carry, unroll=2)

    gain = og_ref[...]
    for r0 in range(0, l, ROW_TILE):
        rows = slice(r0, r0 + ROW_TILE)
        h = b_f[rows, :] + b_r[rows, :]
        o_ref[rows, :] = (_rms(h * gg_ref[rows, :]) * gain).astype(BF16)


def _lru(u, gg, conv_w, conv_b, w_gate, b_gate, lam, out_g, n_ctx):
    b, l, w = u.shape
    const = lambda shape: pl.BlockSpec(shape, lambda i: (0,) * len(shape))
    seq = pl.BlockSpec((None, l, w), lambda i: (i, 0, 0))
    return pl.pallas_call(
        functools.partial(_lru_kernel, n_ctx=n_ctx),
        grid=(b,),
        in_specs=[seq, seq, const(conv_w.shape), const((1, w)), const(w_gate.shape),
                  const(b_gate.shape), const(lam.shape), const((1, w))],
        out_specs=seq,
        out_shape=jax.ShapeDtypeStruct((b, l, w), BF16),
        scratch_shapes=[pltpu.VMEM((l + 2 * SUBLANES, w), F32)] + [pltpu.VMEM((l, w), F32)] * 4,
        compiler_params=_params(("parallel",)),
        name="rg_lru",
    )(u, gg, conv_w, conv_b, w_gate, b_gate, lam, out_g)


def _route(scores, biased):
    group_score = []
    for g in range(N_GROUPS):
        v = biased[g * GROUP_SIZE:(g + 1) * GROUP_SIZE]
        best = None
        for i in range(GROUP_SIZE):
            for j in range(i + 1, GROUP_SIZE):
                pair = v[i] + v[j]
                best = pair if best is None else jnp.maximum(best, pair)
        group_score.append(best)
    gid = jnp.zeros_like(group_score[0], dtype=jnp.int32)
    gbest = group_score[0]
    for g in range(1, N_GROUPS):
        upd = group_score[g] > gbest
        gid = jnp.where(upd, g, gid)
        gbest = jnp.where(upd, group_score[g], gbest)

    def pick(rows, j):
        out = rows[j]
        for g in range(1, N_GROUPS):
            out = jnp.where(gid == g, rows[g * GROUP_SIZE + j], out)
        return out

    v = [pick(biased, j) for j in range(GROUP_SIZE)]
    s = [pick(scores, j) for j in range(GROUP_SIZE)]
    i1 = jnp.zeros_like(gid)
    m1 = v[0]
    for j in range(1, GROUP_SIZE):
        upd = v[j] > m1
        i1 = jnp.where(upd, j, i1)
        m1 = jnp.where(upd, v[j], m1)
    i2 = jnp.zeros_like(gid)
    m2 = jnp.full_like(m1, -jnp.inf)
    for j in range(GROUP_SIZE):
        upd = (i1 != j) & (v[j] > m2)
        i2 = jnp.where(upd, j, i2)
        m2 = jnp.where(upd, v[j], m2)
    w1 = s[0]
    w2 = s[0]
    for j in range(1, GROUP_SIZE):
        w1 = jnp.where(i1 == j, s[j], w1)
        w2 = jnp.where(i2 == j, s[j], w2)
    den = w1 + w2
    w1 = w1 / den
    w2 = w2 / den
    combine = []
    for e in range(N_EXPERTS):
        g, j = divmod(e, GROUP_SIZE)
        cj = jnp.where(i1 == j, w1, jnp.where(i2 == j, w2, 0.0))
        combine.append(jnp.where(gid == g, cj, 0.0))
    return combine, gid


def _outproj_kernel(x_ref, a_ref, r_ref, mod_ref, w_ref, n2_ref, rwh_ref, rwl_ref, rb_ref,
                    x1_ref, h2_ref, route_ref):
    half = a_ref.shape[-1]
    mix = (jnp.dot(a_ref[...], w_ref[0:half, :], preferred_element_type=F32)
           + jnp.dot(r_ref[...], w_ref[half:, :], preferred_element_type=F32))
    x1 = x_ref[...] + mod_ref[2:3, :] * mix
    x1_ref[...] = x1
    h2 = _rms(x1) * n2_ref[...]
    h2 = h2 * (1.0 + mod_ref[4:5, :]) + mod_ref[3:4, :]
    h2_ref[...] = h2.astype(BF16)
    hi, lo = _split_bf16(h2)
    nt = (((1,), (1,)), ((), ()))
    logits = (lax.dot_general(rwh_ref[...], hi, nt, preferred_element_type=F32)
              + lax.dot_general(rwh_ref[...], lo, nt, preferred_element_type=F32)
              + lax.dot_general(rwl_ref[...], hi, nt, preferred_element_type=F32))
    scores = _sigmoid(logits)
    biased = scores + rb_ref[...]
    combine, gid = _route([scores[e:e + 1, :] for e in range(N_EXPERTS)],
                          [biased[e:e + 1, :] for e in range(N_EXPERTS)])
    for e in range(N_EXPERTS):
        route_ref[e:e + 1, :] = combine[e]
    route_ref[N_EXPERTS:N_EXPERTS + 1, :] = gid.astype(F32)
    route_ref[N_EXPERTS + 1:, :] = jnp.zeros((ROUTE_ROWS - N_EXPERTS - 1, x_ref.shape[0]), F32)


def _outproj(xa, attn_n, rec_n, mods, w_out, n2g, rw_hi, rw_lo, rb):
    b, l, d = xa.shape
    nt = l // ROW_TILE
    half = attn_n.shape[-1]
    row = lambda w: pl.BlockSpec((None, ROW_TILE, w), lambda i, t: (i, t, 0))
    const = lambda shape: pl.BlockSpec(shape, lambda i, t: (0,) * len(shape))
    return pl.pallas_call(
        _outproj_kernel,
        grid=(b, nt),
        in_specs=[
            row(d), row(half), row(half),
            pl.BlockSpec((None, N_MOD, d), lambda i, t: (jnp.where(t == 0, b, i), 0, 0)),
            const(w_out.shape), const((1, d)), const(rw_hi.shape), const(rw_lo.shape), const(rb.shape),
        ],
        out_specs=[row(d), row(d),
                   pl.BlockSpec((None, None, ROUTE_ROWS, ROW_TILE), lambda i, t: (i, t, 0, 0))],
        out_shape=[
            jax.ShapeDtypeStruct((b, l, d), F32),
            jax.ShapeDtypeStruct((b, l, d), BF16),
            jax.ShapeDtypeStruct((b, nt, ROUTE_ROWS, ROW_TILE), F32),
        ],
        compiler_params=_params(("parallel", "parallel")),
        name="out_proj_router",
    )(xa, attn_n, rec_n, mods, w_out, n2g, rw_hi, rw_lo, rb)


def _moe_dense_kernel(h_ref, c_ref, wgu_ref, wd_ref, y_ref, acc_ref):
    e = pl.program_id(1)

    @pl.when(e == 0)
    def _():
        acc_ref[...] = jnp.zeros_like(acc_ref)

    d_exp = wd_ref.shape[0]
    gu = jnp.dot(h_ref[...], wgu_ref[...], preferred_element_type=F32)
    gate = gu[:, :d_exp]
    hid = gate * _sigmoid(gate) * gu[:, d_exp:]
    c = c_ref[...]
    lane = lax.broadcasted_iota(jnp.int32, c.shape, 1)
    ce = jnp.sum(jnp.where(lane == e, c, 0.0), axis=-1, keepdims=True)
    acc_ref[...] += ce * jnp.dot(hid.astype(BF16), wd_ref[...], preferred_element_type=F32)

    @pl.when(e == pl.num_programs(1) - 1)
    def _():
        y_ref[...] = acc_ref[...]


def _moe_dense(h2, combine, w_gu, w_d):
    n, d = h2.shape
    n_exp, _, two_de = w_gu.shape
    tm = MOE_ROW_TILE
    return pl.pallas_call(
        _moe_dense_kernel,
        grid=(n // tm, n_exp),
        in_specs=[
            pl.BlockSpec((tm, d), lambda i, e: (i, 0)),
            pl.BlockSpec((tm, combine.shape[1]), lambda i, e: (i, 0)),
            pl.BlockSpec((None, d, two_de), lambda i, e: (e, 0, 0)),
            pl.BlockSpec((None, two_de // 2, d), lambda i, e: (e, 0, 0)),
        ],
        out_specs=pl.BlockSpec((tm, d), lambda i, e: (i, 0)),
        out_shape=jax.ShapeDtypeStruct((n, d), F32),
        scratch_shapes=[pltpu.VMEM((tm, d), F32)],
        compiler_params=_params(("parallel", "arbitrary")),
        name="moe_dense",
    )(h2, combine, w_gu, w_d)


def _residual_kernel(x_ref, y_ref, mod_ref, o_ref):
    o_ref[...] = x_ref[...] + mod_ref[5:6, :] * y_ref[...]


def _final_kernel(x_ref, y_ref, mod_ref, g_ref, o_ref):
    o_ref[...] = _rms(x_ref[...] + mod_ref[5:6, :] * y_ref[...]) * g_ref[...]


def _residual(x1, y, mods):
    b, l, d = x1.shape
    row = pl.BlockSpec((None, ROW_TILE, d), lambda i, t: (i, t, 0))
    return pl.pallas_call(
        _residual_kernel,
        grid=(b, l // ROW_TILE),
        in_specs=[row, row,
                  pl.BlockSpec((None, N_MOD, d), lambda i, t: (jnp.where(t == 0, b, i), 0, 0))],
        out_specs=row,
        out_shape=jax.ShapeDtypeStruct((b, l, d), F32),
        compiler_params=_params(("parallel", "parallel")),
        name="moe_residual",
    )(x1, y, mods)


def _final(x1, y, mods, final_g, n_ctx):
    b, l, d = x1.shape
    skip = n_ctx // ROW_TILE
    row = pl.BlockSpec((None, ROW_TILE, d), lambda i, t: (i, t + skip, 0))
    return pl.pallas_call(
        _final_kernel,
        grid=(b, (l - n_ctx) // ROW_TILE),
        in_specs=[row, row,
                  pl.BlockSpec((None, N_MOD, d), lambda i, t: (i, 0, 0)),
                  pl.BlockSpec((1, d), lambda i, t: (0, 0))],
        out_specs=pl.BlockSpec((None, ROW_TILE, d), lambda i, t: (i, t, 0)),
        out_shape=jax.ShapeDtypeStruct((b, l - n_ctx, d), F32),
        compiler_params=_params(("parallel", "parallel")),
        name="final_norm",
    )(x1, y, mods, final_g)


def _rope_tables(n_ctx, n_lat):
    rows = n_lat // GRID_W
    r, col = jnp.meshgrid(jnp.arange(rows), jnp.arange(GRID_W), indexing="ij")
    r = r.reshape(-1).astype(F32)
    col = col.reshape(-1).astype(F32)
    half = HEAD_DIM // 2
    inv = ROPE_THETA ** (-jnp.arange(0, half, 2, dtype=F32) / half)
    ang_r = r[:, None] * inv
    ang_c = col[:, None] * inv
    cos_h = jnp.concatenate([jnp.cos(ang_r)] * 2 + [jnp.cos(ang_c)] * 2, axis=-1)
    sin_h = jnp.concatenate([-jnp.sin(ang_r), jnp.sin(ang_r), -jnp.sin(ang_c), jnp.sin(ang_c)], axis=-1)
    cos_t = jnp.concatenate([jnp.ones((n_ctx, HEAD_DIM), F32), cos_h], axis=0)
    sin_t = jnp.concatenate([jnp.zeros((n_ctx, HEAD_DIM), F32), sin_h], axis=0)
    reps = LANES // HEAD_DIM
    return jnp.tile(cos_t, (1, reps)), jnp.tile(sin_t, (1, reps))


def _block_diag(w):
    n, d, e = w.shape
    eye = jnp.eye(n, dtype=w.dtype)
    return (w[:, :, None, :] * eye[:, None, :, None]).reshape(n * d, n * e)


def kernel(x, c, ctx, c_ctx, ada_w, ada_b, norm1_g, w_in, q_norm_g, k_norm_g, conv_w, conv_b, lru_wa, lru_ba, lru_wx, lru_bx, lru_lambda, attn_out_g, lru_out_g, w_out, norm2_g, router_w, router_b, exp_w_gate, exp_w_up, exp_w_down, final_g):
    b, s, d = x.shape
    n_ctx = ctx.shape[1]
    l = n_ctx + s
    depth = ada_w.shape[0]
    nt = l // ROW_TILE
    assert n_ctx == ROW_TILE and s % ROW_TILE == 0 and MOE_ROW_TILE % ROW_TILE == 0

    xa = jnp.concatenate([ctx, x], axis=1)
    cvec = jnp.zeros((2 * SUBLANES, d), F32).at[:b].set(c).at[b].set(c_ctx)
    mods_all = _ada_mods(cvec, ada_w, ada_b)[:, :b + 1].reshape(depth, b + 1, N_MOD, d)
    cos_t, sin_t = _rope_tables(n_ctx, s)

    head_avg = _block_diag(jnp.full((N_Q_HEADS, HEAD_DIM, HEAD_DIM), 1.0 / HEAD_DIM, F32)).astype(BF16)
    rw_t = router_w.T
    rw_hi = rw_t.astype(BF16)
    rw_lo = (rw_t - rw_hi.astype(F32)).astype(BF16)
    rb = jnp.broadcast_to(router_b[:, None], (N_EXPERTS, ROW_TILE)).astype(F32)

    qw, kw = ATTN_WIDTH, KV_WIDTH
    out = None
    for li in range(depth):
        wi = w_in[li]
        k0, k1 = wi[:, qw:qw + HEAD_DIM], wi[:, qw + HEAD_DIM:qw + kw]
        v0, v1 = wi[:, qw + kw:qw + kw + HEAD_DIM], wi[:, qw + kw + HEAD_DIM:qw + 2 * kw]
        w_ext = jnp.concatenate([wi[:, :qw], k0, k0, k1, k1, v0, v0, v1, v1, wi[:, qw + 2 * kw:]],
                                axis=1).astype(BF16)
        gq = jnp.tile(q_norm_g[li], N_Q_HEADS)[None, :]
        gk = jnp.tile(k_norm_g[li], 2 * N_KV_HEADS)[None, :]
        mods = mods_all[li]
        q, kd, vd, u, gg = _inproj(xa, mods, norm1_g[li][None, :], w_ext, cos_t, sin_t, gq, gk, head_avg)
        attn_n = _attention(q, kd, vd, attn_out_g[li][None, :], n_ctx)

        w_gate = jnp.concatenate([_block_diag(lru_wa[li, 0]), _block_diag(lru_wx[li, 0]),
                                  _block_diag(lru_wa[li, 1]), _block_diag(lru_wx[li, 1])], axis=1).astype(BF16)
        b_gate = jnp.concatenate([lru_ba[li, 0], lru_bx[li, 0], lru_ba[li, 1], lru_bx[li, 1]])[None, :]
        rec_n = _lru(u, gg, conv_w[li], conv_b[li][None, :], w_gate, b_gate, lru_lambda[li],
                     lru_out_g[li][None, :], n_ctx)

        x1, h2, route = _outproj(xa, attn_n, rec_n, mods, w_out[li].astype(BF16), norm2_g[li][None, :],
                                 rw_hi, rw_lo, rb)
        combine = route.transpose(0, 1, 3, 2).reshape(b * l, ROUTE_ROWS)
        w_gu = jnp.concatenate([exp_w_gate[li], exp_w_up[li]], axis=-1).astype(BF16)
        y = _moe_dense(h2.reshape(b * l, d), combine, w_gu, exp_w_down[li].astype(BF16)).reshape(b, l, d)
        if li == depth - 1:
            out = _final(x1, y, mods, final_g[None, :], n_ctx)
        else:
            xa = _residual(x1, y, mods)
    return out
```

```python
import functools

import jax
import jax.numpy as jnp
from jax import lax
from jax.experimental import pallas as pl
from jax.experimental.pallas import tpu as pltpu

F32 = jnp.float32
BF16 = jnp.bfloat16

HEAD_DIM = 64
N_Q_HEADS = 8
N_KV_HEADS = 2
ATTN_WIDTH = N_Q_HEADS * HEAD_DIM
KV_WIDTH = N_KV_HEADS * HEAD_DIM
LRU_WIDTH = 512
LRU_BLOCKS = 8
CONV_WIDTH = 4
CONV_LEFT = CONV_WIDTH // 2
LRU_C = 8.0
GRID_W = 64
ROPE_THETA = 10000.0
N_EXPERTS = 16
N_GROUPS = 4
GROUP_SIZE = N_EXPERTS // N_GROUPS
N_MOD = 6
NORM_EPS = 1e-6
ATTN_SCALE = HEAD_DIM ** -0.5

LANES = 128
SUBLANES = 8
ROW_TILE = 256
N_PAIRS = GROUP_SIZE * (GROUP_SIZE - 1) // 2
N_BUCKETS = N_GROUPS * N_PAIRS
BUCKET_ROWS = 32
ROUTE_ROWS = 8
VMEM_LIMIT = 60000 * 1024


def _params(sem):
    return pltpu.CompilerParams(dimension_semantics=sem, vmem_limit_bytes=VMEM_LIMIT)


def _sigmoid(x):
    return 1.0 / (1.0 + jnp.exp(-x))


def _rms(x):
    return x * lax.rsqrt(jnp.mean(x * x, axis=-1, keepdims=True) + NORM_EPS)


def _split_bf16(x):
    hi = x.astype(BF16)
    lo = (x - hi.astype(F32)).astype(BF16)
    return hi, lo


def _ada_kernel(c_ref, w_ref, b_ref, o_ref):
    cv = c_ref[...]
    s = cv * _sigmoid(cv)
    o_ref[...] = jnp.dot(s.astype(BF16), w_ref[...].astype(BF16),
                         preferred_element_type=F32) + b_ref[...]


def _ada_mods(cvec, ada_w, ada_b):
    depth, d, n = ada_w.shape
    tn = 512
    rows = cvec.shape[0]
    return pl.pallas_call(
        _ada_kernel,
        grid=(depth, n // tn),
        in_specs=[
            pl.BlockSpec((rows, d), lambda l, j: (0, 0)),
            pl.BlockSpec((None, d, tn), lambda l, j: (l, 0, j)),
            pl.BlockSpec((None, 1, tn), lambda l, j: (l, 0, j)),
        ],
        out_specs=pl.BlockSpec((None, rows, tn), lambda l, j: (l, 0, j)),
        out_shape=jax.ShapeDtypeStruct((depth, rows, n), F32),
        compiler_params=_params(("parallel", "parallel")),
        name="ada_mods",
    )(cvec, ada_w, ada_b.reshape(depth, 1, n))


def _head_rms(t, g_mat, gain):
    hi, lo = _split_bf16(t * t)
    m = (jnp.dot(hi, g_mat, preferred_element_type=F32)
         + jnp.dot(lo, g_mat, preferred_element_type=F32))
    return t * lax.rsqrt(m + NORM_EPS) * gain


def _rope(t, cos_t, sin_t):
    width = t.shape[-1]
    reps = width // LANES
    cos_w = jnp.concatenate([cos_t] * reps, axis=1)
    sin_w = jnp.concatenate([sin_t] * reps, axis=1)
    lane = lax.broadcasted_iota(jnp.int32, t.shape, 1)
    quarter = HEAD_DIM // 4
    first = (lane % (2 * quarter)) < quarter
    partner = jnp.where(first, pltpu.roll(t, width - quarter, 1), pltpu.roll(t, quarter, 1))
    return t * cos_w + partner * sin_w


def _inproj_kernel(x_ref, mod_ref, n1_ref, w_ref, cos_ref, sin_ref, gq_ref, gk_ref, gm_ref,
                   q_ref, k_ref, v_ref, u_ref, gg_ref):
    x = x_ref[...]
    h = _rms(x) * n1_ref[...]
    h = h * (1.0 + mod_ref[1:2, :]) + mod_ref[0:1, :]
    y = jnp.dot(h.astype(BF16), w_ref[...], preferred_element_type=F32)
    kd = 2 * KV_WIDTH
    o_k, o_v, o_u, o_g = ATTN_WIDTH, ATTN_WIDTH + kd, ATTN_WIDTH + 2 * kd, ATTN_WIDTH + 2 * kd + LRU_WIDTH
    cos_t = cos_ref[...]
    sin_t = sin_ref[...]
    g_mat = gm_ref[...]
    q = _rope(_head_rms(y[:, :o_k], g_mat, gq_ref[...]), cos_t, sin_t)
    q_ref[...] = (q * ATTN_SCALE).astype(BF16)
    k = _rope(_head_rms(y[:, o_k:o_v], g_mat[:kd, :kd], gk_ref[...]), cos_t, sin_t)
    k_ref[...] = k.astype(BF16)
    v_ref[...] = y[:, o_v:o_u].astype(BF16)
    u_ref[...] = y[:, o_u:o_g]
    gb = y[:, o_g:]
    gg_ref[...] = 0.5 * gb * (1.0 + jnp.tanh(0.7978845608028654 * (gb + 0.044715 * gb * gb * gb)))


def _inproj(xa, mods, n1g, w_ext, cos_t, sin_t, gq, gk, g_mat):
    b, l, d = xa.shape
    nt = l // ROW_TILE
    n_ext = w_ext.shape[1]
    kd = 2 * KV_WIDTH
    row = lambda w: pl.BlockSpec((None, ROW_TILE, w), lambda i, t: (i, t, 0))
    const = lambda shape: pl.BlockSpec(shape, lambda i, t: (0,) * len(shape))
    return pl.pallas_call(
        _inproj_kernel,
        grid=(b, nt),
        in_specs=[
            row(d),
            pl.BlockSpec((None, N_MOD, d), lambda i, t: (jnp.where(t == 0, b, i), 0, 0)),
            const((1, d)),
            const((d, n_ext)),
            pl.BlockSpec((ROW_TILE, LANES), lambda i, t: (t, 0)),
            pl.BlockSpec((ROW_TILE, LANES), lambda i, t: (t, 0)),
            const((1, ATTN_WIDTH)),
            const((1, kd)),
            const((ATTN_WIDTH, ATTN_WIDTH)),
        ],
        out_specs=[row(ATTN_WIDTH), row(kd), row(kd), row(LRU_WIDTH), row(LRU_WIDTH)],
        out_shape=[
            jax.ShapeDtypeStruct((b, l, ATTN_WIDTH), BF16),
            jax.ShapeDtypeStruct((b, l, kd), BF16),
            jax.ShapeDtypeStruct((b, l, kd), BF16),
            jax.ShapeDtypeStruct((b, l, LRU_WIDTH), F32),
            jax.ShapeDtypeStruct((b, l, LRU_WIDTH), F32),
        ],
        compiler_params=_params(("parallel", "parallel")),
        name="in_proj",
    )(xa, mods, n1g, w_ext, cos_t, sin_t, gq, gk, g_mat)


def _attn_kernel(q_ref, k_ref, v_ref, g_ref, o_ref, *, n_ctx):
    t = pl.program_id(1)
    rows = q_ref.shape[0]
    n_all = k_ref.shape[0]

    def run(n_keys):
        lane = lax.broadcasted_iota(jnp.int32, (rows, LANES), 1)
        low = lane < HEAD_DIM
        outs = []
        for pair in range(N_Q_HEADS // 2):
            kv = (2 * pair) // (N_Q_HEADS // N_KV_HEADS)
            qp = q_ref[:, pair * LANES:(pair + 1) * LANES]
            zero = jnp.zeros_like(qp)
            q2 = jnp.concatenate([jnp.where(low, qp, zero), jnp.where(low, zero, qp)], axis=0)
            kd = k_ref[0:n_keys, kv * LANES:(kv + 1) * LANES]
            vd = v_ref[0:n_keys, kv * LANES:(kv + 1) * LANES]
            s = lax.dot_general(q2, kd, (((1,), (1,)), ((), ())), preferred_element_type=F32)
            p = jnp.exp(s - jnp.max(s, axis=-1, keepdims=True))
            den = jnp.sum(p, axis=-1, keepdims=True)
            o = jnp.dot(p.astype(BF16), vd, preferred_element_type=F32) * (1.0 / den)
            outs.append(jnp.where(low, o[:rows], o[rows:]))
        a = jnp.concatenate(outs, axis=1)
        o_ref[...] = (_rms(a) * g_ref[...]).astype(BF16)

    @pl.when(t == 0)
    def _():
        run(n_ctx)

    @pl.when(t > 0)
    def _():
        run(n_all)


def _attention(q, kd, vd, gain, n_ctx):
    b, l, w = q.shape
    nt = l // ROW_TILE
    kw = kd.shape[-1]
    return pl.pallas_call(
        functools.partial(_attn_kernel, n_ctx=n_ctx),
        grid=(b, nt),
        in_specs=[
            pl.BlockSpec((None, ROW_TILE, w), lambda i, t: (i, t, 0)),
            pl.BlockSpec((None, l, kw), lambda i, t: (i, 0, 0)),
            pl.BlockSpec((None, l, kw), lambda i, t: (i, 0, 0)),
            pl.BlockSpec((1, w), lambda i, t: (0, 0)),
        ],
        out_specs=pl.BlockSpec((None, ROW_TILE, w), lambda i, t: (i, t, 0)),
        out_shape=jax.ShapeDtypeStruct((b, l, w), BF16),
        compiler_params=_params(("parallel", "parallel")),
        name="attention",
    )(q, kd, vd, gain)


def _lru_kernel(u_ref, gg_ref, cw_ref, cb_ref, wg_ref, bg_ref, lam_ref, og_ref, o_ref,
                upad, a_f, b_f, a_r, b_r, *, n_ctx):
    l, w = u_ref.shape
    pad = SUBLANES
    zeros_pad = jnp.zeros((pad, w), F32)
    upad[0:pad, :] = zeros_pad
    upad[pad + l:2 * pad + l, :] = zeros_pad
    for r0 in range(0, l, ROW_TILE):
        upad[pad + r0:pad + r0 + ROW_TILE, :] = u_ref[r0:r0 + ROW_TILE, :]

    neg_lam = -lam_ref[...]
    softplus = jnp.maximum(neg_lam, 0.0) + jnp.log1p(jnp.exp(-jnp.abs(neg_lam)))
    cw = cw_ref[...]
    cb = cb_ref[...]

    for r0 in range(0, l, ROW_TILE):
        row = r0 + lax.broadcasted_iota(jnp.int32, (ROW_TILE, w), 0)
        is_lat = row >= n_ctx
        near_boundary = r0 - CONV_WIDTH < n_ctx < r0 + ROW_TILE + CONV_WIDTH
        uc = jnp.zeros((ROW_TILE, w), F32) + cb
        for j in range(CONV_WIDTH):
            off = j - CONV_LEFT
            tap = upad[pad + r0 + off:pad + r0 + off + ROW_TILE, :]
            if off != 0 and near_boundary:
                tap = jnp.where(((row + off) >= n_ctx) == is_lat, tap, 0.0)
            uc = uc + tap * cw[j:j + 1, :]
        z = jnp.dot(uc.astype(BF16), wg_ref[...], preferred_element_type=F32) + bg_ref[...]
        for d, (a_ref, b_ref) in enumerate(((a_f, b_f), (a_r, b_r))):
            base = 2 * d * w
            r_gate = _sigmoid(z[:, base:base + w])
            i_gate = _sigmoid(z[:, base + w:base + 2 * w])
            log_a = (-LRU_C * r_gate) * softplus[d:d + 1, :]
            a = jnp.exp(log_a)
            one_minus_a2 = -jnp.tanh(log_a) * (a * a + 1.0)
            bb = jnp.sqrt(one_minus_a2) * (i_gate * uc)
            a_ref[r0:r0 + ROW_TILE, :] = a
            b_ref[r0:r0 + ROW_TILE, :] = bb

    sub = lax.broadcasted_iota(jnp.int32, (SUBLANES, w), 0)

    def tile_scan(a_ref, b_ref, blk, carry, reverse):
        rows = pl.ds(pl.multiple_of(blk * SUBLANES, SUBLANES), SUBLANES)
        a = a_ref[rows, :]
        b = b_ref[rows, :]
        s = 1
        while s < SUBLANES:
            keep = (sub < SUBLANES - s) if reverse else (sub >= s)
            shift = SUBLANES - s if reverse else s
            b = b + a * jnp.where(keep, pltpu.roll(b, shift, 0), 0.0)
            a = a * jnp.where(keep, pltpu.roll(a, shift, 0), 1.0)
            s *= 2
        h = b + a * carry
        b_ref[rows, :] = h
        last = h[0:1, :] if reverse else h[SUBLANES - 1:SUBLANES, :]
        return jnp.broadcast_to(last, h.shape)

    n_blk = l // SUBLANES
    c_blk = n_ctx // SUBLANES

    def ctx_body(i, carry):
        cf, cr = carry
        return tile_scan(a_f, b_f, i, cf, False), tile_scan(a_r, b_r, c_blk - 1 - i, cr, True)

    def lat_body(i, carry):
        cf, cr = carry
        return tile_scan(a_f, b_f, i, cf, False), tile_scan(a_r, b_r, n_blk + c_blk - 1 - i, cr, True)

    zero = jnp.zeros((SUBLANES, w), F32)
    carry = lax.fori_loop(0, c_blk, ctx_body, (zero, zero), unroll=2)
    lax.fori_loop(c_blk, n_blk, lat_body, carry, unroll=2)

    gain = og_ref[...]
    for r0 in range(0, l, ROW_TILE):
        rows = slice(r0, r0 + ROW_TILE)
        h = b_f[rows, :] + b_r[rows, :]
        o_ref[rows, :] = (_rms(h * gg_ref[rows, :]) * gain).astype(BF16)


def _lru(u, gg, conv_w, conv_b, w_gate, b_gate, lam, out_g, n_ctx):
    b, l, w = u.shape
    const = lambda shape: pl.BlockSpec(shape, lambda i: (0,) * len(shape))
    seq = pl.BlockSpec((None, l, w), lambda i: (i, 0, 0))
    return pl.pallas_call(
        functools.partial(_lru_kernel, n_ctx=n_ctx),
        grid=(b,),
        in_specs=[seq, seq, const(conv_w.shape), const((1, w)), const(w_gate.shape),
                  const(b_gate.shape), const(lam.shape), const((1, w))],
        out_specs=seq,
        out_shape=jax.ShapeDtypeStruct((b, l, w), BF16),
        scratch_shapes=[pltpu.VMEM((l + 2 * SUBLANES, w), F32)] + [pltpu.VMEM((l, w), F32)] * 4,
        compiler_params=_params(("parallel",)),
        name="rg_lru",
    )(u, gg, conv_w, conv_b, w_gate, b_gate, lam, out_g)


def _route(scores, biased):
    group_score = []
    for g in range(N_GROUPS):
        v = biased[g * GROUP_SIZE:(g + 1) * GROUP_SIZE]
        best = None
        for i in range(GROUP_SIZE):
            for j in range(i + 1, GROUP_SIZE):
                pair = v[i] + v[j]
                best = pair if best is None else jnp.maximum(best, pair)
        group_score.append(best)
    gid = jnp.zeros_like(group_score[0], dtype=jnp.int32)
    gbest = group_score[0]
    for g in range(1, N_GROUPS):
        upd = group_score[g] > gbest
        gid = jnp.where(upd, g, gid)
        gbest = jnp.where(upd, group_score[g], gbest)

    def pick(rows, j):
        out = rows[j]
        for g in range(1, N_GROUPS):
            out = jnp.where(gid == g, rows[g * GROUP_SIZE + j], out)
        return out

    v = [pick(biased, j) for j in range(GROUP_SIZE)]
    s = [pick(scores, j) for j in range(GROUP_SIZE)]
    i1 = jnp.zeros_like(gid)
    m1 = v[0]
    for j in range(1, GROUP_SIZE):
        upd = v[j] > m1
        i1 = jnp.where(upd, j, i1)
        m1 = jnp.where(upd, v[j], m1)
    i2 = jnp.zeros_like(gid)
    m2 = jnp.full_like(m1, -jnp.inf)
    for j in range(GROUP_SIZE):
        upd = (i1 != j) & (v[j] > m2)
        i2 = jnp.where(upd, j, i2)
        m2 = jnp.where(upd, v[j], m2)
    w1 = s[0]
    w2 = s[0]
    for j in range(1, GROUP_SIZE):
        w1 = jnp.where(i1 == j, s[j], w1)
        w2 = jnp.where(i2 == j, s[j], w2)
    den = w1 + w2
    w1 = w1 / den
    w2 = w2 / den
    lo = jnp.minimum(i1, i2)
    hi = jnp.maximum(i1, i2)
    pair = hi - lo - 1
    for k in range(1, GROUP_SIZE - 1):
        pair = pair + jnp.where(lo >= k, GROUP_SIZE - k, 0)
    first_is_lo = i1 < i2
    return gid * N_PAIRS + pair, jnp.where(first_is_lo, w1, w2), jnp.where(first_is_lo, w2, w1)


def _outproj_kernel(x_ref, a_ref, r_ref, mod_ref, w_ref, n2_ref, rwh_ref, rwl_ref, rb_ref, tri_ref,
                    x1_ref, h2_ref, route_ref, counts_ref, cnt_ref):
    rows = x_ref.shape[0]
    half = a_ref.shape[-1]
    mix = (jnp.dot(a_ref[...], w_ref[0:half, :], preferred_element_type=F32)
           + jnp.dot(r_ref[...], w_ref[half:, :], preferred_element_type=F32))
    x1 = x_ref[...] + mod_ref[2:3, :] * mix
    x1_ref[...] = x1
    h2 = _rms(x1) * n2_ref[...]
    h2 = h2 * (1.0 + mod_ref[4:5, :]) + mod_ref[3:4, :]
    h2_ref[...] = h2
    hi, lo = _split_bf16(h2)
    nt = (((1,), (1,)), ((), ()))
    logits = (lax.dot_general(rwh_ref[...], hi, nt, preferred_element_type=F32)
              + lax.dot_general(rwh_ref[...], lo, nt, preferred_element_type=F32)
              + lax.dot_general(rwl_ref[...], hi, nt, preferred_element_type=F32))
    scores = _sigmoid(logits)
    biased = scores + rb_ref[...]
    bucket, w_lo, w_hi = _route([scores[e:e + 1, :] for e in range(N_EXPERTS)],
                                [biased[e:e + 1, :] for e in range(N_EXPERTS)])

    @pl.when((pl.program_id(0) == 0) & (pl.program_id(1) == 0))
    def _():
        cnt_ref[...] = jnp.zeros_like(cnt_ref)

    sub = lax.broadcasted_iota(jnp.int32, (BUCKET_ROWS, rows), 0)
    onehot = jnp.where(sub == bucket, 1.0, 0.0)
    before = jnp.dot(onehot.astype(BF16), tri_ref[...], preferred_element_type=F32) + cnt_ref[...]
    rank = jnp.sum(onehot * before, axis=0, keepdims=True)
    total = cnt_ref[...] + jnp.sum(onehot, axis=1, keepdims=True)
    cnt_ref[...] = total
    counts_ref[...] = total

    route_ref[0:1, :] = bucket.astype(F32)
    route_ref[1:2, :] = rank
    route_ref[2:3, :] = w_lo
    route_ref[3:4, :] = w_hi
    route_ref[4:, :] = jnp.zeros((ROUTE_ROWS - 4, rows), F32)


def _outproj(xa, attn_n, rec_n, mods, w_out, n2g, rw_hi, rw_lo, rb, tri, skip):
    b, l, d = xa.shape
    nt = l // ROW_TILE - skip
    half = attn_n.shape[-1]
    row = lambda w: pl.BlockSpec((None, ROW_TILE, w), lambda i, t: (i, t + skip, 0))
    out_row = pl.BlockSpec((None, ROW_TILE, d), lambda i, t: (i, t, 0))
    const = lambda shape: pl.BlockSpec(shape, lambda i, t: (0,) * len(shape))
    return pl.pallas_call(
        _outproj_kernel,
        grid=(b, nt),
        in_specs=[
            row(d), row(half), row(half),
            pl.BlockSpec((None, N_MOD, d), lambda i, t: (jnp.where(t + skip == 0, b, i), 0, 0)),
            const(w_out.shape), const((1, d)), const(rw_hi.shape), const(rw_lo.shape), const(rb.shape),
            const(tri.shape),
        ],
        out_specs=[out_row, out_row,
                   pl.BlockSpec((None, None, ROUTE_ROWS, ROW_TILE), lambda i, t: (i, t, 0, 0)),
                   const((BUCKET_ROWS, ROW_TILE))],
        out_shape=[
            jax.ShapeDtypeStruct((b, nt * ROW_TILE, d), F32),
            jax.ShapeDtypeStruct((b, nt * ROW_TILE, d), F32),
            jax.ShapeDtypeStruct((b, nt, ROUTE_ROWS, ROW_TILE), F32),
            jax.ShapeDtypeStruct((BUCKET_ROWS, ROW_TILE), F32),
        ],
        scratch_shapes=[pltpu.VMEM((BUCKET_ROWS, ROW_TILE), F32)],
        compiler_params=_params(("arbitrary", "arbitrary")),
        name="out_proj_router",
    )(xa, attn_n, rec_n, mods, w_out, n2g, rw_hi, rw_lo, rb, tri)


def _moe_plan(route, counts, n_tiles):
    tm = ROW_TILE
    bucket = route[:, :, 0, :].reshape(-1).astype(jnp.int32)
    rank = route[:, :, 1, :].reshape(-1).astype(jnp.int32)
    cnt = counts[:N_BUCKETS, 0].astype(jnp.int32)
    tiles_per = (cnt + tm - 1) // tm
    tile_end = jnp.cumsum(tiles_per)
    pos = (tile_end - tiles_per)[bucket] * tm + rank
    total = tile_end[-1]
    tile = jnp.arange(n_tiles, dtype=jnp.int32)
    tile_bucket = jnp.searchsorted(tile_end, jnp.minimum(tile, total - 1), side="right")
    tile_bucket = jnp.minimum(tile_bucket, N_BUCKETS - 1).astype(jnp.int32)
    group, pair = tile_bucket // N_PAIRS, tile_bucket % N_PAIRS
    pairs = [(i, j) for i in range(GROUP_SIZE) for j in range(i + 1, GROUP_SIZE)]
    lo = jnp.array([p[0] for p in pairs], jnp.int32)[pair]
    hi = jnp.array([p[1] for p in pairs], jnp.int32)[pair]
    weights = route[:, :, 2:4, :].transpose(0, 1, 3, 2).reshape(-1, 2)
    w_aug = jnp.pad(weights, ((0, 0), (0, LANES - 2)))
    return (pos, group * GROUP_SIZE + lo, group * GROUP_SIZE + hi, (tile < total).astype(jnp.int32), w_aug)


def _dispatch_kernel(pos_ref, h_ref, w_ref, xs_in_ref, xs_ref, stage, sem):
    del xs_in_ref
    rows, d = h_ref.shape
    i = pl.program_id(0) * pl.num_programs(1) + pl.program_id(1)
    n = pl.num_programs(0) * pl.num_programs(1)
    slot = i % 2

    def drain(s):
        pltpu.make_async_copy(stage.at[s], xs_ref.at[pl.ds(0, rows), :], sem.at[s]).wait()

    @pl.when(i >= 2)
    def _():
        drain(slot)

    stage[slot, :, 0:d] = h_ref[...]
    stage[slot, :, d:] = w_ref[...]
    base = i * rows

    def issue(r, c):
        p = pos_ref[base + r]
        pltpu.make_async_copy(stage.at[slot, pl.ds(r, 1), :], xs_ref.at[pl.ds(p, 1), :], sem.at[slot]).start()
        return c

    lax.fori_loop(0, rows, issue, 0, unroll=8)

    @pl.when(i == n - 1)
    def _():
        drain(1 - slot)
        drain(slot)


def _dispatch(pos, h2, w_aug, n_rows):
    b, l, d = h2.shape
    nt = l // ROW_TILE
    assert b * nt >= 2
    xs0 = jnp.zeros((n_rows, d + LANES), F32)
    return pl.pallas_call(
        _dispatch_kernel,
        grid_spec=pltpu.PrefetchScalarGridSpec(
            num_scalar_prefetch=1,
            grid=(b, nt),
            in_specs=[
                pl.BlockSpec((None, ROW_TILE, d), lambda i, t, pos: (i, t, 0)),
                pl.BlockSpec((ROW_TILE, LANES), lambda i, t, pos: (i * nt + t, 0)),
                pl.BlockSpec(memory_space=pl.ANY),
            ],
            out_specs=pl.BlockSpec(memory_space=pl.ANY),
            scratch_shapes=[pltpu.VMEM((2, ROW_TILE, d + LANES), F32), pltpu.SemaphoreType.DMA((2,))],
        ),
        out_shape=jax.ShapeDtypeStruct((n_rows, d + LANES), F32),
        input_output_aliases={3: 0},
        compiler_params=_params(("arbitrary", "arbitrary")),
        name="moe_dispatch",
    )(pos, h2, w_aug, xs0)


def _moe_pair_kernel(ea_ref, eb_ref, valid_ref, xs_ref, wgu_a, wd_a, wgu_b, wd_b, y_ref):
    del ea_ref, eb_ref
    i = pl.program_id(0)
    d = y_ref.shape[1]

    @pl.when(valid_ref[i] != 0)
    def _():
        xs = xs_ref[...]
        x = xs[:, :d].astype(BF16)

        def expert(wgu, wd):
            d_exp = wd.shape[0]
            gu = jnp.dot(x, wgu[...], preferred_element_type=F32)
            gate = gu[:, :d_exp]
            hid = gate * _sigmoid(gate) * gu[:, d_exp:]
            return jnp.dot(hid.astype(BF16), wd[...], preferred_element_type=F32)

        y_ref[...] = xs[:, d:d + 1] * expert(wgu_a, wd_a) + xs[:, d + 1:d + 2] * expert(wgu_b, wd_b)

    @pl.when(valid_ref[i] == 0)
    def _():
        y_ref[...] = jnp.zeros_like(y_ref)


def _moe_pairs(xs, tile_ea, tile_eb, tile_valid, w_gu, w_d):
    n_rows, da = xs.shape
    d = da - LANES
    n_exp, _, two_de = w_gu.shape
    tm = ROW_TILE
    return pl.pallas_call(
        _moe_pair_kernel,
        grid_spec=pltpu.PrefetchScalarGridSpec(
            num_scalar_prefetch=3,
            grid=(n_rows // tm,),
            in_specs=[
                pl.BlockSpec((tm, da), lambda i, ea, eb, va: (i, 0)),
                pl.BlockSpec((None, d, two_de), lambda i, ea, eb, va: (ea[i], 0, 0)),
                pl.BlockSpec((None, two_de // 2, d), lambda i, ea, eb, va: (ea[i], 0, 0)),
                pl.BlockSpec((None, d, two_de), lambda i, ea, eb, va: (eb[i], 0, 0)),
                pl.BlockSpec((None, two_de // 2, d), lambda i, ea, eb, va: (eb[i], 0, 0)),
            ],
            out_specs=pl.BlockSpec((tm, d), lambda i, ea, eb, va: (i, 0)),
        ),
        out_shape=jax.ShapeDtypeStruct((n_rows, d), F32),
        compiler_params=_params(("arbitrary",)),
        name="moe_pairs",
    )(tile_ea, tile_eb, tile_valid, xs, w_gu, w_d, w_gu, w_d)


def _combine_kernel(pos_ref, x_ref, mod_ref, g_ref, ys_ref, o_ref, ybuf, sem, *, final):
    rows = x_ref.shape[0]
    i = pl.program_id(0) * pl.num_programs(1) + pl.program_id(1)
    n = pl.num_programs(0) * pl.num_programs(1)
    slot = i % 2

    def gather(step, s):
        base = step * rows

        def issue(r, c):
            p = pos_ref[base + r]
            pltpu.make_async_copy(ys_ref.at[pl.ds(p, 1), :], ybuf.at[s, pl.ds(r, 1), :], sem.at[s]).start()
            return c

        lax.fori_loop(0, rows, issue, 0, unroll=8)

    @pl.when(i == 0)
    def _():
        gather(0, 0)

    @pl.when(i + 1 < n)
    def _():
        gather(i + 1, 1 - slot)

    pltpu.make_async_copy(ys_ref.at[pl.ds(0, rows), :], ybuf.at[slot], sem.at[slot]).wait()
    x2 = x_ref[...] + mod_ref[5:6, :] * ybuf[slot]
    o_ref[...] = _rms(x2) * g_ref[...] if final else x2


def _combine(pos, x1, mods, gain, ys, has_ctx, final):
    b, l, d = x1.shape
    nt = l // ROW_TILE
    ctx_row = lambda i, t: jnp.where(t == 0, b, i) if has_ctx else i
    return pl.pallas_call(
        functools.partial(_combine_kernel, final=final),
        grid_spec=pltpu.PrefetchScalarGridSpec(
            num_scalar_prefetch=1,
            grid=(b, nt),
            in_specs=[
                pl.BlockSpec((None, ROW_TILE, d), lambda i, t, pos: (i, t, 0)),
                pl.BlockSpec((None, N_MOD, d), lambda i, t, pos: (ctx_row(i, t), 0, 0)),
                pl.BlockSpec((1, d), lambda i, t, pos: (0, 0)),
                pl.BlockSpec(memory_space=pl.ANY),
            ],
            out_specs=pl.BlockSpec((None, ROW_TILE, d), lambda i, t, pos: (i, t, 0)),
            scratch_shapes=[pltpu.VMEM((2, ROW_TILE, d), F32), pltpu.SemaphoreType.DMA((2,))],
        ),
        out_shape=jax.ShapeDtypeStruct((b, l, d), F32),
        compiler_params=_params(("arbitrary", "arbitrary")),
        name="moe_combine",
    )(pos, x1, mods, gain, ys)


def _rope_tables(n_ctx, n_lat):
    rows = n_lat // GRID_W
    r, col = jnp.meshgrid(jnp.arange(rows), jnp.arange(GRID_W), indexing="ij")
    r = r.reshape(-1).astype(F32)
    col = col.reshape(-1).astype(F32)
    half = HEAD_DIM // 2
    inv = ROPE_THETA ** (-jnp.arange(0, half, 2, dtype=F32) / half)
    ang_r = r[:, None] * inv
    ang_c = col[:, None] * inv
    cos_h = jnp.concatenate([jnp.cos(ang_r)] * 2 + [jnp.cos(ang_c)] * 2, axis=-1)
    sin_h = jnp.concatenate([-jnp.sin(ang_r), jnp.sin(ang_r), -jnp.sin(ang_c), jnp.sin(ang_c)], axis=-1)
    cos_t = jnp.concatenate([jnp.ones((n_ctx, HEAD_DIM), F32), cos_h], axis=0)
    sin_t = jnp.concatenate([jnp.zeros((n_ctx, HEAD_DIM), F32), sin_h], axis=0)
    reps = LANES // HEAD_DIM
    return jnp.tile(cos_t, (1, reps)), jnp.tile(sin_t, (1, reps))


def _block_diag(w):
    n, d, e = w.shape
    eye = jnp.eye(n, dtype=w.dtype)
    return (w[:, :, None, :] * eye[:, None, :, None]).reshape(n * d, n * e)


def kernel(x, c, ctx, c_ctx, ada_w, ada_b, norm1_g, w_in, q_norm_g, k_norm_g, conv_w, conv_b, lru_wa, lru_ba, lru_wx, lru_bx, lru_lambda, attn_out_g, lru_out_g, w_out, norm2_g, router_w, router_b, exp_w_gate, exp_w_up, exp_w_down, final_g):
    b, s, d = x.shape
    n_ctx = ctx.shape[1]
    l = n_ctx + s
    depth = ada_w.shape[0]
    nt = l // ROW_TILE
    assert n_ctx == ROW_TILE and s % ROW_TILE == 0

    xa = jnp.concatenate([ctx, x], axis=1)
    cvec = jnp.zeros((2 * SUBLANES, d), F32).at[:b].set(c).at[b].set(c_ctx)
    mods_all = _ada_mods(cvec, ada_w, ada_b)[:, :b + 1].reshape(depth, b + 1, N_MOD, d)
    cos_t, sin_t = _rope_tables(n_ctx, s)

    head_avg = _block_diag(jnp.full((N_Q_HEADS, HEAD_DIM, HEAD_DIM), 1.0 / HEAD_DIM, F32)).astype(BF16)
    rw_t = router_w.T
    rw_hi = rw_t.astype(BF16)
    rw_lo = (rw_t - rw_hi.astype(F32)).astype(BF16)
    rb = jnp.broadcast_to(router_b[:, None], (N_EXPERTS, ROW_TILE)).astype(F32)
    tri = jnp.triu(jnp.ones((ROW_TILE, ROW_TILE), BF16), 1)

    qw, kw = ATTN_WIDTH, KV_WIDTH
    out = None
    for li in range(depth):
        wi = w_in[li]
        k0, k1 = wi[:, qw:qw + HEAD_DIM], wi[:, qw + HEAD_DIM:qw + kw]
        v0, v1 = wi[:, qw + kw:qw + kw + HEAD_DIM], wi[:, qw + kw + HEAD_DIM:qw + 2 * kw]
        w_ext = jnp.concatenate([wi[:, :qw], k0, k0, k1, k1, v0, v0, v1, v1, wi[:, qw + 2 * kw:]],
                                axis=1).astype(BF16)
        gq = jnp.tile(q_norm_g[li], N_Q_HEADS)[None, :]
        gk = jnp.tile(k_norm_g[li], 2 * N_KV_HEADS)[None, :]
        mods = mods_all[li]
        q, kd, vd, u, gg = _inproj(xa, mods, norm1_g[li][None, :], w_ext, cos_t, sin_t, gq, gk, head_avg)
        attn_n = _attention(q, kd, vd, attn_out_g[li][None, :], n_ctx)

        w_gate = jnp.concatenate([_block_diag(lru_wa[li, 0]), _block_diag(lru_wx[li, 0]),
                                  _block_diag(lru_wa[li, 1]), _block_diag(lru_wx[li, 1])], axis=1).astype(BF16)
        b_gate = jnp.concatenate([lru_ba[li, 0], lru_bx[li, 0], lru_ba[li, 1], lru_bx[li, 1]])[None, :]
        rec_n = _lru(u, gg, conv_w[li], conv_b[li][None, :], w_gate, b_gate, lru_lambda[li],
                     lru_out_g[li][None, :], n_ctx)

        skip = n_ctx // ROW_TILE if li == depth - 1 else 0
        x1, h2, route, counts = _outproj(xa, attn_n, rec_n, mods, w_out[li].astype(BF16),
                                         norm2_g[li][None, :], rw_hi, rw_lo, rb, tri, skip)
        n_tiles = b * (nt - skip) + N_BUCKETS
        pos, tile_ea, tile_eb, tile_valid, w_aug = _moe_plan(route, counts, n_tiles)
        xs = _dispatch(pos, h2, w_aug, n_tiles * ROW_TILE)
        w_gu = jnp.concatenate([exp_w_gate[li], exp_w_up[li]], axis=-1).astype(BF16)
        ys = _moe_pairs(xs, tile_ea, tile_eb, tile_valid, w_gu, exp_w_down[li].astype(BF16))
        if li == depth - 1:
            out = _combine(pos, x1, mods, final_g[None, :], ys, skip == 0, True)
        else:
            xa = _combine(pos, x1, mods, final_g[None, :], ys, skip == 0, False)
    return out
```

```python
import functools

import jax
import jax.numpy as jnp
from jax import lax
from jax.experimental import pallas as pl
from jax.experimental.pallas import tpu as pltpu

F32 = jnp.float32
BF16 = jnp.bfloat16

HEAD_DIM = 64
N_Q_HEADS = 8
N_KV_HEADS = 2
ATTN_WIDTH = N_Q_HEADS * HEAD_DIM
KV_WIDTH = N_KV_HEADS * HEAD_DIM
LRU_WIDTH = 512
LRU_BLOCKS = 8
CONV_WIDTH = 4
CONV_LEFT = CONV_WIDTH // 2
LRU_C = 8.0
GRID_W = 64
ROPE_THETA = 10000.0
N_EXPERTS = 16
N_GROUPS = 4
GROUP_SIZE = N_EXPERTS // N_GROUPS
N_MOD = 6
NORM_EPS = 1e-6
ATTN_SCALE = HEAD_DIM ** -0.5

LANES = 128
SUBLANES = 8
ROW_TILE = 256
N_PAIRS = GROUP_SIZE * (GROUP_SIZE - 1) // 2
N_BUCKETS = N_GROUPS * N_PAIRS
BUCKET_ROWS = 32
ROUTE_ROWS = 8
VMEM_LIMIT = 60000 * 1024


def _params(sem):
    return pltpu.CompilerParams(dimension_semantics=sem, vmem_limit_bytes=VMEM_LIMIT)


def _sigmoid(x):
    return 1.0 / (1.0 + jnp.exp(-x))


def _rms(x):
    return x * lax.rsqrt(jnp.mean(x * x, axis=-1, keepdims=True) + NORM_EPS)


def _split_bf16(x):
    hi = x.astype(BF16)
    lo = (x - hi.astype(F32)).astype(BF16)
    return hi, lo


def _ada_kernel(c_ref, w_ref, b_ref, o_ref):
    cv = c_ref[...]
    s = cv * _sigmoid(cv)
    o_ref[...] = jnp.dot(s.astype(BF16), w_ref[...].astype(BF16),
                         preferred_element_type=F32) + b_ref[...]


def _ada_mods(cvec, ada_w, ada_b):
    depth, d, n = ada_w.shape
    tn = 512
    rows = cvec.shape[0]
    return pl.pallas_call(
        _ada_kernel,
        grid=(depth, n // tn),
        in_specs=[
            pl.BlockSpec((rows, d), lambda l, j: (0, 0)),
            pl.BlockSpec((None, d, tn), lambda l, j: (l, 0, j)),
            pl.BlockSpec((None, 1, tn), lambda l, j: (l, 0, j)),
        ],
        out_specs=pl.BlockSpec((None, rows, tn), lambda l, j: (l, 0, j)),
        out_shape=jax.ShapeDtypeStruct((depth, rows, n), F32),
        compiler_params=_params(("parallel", "parallel")),
        name="ada_mods",
    )(cvec, ada_w, ada_b.reshape(depth, 1, n))


def _head_rms(t, g_mat, gain):
    hi, lo = _split_bf16(t * t)
    m = (jnp.dot(hi, g_mat, preferred_element_type=F32)
         + jnp.dot(lo, g_mat, preferred_element_type=F32))
    return t * lax.rsqrt(m + NORM_EPS) * gain


def _rope(t, cos_t, sin_t):
    width = t.shape[-1]
    reps = width // LANES
    cos_w = jnp.concatenate([cos_t] * reps, axis=1)
    sin_w = jnp.concatenate([sin_t] * reps, axis=1)
    lane = lax.broadcasted_iota(jnp.int32, t.shape, 1)
    quarter = HEAD_DIM // 4
    first = (lane % (2 * quarter)) < quarter
    partner = jnp.where(first, pltpu.roll(t, width - quarter, 1), pltpu.roll(t, quarter, 1))
    return t * cos_w + partner * sin_w


def _inproj_kernel(x_ref, mod_ref, n1_ref, w_ref, cos_ref, sin_ref, gq_ref, gk_ref, gm_ref,
                   q_ref, k_ref, v_ref, u_ref, gg_ref):
    x = x_ref[...]
    h = _rms(x) * n1_ref[...]
    h = h * (1.0 + mod_ref[1:2, :]) + mod_ref[0:1, :]
    y = jnp.dot(h.astype(BF16), w_ref[...], preferred_element_type=F32)
    kd = 2 * KV_WIDTH
    o_k, o_v, o_u, o_g = ATTN_WIDTH, ATTN_WIDTH + kd, ATTN_WIDTH + 2 * kd, ATTN_WIDTH + 2 * kd + LRU_WIDTH
    cos_t = cos_ref[...]
    sin_t = sin_ref[...]
    g_mat = gm_ref[...]
    q = _rope(_head_rms(y[:, :o_k], g_mat, gq_ref[...]), cos_t, sin_t)
    q_ref[...] = (q * ATTN_SCALE).astype(BF16)
    k = _rope(_head_rms(y[:, o_k:o_v], g_mat[:kd, :kd], gk_ref[...]), cos_t, sin_t)
    k_ref[...] = k.astype(BF16)
    v_ref[...] = y[:, o_v:o_u].T.astype(BF16)
    u_ref[...] = y[:, o_u:o_g]
    gb = y[:, o_g:]
    gg_ref[...] = 0.5 * gb * (1.0 + jnp.tanh(0.7978845608028654 * (gb + 0.044715 * gb * gb * gb)))


def _inproj(xa, mods, n1g, w_ext, cos_t, sin_t, gq, gk, g_mat):
    b, l, d = xa.shape
    nt = l // ROW_TILE
    n_ext = w_ext.shape[1]
    kd = 2 * KV_WIDTH
    row = lambda w: pl.BlockSpec((None, ROW_TILE, w), lambda i, t: (i, t, 0))
    const = lambda shape: pl.BlockSpec(shape, lambda i, t: (0,) * len(shape))
    return pl.pallas_call(
        _inproj_kernel,
        grid=(b, nt),
        in_specs=[
            row(d),
            pl.BlockSpec((None, N_MOD, d), lambda i, t: (jnp.where(t == 0, b, i), 0, 0)),
            const((1, d)),
            const((d, n_ext)),
            pl.BlockSpec((ROW_TILE, LANES), lambda i, t: (t, 0)),
            pl.BlockSpec((ROW_TILE, LANES), lambda i, t: (t, 0)),
            const((1, ATTN_WIDTH)),
            const((1, kd)),
            const((ATTN_WIDTH, ATTN_WIDTH)),
        ],
        out_specs=[row(ATTN_WIDTH), row(kd), pl.BlockSpec((None, kd, ROW_TILE), lambda i, t: (i, 0, t)),
                   row(LRU_WIDTH), row(LRU_WIDTH)],
        out_shape=[
            jax.ShapeDtypeStruct((b, l, ATTN_WIDTH), BF16),
            jax.ShapeDtypeStruct((b, l, kd), BF16),
            jax.ShapeDtypeStruct((b, kd, l), BF16),
            jax.ShapeDtypeStruct((b, l, LRU_WIDTH), F32),
            jax.ShapeDtypeStruct((b, l, LRU_WIDTH), F32),
        ],
        compiler_params=_params(("parallel", "parallel")),
        name="in_proj",
    )(xa, mods, n1g, w_ext, cos_t, sin_t, gq, gk, g_mat)


def _attn_kernel(q_ref, k_ref, v_ref, g_ref, o_ref, *, n_ctx):
    t = pl.program_id(1)
    rows = q_ref.shape[0]
    n_all = k_ref.shape[0]

    def run(n_keys):
        low = lax.broadcasted_iota(jnp.int32, (rows, LANES), 1) < HEAD_DIM
        top = lax.broadcasted_iota(jnp.int32, (LANES, rows), 0) < HEAD_DIM
        n_pairs = N_Q_HEADS // 2
        kv_of = lambda pair: (2 * pair) // (N_Q_HEADS // N_KV_HEADS)

        def scores(pair):
            qp = q_ref[:, pair * LANES:(pair + 1) * LANES]
            zero = jnp.zeros_like(qp)
            q2 = jnp.concatenate([jnp.where(low, qp, zero), jnp.where(low, zero, qp)], axis=0)
            kd = k_ref[0:n_keys, kv_of(pair) * LANES:(kv_of(pair) + 1) * LANES]
            return lax.dot_general(kd, q2, (((1,), (1,)), ((), ())), preferred_element_type=F32)

        def softmax_pv(pair, st):
            vt = v_ref[kv_of(pair) * LANES:(kv_of(pair) + 1) * LANES, 0:n_keys]
            p = jnp.exp(st - jnp.max(st, axis=0, keepdims=True))
            den = jnp.sum(p, axis=0, keepdims=True)
            ot = jnp.dot(vt, p.astype(BF16), preferred_element_type=F32) * (1.0 / den)
            return jnp.where(top, ot[:, :rows], ot[:, rows:]).T

        outs = []
        st = scores(0)
        for pair in range(n_pairs):
            st_next = scores(pair + 1) if pair + 1 < n_pairs else None
            outs.append(softmax_pv(pair, st))
            st = st_next
        a = jnp.concatenate(outs, axis=1)
        o_ref[...] = (_rms(a) * g_ref[...]).astype(BF16)

    @pl.when(t == 0)
    def _():
        run(n_ctx)

    @pl.when(t > 0)
    def _():
        run(n_all)


def _attention(q, kd, vd, gain, n_ctx):
    b, l, w = q.shape
    nt = l // ROW_TILE
    kw = kd.shape[-1]
    return pl.pallas_call(
        functools.partial(_attn_kernel, n_ctx=n_ctx),
        grid=(b, nt),
        in_specs=[
            pl.BlockSpec((None, ROW_TILE, w), lambda i, t: (i, t, 0)),
            pl.BlockSpec((None, l, kw), lambda i, t: (i, 0, 0)),
            pl.BlockSpec((None, kw, l), lambda i, t: (i, 0, 0)),
            pl.BlockSpec((1, w), lambda i, t: (0, 0)),
        ],
        out_specs=pl.BlockSpec((None, ROW_TILE, w), lambda i, t: (i, t, 0)),
        out_shape=jax.ShapeDtypeStruct((b, l, w), BF16),
        compiler_params=_params(("parallel", "parallel")),
        name="attention",
    )(q, kd, vd, gain)


def _lru_kernel(u_ref, gg_ref, cw_ref, cb_ref, wg_ref, bg_ref, lam_ref, og_ref, o_ref,
                upad, a_f, b_f, a_r, b_r, *, n_ctx):
    l, w = u_ref.shape
    pad = SUBLANES
    zeros_pad = jnp.zeros((pad, w), F32)
    upad[0:pad, :] = zeros_pad
    upad[pad + l:2 * pad + l, :] = zeros_pad
    for r0 in range(0, l, ROW_TILE):
        upad[pad + r0:pad + r0 + ROW_TILE, :] = u_ref[r0:r0 + ROW_TILE, :]

    neg_lam = -lam_ref[...]
    softplus = jnp.maximum(neg_lam, 0.0) + jnp.log1p(jnp.exp(-jnp.abs(neg_lam)))
    cw = cw_ref[...]
    cb = cb_ref[...]

    for r0 in range(0, l, ROW_TILE):
        row = r0 + lax.broadcasted_iota(jnp.int32, (ROW_TILE, w), 0)
        is_lat = row >= n_ctx
        near_boundary = r0 - CONV_WIDTH < n_ctx < r0 + ROW_TILE + CONV_WIDTH
        uc = jnp.zeros((ROW_TILE, w), F32) + cb
        for j in range(CONV_WIDTH):
            off = j - CONV_LEFT
            tap = upad[pad + r0 + off:pad + r0 + off + ROW_TILE, :]
            if off != 0 and near_boundary:
                tap = jnp.where(((row + off) >= n_ctx) == is_lat, tap, 0.0)
            uc = uc + tap * cw[j:j + 1, :]
        z = jnp.dot(uc.astype(BF16), wg_ref[...], preferred_element_type=F32) + bg_ref[...]
        for d, (a_ref, b_ref) in enumerate(((a_f, b_f), (a_r, b_r))):
            base = 2 * d * w
            r_gate = _sigmoid(z[:, base:base + w])
            i_gate = _sigmoid(z[:, base + w:base + 2 * w])
            log_a = (-LRU_C * r_gate) * softplus[d:d + 1, :]
            a = jnp.exp(log_a)
            one_minus_a2 = -jnp.tanh(log_a) * (a * a + 1.0)
            bb = jnp.sqrt(one_minus_a2) * (i_gate * uc)
            a_ref[r0:r0 + ROW_TILE, :] = a
            b_ref[r0:r0 + ROW_TILE, :] = bb

    sub = lax.broadcasted_iota(jnp.int32, (SUBLANES, w), 0)

    def tile_scan(a_ref, b_ref, blk, carry, reverse):
        rows = pl.ds(pl.multiple_of(blk * SUBLANES, SUBLANES), SUBLANES)
        a = a_ref[rows, :]
        b = b_ref[rows, :]
        s = 1
        while s < SUBLANES:
            keep = (sub < SUBLANES - s) if reverse else (sub >= s)
            shift = SUBLANES - s if reverse else s
            b = b + a * jnp.where(keep, pltpu.roll(b, shift, 0), 0.0)
            a = a * jnp.where(keep, pltpu.roll(a, shift, 0), 1.0)
            s *= 2
        h = b + a * carry
        b_ref[rows, :] = h
        last = h[0:1, :] if reverse else h[SUBLANES - 1:SUBLANES, :]
        return jnp.broadcast_to(last, h.shape)

    n_blk = l // SUBLANES
    c_blk = n_ctx // SUBLANES

    def ctx_body(i, carry):
        cf, cr = carry
        return tile_scan(a_f, b_f, i, cf, False), tile_scan(a_r, b_r, c_blk - 1 - i, cr, True)

    def lat_body(i, carry):
        cf, cr = carry
        return tile_scan(a_f, b_f, i, cf, False), tile_scan(a_r, b_r, n_blk + c_blk - 1 - i, cr, True)

    zero = jnp.zeros((SUBLANES, w), F32)
    carry = lax.fori_loop(0, c_blk, ctx_body, (zero, zero), unroll=2)
    lax.fori_loop(c_blk, n_blk, lat_body, carry, unroll=2)

    gain = og_ref[...]
    for r0 in range(0, l, ROW_TILE):
        rows = slice(r0, r0 + ROW_TILE)
        h = b_f[rows, :] + b_r[rows, :]
        o_ref[rows, :] = (_rms(h * gg_ref[rows, :]) * gain).astype(BF16)


def _lru(u, gg, conv_w, conv_b, w_gate, b_gate, lam, out_g, n_ctx):
    b, l, w = u.shape
    const = lambda shape: pl.BlockSpec(shape, lambda i: (0,) * len(shape))
    seq = pl.BlockSpec((None, l, w), lambda i: (i, 0, 0))
    return pl.pallas_call(
        functools.partial(_lru_kernel, n_ctx=n_ctx),
        grid=(b,),
        in_specs=[seq, seq, const(conv_w.shape), const((1, w)), const(w_gate.shape),
                  const(b_gate.shape), const(lam.shape), const((1, w))],
        out_specs=seq,
        out_shape=jax.ShapeDtypeStruct((b, l, w), BF16),
        scratch_shapes=[pltpu.VMEM((l + 2 * SUBLANES, w), F32)] + [pltpu.VMEM((l, w), F32)] * 4,
        compiler_params=_params(("parallel",)),
        name="rg_lru",
    )(u, gg, conv_w, conv_b, w_gate, b_gate, lam, out_g)


def _route(scores, biased):
    group_score = []
    for g in range(N_GROUPS):
        v = biased[g * GROUP_SIZE:(g + 1) * GROUP_SIZE]
        best = None
        for i in range(GROUP_SIZE):
            for j in range(i + 1, GROUP_SIZE):
                pair = v[i] + v[j]
                best = pair if best is None else jnp.maximum(best, pair)
        group_score.append(best)
    gid = jnp.zeros_like(group_score[0], dtype=jnp.int32)
    gbest = group_score[0]
    for g in range(1, N_GROUPS):
        upd = group_score[g] > gbest
        gid = jnp.where(upd, g, gid)
        gbest = jnp.where(upd, group_score[g], gbest)

    def pick(rows, j):
        out = rows[j]
        for g in range(1, N_GROUPS):
            out = jnp.where(gid == g, rows[g * GROUP_SIZE + j], out)
        return out

    v = [pick(biased, j) for j in range(GROUP_SIZE)]
    s = [pick(scores, j) for j in range(GROUP_SIZE)]
    i1 = jnp.zeros_like(gid)
    m1 = v[0]
    for j in range(1, GROUP_SIZE):
        upd = v[j] > m1
        i1 = jnp.where(upd, j, i1)
        m1 = jnp.where(upd, v[j], m1)
    i2 = jnp.zeros_like(gid)
    m2 = jnp.full_like(m1, -jnp.inf)
    for j in range(GROUP_SIZE):
        upd = (i1 != j) & (v[j] > m2)
        i2 = jnp.where(upd, j, i2)
        m2 = jnp.where(upd, v[j], m2)
    w1 = s[0]
    w2 = s[0]
    for j in range(1, GROUP_SIZE):
        w1 = jnp.where(i1 == j, s[j], w1)
        w2 = jnp.where(i2 == j, s[j], w2)
    den = w1 + w2
    w1 = w1 / den
    w2 = w2 / den
    lo = jnp.minimum(i1, i2)
    hi = jnp.maximum(i1, i2)
    pair = hi - lo - 1
    for k in range(1, GROUP_SIZE - 1):
        pair = pair + jnp.where(lo >= k, GROUP_SIZE - k, 0)
    first_is_lo = i1 < i2
    return gid * N_PAIRS + pair, jnp.where(first_is_lo, w1, w2), jnp.where(first_is_lo, w2, w1)


def _outproj_kernel(x_ref, a_ref, r_ref, mod_ref, w_ref, n2_ref, rwh_ref, rwl_ref, rb_ref, tri_ref,
                    x1_ref, h2_ref, route_ref, counts_ref, cnt_ref):
    rows = x_ref.shape[0]
    half = a_ref.shape[-1]
    mix = (jnp.dot(a_ref[...], w_ref[0:half, :], preferred_element_type=F32)
           + jnp.dot(r_ref[...], w_ref[half:, :], preferred_element_type=F32))
    x1 = x_ref[...] + mod_ref[2:3, :] * mix
    x1_ref[...] = x1
    h2 = _rms(x1) * n2_ref[...]
    h2 = h2 * (1.0 + mod_ref[4:5, :]) + mod_ref[3:4, :]
    h2_ref[...] = h2
    hi, lo = _split_bf16(h2)
    nt = (((1,), (1,)), ((), ()))
    logits = (lax.dot_general(rwh_ref[...], hi, nt, preferred_element_type=F32)
              + lax.dot_general(rwh_ref[...], lo, nt, preferred_element_type=F32)
              + lax.dot_general(rwl_ref[...], hi, nt, preferred_element_type=F32))
    scores = _sigmoid(logits)
    biased = scores + rb_ref[...]
    bucket, w_lo, w_hi = _route([scores[e:e + 1, :] for e in range(N_EXPERTS)],
                                [biased[e:e + 1, :] for e in range(N_EXPERTS)])

    @pl.when((pl.program_id(0) == 0) & (pl.program_id(1) == 0))
    def _():
        cnt_ref[...] = jnp.zeros_like(cnt_ref)

    sub = lax.broadcasted_iota(jnp.int32, (BUCKET_ROWS, rows), 0)
    onehot = jnp.where(sub == bucket, 1.0, 0.0)
    before = jnp.dot(onehot.astype(BF16), tri_ref[...], preferred_element_type=F32) + cnt_ref[...]
    rank = jnp.sum(onehot * before, axis=0, keepdims=True)
    total = cnt_ref[...] + jnp.sum(onehot, axis=1, keepdims=True)
    cnt_ref[...] = total
    counts_ref[...] = total

    route_ref[0:1, :] = bucket.astype(F32)
    route_ref[1:2, :] = rank
    route_ref[2:3, :] = w_lo
    route_ref[3:4, :] = w_hi
    route_ref[4:, :] = jnp.zeros((ROUTE_ROWS - 4, rows), F32)


def _outproj(xa, attn_n, rec_n, mods, w_out, n2g, rw_hi, rw_lo, rb, tri, skip):
    b, l, d = xa.shape
    nt = l // ROW_TILE - skip
    half = attn_n.shape[-1]
    row = lambda w: pl.BlockSpec((None, ROW_TILE, w), lambda i, t: (i, t + skip, 0))
    out_row = pl.BlockSpec((None, ROW_TILE, d), lambda i, t: (i, t, 0))
    const = lambda shape: pl.BlockSpec(shape, lambda i, t: (0,) * len(shape))
    return pl.pallas_call(
        _outproj_kernel,
        grid=(b, nt),
        in_specs=[
            row(d), row(half), row(half),
            pl.BlockSpec((None, N_MOD, d), lambda i, t: (jnp.where(t + skip == 0, b, i), 0, 0)),
            const(w_out.shape), const((1, d)), const(rw_hi.shape), const(rw_lo.shape), const(rb.shape),
            const(tri.shape),
        ],
        out_specs=[out_row, out_row,
                   pl.BlockSpec((None, None, ROUTE_ROWS, ROW_TILE), lambda i, t: (i, t, 0, 0)),
                   const((BUCKET_ROWS, ROW_TILE))],
        out_shape=[
            jax.ShapeDtypeStruct((b, nt * ROW_TILE, d), F32),
            jax.ShapeDtypeStruct((b, nt * ROW_TILE, d), F32),
            jax.ShapeDtypeStruct((b, nt, ROUTE_ROWS, ROW_TILE), F32),
            jax.ShapeDtypeStruct((BUCKET_ROWS, ROW_TILE), F32),
        ],
        scratch_shapes=[pltpu.VMEM((BUCKET_ROWS, ROW_TILE), F32)],
        compiler_params=_params(("arbitrary", "arbitrary")),
        name="out_proj_router",
    )(xa, attn_n, rec_n, mods, w_out, n2g, rw_hi, rw_lo, rb, tri)


def _moe_plan(route, counts, n_tiles):
    tm = ROW_TILE
    bucket = route[:, :, 0, :].reshape(-1).astype(jnp.int32)
    rank = route[:, :, 1, :].reshape(-1).astype(jnp.int32)
    cnt = counts[:N_BUCKETS, 0].astype(jnp.int32)
    tiles_per = (cnt + tm - 1) // tm
    tile_end = jnp.cumsum(tiles_per)
    pos = (tile_end - tiles_per)[bucket] * tm + rank
    total = tile_end[-1]
    tile = jnp.arange(n_tiles, dtype=jnp.int32)
    last = jnp.minimum(tile, total - 1)
    tile_bucket = jnp.sum((tile_end[None, :] <= last[:, None]).astype(jnp.int32), axis=1)
    tile_bucket = jnp.minimum(tile_bucket, N_BUCKETS - 1)
    group, pair = tile_bucket // N_PAIRS, tile_bucket % N_PAIRS
    pairs = [(i, j) for i in range(GROUP_SIZE) for j in range(i + 1, GROUP_SIZE)]
    lo = jnp.array([p[0] for p in pairs], jnp.int32)[pair]
    hi = jnp.array([p[1] for p in pairs], jnp.int32)[pair]
    weights = route[:, :, 2:4, :].transpose(0, 1, 3, 2).reshape(-1, 2)
    w_aug = jnp.pad(weights, ((0, 0), (0, LANES - 2)))
    return (pos, group * GROUP_SIZE + lo, group * GROUP_SIZE + hi, (tile < total).astype(jnp.int32), w_aug)


def _dispatch_kernel(pos_ref, h_ref, w_ref, xs_in_ref, xs_ref, stage, sem):
    del xs_in_ref
    rows, d = h_ref.shape
    i = pl.program_id(0) * pl.num_programs(1) + pl.program_id(1)
    n = pl.num_programs(0) * pl.num_programs(1)
    slot = i % 2

    def drain(s):
        pltpu.make_async_copy(stage.at[s], xs_ref.at[pl.ds(0, rows), :], sem.at[s]).wait()

    @pl.when(i >= 2)
    def _():
        drain(slot)

    stage[slot, :, 0:d] = h_ref[...]
    stage[slot, :, d:] = w_ref[...]
    base = i * rows

    def issue(r, c):
        p = pos_ref[base + r]
        pltpu.make_async_copy(stage.at[slot, pl.ds(r, 1), :], xs_ref.at[pl.ds(p, 1), :], sem.at[slot]).start()
        return c

    lax.fori_loop(0, rows, issue, 0, unroll=8)

    @pl.when(i == n - 1)
    def _():
        drain(1 - slot)
        drain(slot)


def _dispatch(pos, h2, w_aug, n_rows):
    b, l, d = h2.shape
    nt = l // ROW_TILE
    assert b * nt >= 2
    xs0 = jnp.zeros((n_rows, d + LANES), F32)
    return pl.pallas_call(
        _dispatch_kernel,
        grid_spec=pltpu.PrefetchScalarGridSpec(
            num_scalar_prefetch=1,
            grid=(b, nt),
            in_specs=[
                pl.BlockSpec((None, ROW_TILE, d), lambda i, t, pos: (i, t, 0)),
                pl.BlockSpec((ROW_TILE, LANES), lambda i, t, pos: (i * nt + t, 0)),
                pl.BlockSpec(memory_space=pl.ANY),
            ],
            out_specs=pl.BlockSpec(memory_space=pl.ANY),
            scratch_shapes=[pltpu.VMEM((2, ROW_TILE, d + LANES), F32), pltpu.SemaphoreType.DMA((2,))],
        ),
        out_shape=jax.ShapeDtypeStruct((n_rows, d + LANES), F32),
        input_output_aliases={3: 0},
        compiler_params=_params(("arbitrary", "arbitrary")),
        name="moe_dispatch",
    )(pos, h2, w_aug, xs0)


def _moe_pair_kernel(ea_ref, eb_ref, valid_ref, xs_ref, wgu_a, wd_a, wgu_b, wd_b, y_ref):
    del ea_ref, eb_ref
    i = pl.program_id(0)
    d = y_ref.shape[1]

    @pl.when(valid_ref[i] != 0)
    def _():
        xs = xs_ref[...]
        x = xs[:, :d].astype(BF16)

        def expert(wgu, wd):
            d_exp = wd.shape[0]
            gu = jnp.dot(x, wgu[...], preferred_element_type=F32)
            gate = gu[:, :d_exp]
            hid = gate * _sigmoid(gate) * gu[:, d_exp:]
            return jnp.dot(hid.astype(BF16), wd[...], preferred_element_type=F32)

        y_ref[...] = xs[:, d:d + 1] * expert(wgu_a, wd_a) + xs[:, d + 1:d + 2] * expert(wgu_b, wd_b)

    @pl.when(valid_ref[i] == 0)
    def _():
        y_ref[...] = jnp.zeros_like(y_ref)


def _moe_pairs(xs, tile_ea, tile_eb, tile_valid, w_gu, w_d):
    n_rows, da = xs.shape
    d = da - LANES
    n_exp, _, two_de = w_gu.shape
    tm = ROW_TILE
    return pl.pallas_call(
        _moe_pair_kernel,
        grid_spec=pltpu.PrefetchScalarGridSpec(
            num_scalar_prefetch=3,
            grid=(n_rows // tm,),
            in_specs=[
                pl.BlockSpec((tm, da), lambda i, ea, eb, va: (i, 0)),
                pl.BlockSpec((None, d, two_de), lambda i, ea, eb, va: (ea[i], 0, 0)),
                pl.BlockSpec((None, two_de // 2, d), lambda i, ea, eb, va: (ea[i], 0, 0)),
                pl.BlockSpec((None, d, two_de), lambda i, ea, eb, va: (eb[i], 0, 0)),
                pl.BlockSpec((None, two_de // 2, d), lambda i, ea, eb, va: (eb[i], 0, 0)),
            ],
            out_specs=pl.BlockSpec((tm, d), lambda i, ea, eb, va: (i, 0)),
        ),
        out_shape=jax.ShapeDtypeStruct((n_rows, d), F32),
        compiler_params=_params(("arbitrary",)),
        name="moe_pairs",
    )(tile_ea, tile_eb, tile_valid, xs, w_gu, w_d, w_gu, w_d)


def _combine_kernel(pos_ref, x_ref, mod_ref, g_ref, ys_ref, o_ref, ybuf, sem, *, final):
    rows = x_ref.shape[0]
    i = pl.program_id(0) * pl.num_programs(1) + pl.program_id(1)
    n = pl.num_programs(0) * pl.num_programs(1)
    slot = i % 2

    def gather(step, s):
        base = step * rows

        def issue(r, c):
            p = pos_ref[base + r]
            pltpu.make_async_copy(ys_ref.at[pl.ds(p, 1), :], ybuf.at[s, pl.ds(r, 1), :], sem.at[s]).start()
            return c

        lax.fori_loop(0, rows, issue, 0, unroll=8)

    @pl.when(i == 0)
    def _():
        gather(0, 0)

    @pl.when(i + 1 < n)
    def _():
        gather(i + 1, 1 - slot)

    pltpu.make_async_copy(ys_ref.at[pl.ds(0, rows), :], ybuf.at[slot], sem.at[slot]).wait()
    x2 = x_ref[...] + mod_ref[5:6, :] * ybuf[slot]
    o_ref[...] = _rms(x2) * g_ref[...] if final else x2


def _combine(pos, x1, mods, gain, ys, has_ctx, final):
    b, l, d = x1.shape
    nt = l // ROW_TILE
    ctx_row = lambda i, t: jnp.where(t == 0, b, i) if has_ctx else i
    return pl.pallas_call(
        functools.partial(_combine_kernel, final=final),
        grid_spec=pltpu.PrefetchScalarGridSpec(
            num_scalar_prefetch=1,
            grid=(b, nt),
            in_specs=[
                pl.BlockSpec((None, ROW_TILE, d), lambda i, t, pos: (i, t, 0)),
                pl.BlockSpec((None, N_MOD, d), lambda i, t, pos: (ctx_row(i, t), 0, 0)),
                pl.BlockSpec((1, d), lambda i, t, pos: (0, 0)),
                pl.BlockSpec(memory_space=pl.ANY),
            ],
            out_specs=pl.BlockSpec((None, ROW_TILE, d), lambda i, t, pos: (i, t, 0)),
            scratch_shapes=[pltpu.VMEM((2, ROW_TILE, d), F32), pltpu.SemaphoreType.DMA((2,))],
        ),
        out_shape=jax.ShapeDtypeStruct((b, l, d), F32),
        compiler_params=_params(("arbitrary", "arbitrary")),
        name="moe_combine",
    )(pos, x1, mods, gain, ys)


def _rope_tables(n_ctx, n_lat):
    rows = n_lat // GRID_W
    r, col = jnp.meshgrid(jnp.arange(rows), jnp.arange(GRID_W), indexing="ij")
    r = r.reshape(-1).astype(F32)
    col = col.reshape(-1).astype(F32)
    half = HEAD_DIM // 2
    inv = ROPE_THETA ** (-jnp.arange(0, half, 2, dtype=F32) / half)
    ang_r = r[:, None] * inv
    ang_c = col[:, None] * inv
    cos_h = jnp.concatenate([jnp.cos(ang_r)] * 2 + [jnp.cos(ang_c)] * 2, axis=-1)
    sin_h = jnp.concatenate([-jnp.sin(ang_r), jnp.sin(ang_r), -jnp.sin(ang_c), jnp.sin(ang_c)], axis=-1)
    cos_t = jnp.concatenate([jnp.ones((n_ctx, HEAD_DIM), F32), cos_h], axis=0)
    sin_t = jnp.concatenate([jnp.zeros((n_ctx, HEAD_DIM), F32), sin_h], axis=0)
    reps = LANES // HEAD_DIM
    return jnp.tile(cos_t, (1, reps)), jnp.tile(sin_t, (1, reps))


def _block_diag(w):
    n, d, e = w.shape
    eye = jnp.eye(n, dtype=w.dtype)
    return (w[:, :, None, :] * eye[:, None, :, None]).reshape(n * d, n * e)


def kernel(x, c, ctx, c_ctx, ada_w, ada_b, norm1_g, w_in, q_norm_g, k_norm_g, conv_w, conv_b, lru_wa, lru_ba, lru_wx, lru_bx, lru_lambda, attn_out_g, lru_out_g, w_out, norm2_g, router_w, router_b, exp_w_gate, exp_w_up, exp_w_down, final_g):
    b, s, d = x.shape
    n_ctx = ctx.shape[1]
    l = n_ctx + s
    depth = ada_w.shape[0]
    nt = l // ROW_TILE
    assert n_ctx == ROW_TILE and s % ROW_TILE == 0

    xa = jnp.concatenate([ctx, x], axis=1)
    cvec = jnp.zeros((2 * SUBLANES, d), F32).at[:b].set(c).at[b].set(c_ctx)
    mods_all = _ada_mods(cvec, ada_w, ada_b)[:, :b + 1].reshape(depth, b + 1, N_MOD, d)
    cos_t, sin_t = _rope_tables(n_ctx, s)

    head_avg = _block_diag(jnp.full((N_Q_HEADS, HEAD_DIM, HEAD_DIM), 1.0 / HEAD_DIM, F32)).astype(BF16)
    rw_t = router_w.T
    rw_hi = rw_t.astype(BF16)
    rw_lo = (rw_t - rw_hi.astype(F32)).astype(BF16)
    rb = jnp.broadcast_to(router_b[:, None], (N_EXPERTS, ROW_TILE)).astype(F32)
    tri = jnp.triu(jnp.ones((ROW_TILE, ROW_TILE), BF16), 1)

    qw, kw = ATTN_WIDTH, KV_WIDTH
    out = None
    for li in range(depth):
        wi = w_in[li]
        k0, k1 = wi[:, qw:qw + HEAD_DIM], wi[:, qw + HEAD_DIM:qw + kw]
        v0, v1 = wi[:, qw + kw:qw + kw + HEAD_DIM], wi[:, qw + kw + HEAD_DIM:qw + 2 * kw]
        w_ext = jnp.concatenate([wi[:, :qw], k0, k0, k1, k1, v0, v0, v1, v1, wi[:, qw + 2 * kw:]],
                                axis=1).astype(BF16)
        gq = jnp.tile(q_norm_g[li], N_Q_HEADS)[None, :]
        gk = jnp.tile(k_norm_g[li], 2 * N_KV_HEADS)[None, :]
        mods = mods_all[li]
        q, kd, vd, u, gg = _inproj(xa, mods, norm1_g[li][None, :], w_ext, cos_t, sin_t, gq, gk, head_avg)
        attn_n = _attention(q, kd, vd, attn_out_g[li][None, :], n_ctx)

        w_gate = jnp.concatenate([_block_diag(lru_wa[li, 0]), _block_diag(lru_wx[li, 0]),
                                  _block_diag(lru_wa[li, 1]), _block_diag(lru_wx[li, 1])], axis=1).astype(BF16)
        b_gate = jnp.concatenate([lru_ba[li, 0], lru_bx[li, 0], lru_ba[li, 1], lru_bx[li, 1]])[None, :]
        rec_n = _lru(u, gg, conv_w[li], conv_b[li][None, :], w_gate, b_gate, lru_lambda[li],
                     lru_out_g[li][None, :], n_ctx)

        skip = n_ctx // ROW_TILE if li == depth - 1 else 0
        x1, h2, route, counts = _outproj(xa, attn_n, rec_n, mods, w_out[li].astype(BF16),
                                         norm2_g[li][None, :], rw_hi, rw_lo, rb, tri, skip)
        n_tiles = b * (nt - skip) + N_BUCKETS
        pos, tile_ea, tile_eb, tile_valid, w_aug = _moe_plan(route, counts, n_tiles)
        xs = _dispatch(pos, h2, w_aug, n_tiles * ROW_TILE)
        w_gu = jnp.concatenate([exp_w_gate[li], exp_w_up[li]], axis=-1).astype(BF16)
        ys = _moe_pairs(xs, tile_ea, tile_eb, tile_valid, w_gu, exp_w_down[li].astype(BF16))
        if li == depth - 1:
            out = _combine(pos, x1, mods, final_g[None, :], ys, skip == 0, True)
        else:
            xa = _combine(pos, x1, mods, final_g[None, :], ys, skip == 0, False)
    return out
```

```python
import functools

import jax
import jax.numpy as jnp
from jax import lax
from jax.experimental import pallas as pl
from jax.experimental.pallas import tpu as pltpu

F32 = jnp.float32
BF16 = jnp.bfloat16

HEAD_DIM = 64
N_Q_HEADS = 8
N_KV_HEADS = 2
ATTN_WIDTH = N_Q_HEADS * HEAD_DIM
KV_WIDTH = N_KV_HEADS * HEAD_DIM
LRU_WIDTH = 512
LRU_BLOCKS = 8
CONV_WIDTH = 4
CONV_LEFT = CONV_WIDTH // 2
LRU_C = 8.0
GRID_W = 64
ROPE_THETA = 10000.0
N_EXPERTS = 16
N_GROUPS = 4
GROUP_SIZE = N_EXPERTS // N_GROUPS
N_MOD = 6
NORM_EPS = 1e-6
ATTN_SCALE = HEAD_DIM ** -0.5
LOG2_E = 1.4426950408889634

LANES = 128
SUBLANES = 8
ROW_TILE = 256
N_PAIRS = GROUP_SIZE * (GROUP_SIZE - 1) // 2
N_BUCKETS = N_GROUPS * N_PAIRS
BUCKET_ROWS = 32
ROUTE_ROWS = 8
VMEM_LIMIT = 60000 * 1024


def _params(sem):
    return pltpu.CompilerParams(dimension_semantics=sem, vmem_limit_bytes=VMEM_LIMIT)


def _sigmoid(x):
    return 1.0 / (1.0 + jnp.exp(-x))


def _rms(x):
    return x * lax.rsqrt(jnp.mean(x * x, axis=-1, keepdims=True) + NORM_EPS)


def _split_bf16(x):
    hi = x.astype(BF16)
    lo = (x - hi.astype(F32)).astype(BF16)
    return hi, lo


def _ada_kernel(c_ref, w_ref, b_ref, o_ref):
    cv = c_ref[...]
    s = cv * _sigmoid(cv)
    o_ref[...] = jnp.dot(s.astype(BF16), w_ref[...].astype(BF16),
                         preferred_element_type=F32) + b_ref[...]


def _ada_mods(cvec, ada_w, ada_b):
    depth, d, n = ada_w.shape
    tn = 512
    rows = cvec.shape[0]
    return pl.pallas_call(
        _ada_kernel,
        grid=(depth, n // tn),
        in_specs=[
            pl.BlockSpec((rows, d), lambda l, j: (0, 0)),
            pl.BlockSpec((None, d, tn), lambda l, j: (l, 0, j)),
            pl.BlockSpec((None, 1, tn), lambda l, j: (l, 0, j)),
        ],
        out_specs=pl.BlockSpec((None, rows, tn), lambda l, j: (l, 0, j)),
        out_shape=jax.ShapeDtypeStruct((depth, rows, n), F32),
        compiler_params=_params(("parallel", "parallel")),
        name="ada_mods",
    )(cvec, ada_w, ada_b.reshape(depth, 1, n))


def _head_rms(t, g_mat, gain):
    hi, lo = _split_bf16(t * t)
    m = (jnp.dot(hi, g_mat, preferred_element_type=F32)
         + jnp.dot(lo, g_mat, preferred_element_type=F32))
    return t * lax.rsqrt(m + NORM_EPS) * gain


def _rope(t, cos_t, sin_t):
    width = t.shape[-1]
    reps = width // LANES
    cos_w = jnp.concatenate([cos_t] * reps, axis=1)
    sin_w = jnp.concatenate([sin_t] * reps, axis=1)
    lane = lax.broadcasted_iota(jnp.int32, t.shape, 1)
    quarter = HEAD_DIM // 4
    first = (lane % (2 * quarter)) < quarter
    partner = jnp.where(first, pltpu.roll(t, width - quarter, 1), pltpu.roll(t, quarter, 1))
    return t * cos_w + partner * sin_w


def _inproj_kernel(lat_ref, ctx_ref, mod_ref, n1_ref, w_ref, cos_ref, sin_ref, gq_ref, gk_ref, gm_ref,
                   q_ref, k_ref, v_ref, u_ref, gg_ref):
    x = jnp.where(pl.program_id(1) == 0, ctx_ref[...], lat_ref[...])
    h = _rms(x) * n1_ref[...]
    h = h * (1.0 + mod_ref[1:2, :]) + mod_ref[0:1, :]
    y = jnp.dot(h.astype(BF16), w_ref[...], preferred_element_type=F32)
    kd = 2 * KV_WIDTH
    o_k, o_v, o_u, o_g = ATTN_WIDTH, ATTN_WIDTH + kd, ATTN_WIDTH + 2 * kd, ATTN_WIDTH + 2 * kd + LRU_WIDTH
    cos_t = cos_ref[...]
    sin_t = sin_ref[...]
    g_mat = gm_ref[...]
    q = _rope(_head_rms(y[:, :o_k], g_mat, gq_ref[...]), cos_t, sin_t)
    q_ref[...] = (q * (ATTN_SCALE * LOG2_E)).astype(BF16)
    k = _rope(_head_rms(y[:, o_k:o_v], g_mat[:kd, :kd], gk_ref[...]), cos_t, sin_t)
    k_ref[...] = k.astype(BF16)
    vt = y[:, o_v:o_u].T
    sub = lax.broadcasted_iota(jnp.int32, vt.shape, 0)
    v_ref[...] = jnp.where(sub % LANES < HEAD_DIM, vt, 1.0).astype(BF16)
    u_ref[...] = y[:, o_u:o_g]
    gb = y[:, o_g:]
    gg_ref[...] = 0.5 * gb * (1.0 + jnp.tanh(0.7978845608028654 * (gb + 0.044715 * gb * gb * gb)))


def _seq_specs(lat_off, d):
    lat = pl.BlockSpec((None, ROW_TILE, d), lambda i, t: (i, jnp.maximum(t - lat_off, 0), 0))
    ctx = pl.BlockSpec((None, ROW_TILE, d), lambda i, t: (i, 0, 0))
    return lat, ctx


def _inproj(x_lat, x_ctx, lat_off, mods, n1g, w_ext, cos_t, sin_t, gq, gk, g_mat):
    b, _, d = x_lat.shape
    l = x_lat.shape[1] + lat_off * x_ctx.shape[1]
    nt = l // ROW_TILE
    n_ext = w_ext.shape[1]
    kd = 2 * KV_WIDTH
    row = lambda w: pl.BlockSpec((None, ROW_TILE, w), lambda i, t: (i, t, 0))
    const = lambda shape: pl.BlockSpec(shape, lambda i, t: (0,) * len(shape))
    return pl.pallas_call(
        _inproj_kernel,
        grid=(b, nt),
        in_specs=[
            *_seq_specs(lat_off, d),
            pl.BlockSpec((None, N_MOD, d), lambda i, t: (jnp.where(t == 0, b, i), 0, 0)),
            const((1, d)),
            const((d, n_ext)),
            pl.BlockSpec((ROW_TILE, LANES), lambda i, t: (t, 0)),
            pl.BlockSpec((ROW_TILE, LANES), lambda i, t: (t, 0)),
            const((1, ATTN_WIDTH)),
            const((1, kd)),
            const((ATTN_WIDTH, ATTN_WIDTH)),
        ],
        out_specs=[row(ATTN_WIDTH), row(kd), pl.BlockSpec((None, kd, ROW_TILE), lambda i, t: (i, 0, t)),
                   row(LRU_WIDTH), row(LRU_WIDTH)],
        out_shape=[
            jax.ShapeDtypeStruct((b, l, ATTN_WIDTH), BF16),
            jax.ShapeDtypeStruct((b, l, kd), BF16),
            jax.ShapeDtypeStruct((b, kd, l), BF16),
            jax.ShapeDtypeStruct((b, l, LRU_WIDTH), F32),
            jax.ShapeDtypeStruct((b, l, LRU_WIDTH), F32),
        ],
        compiler_params=_params(("parallel", "parallel")),
        name="in_proj",
    )(x_lat, x_ctx, mods, n1g, w_ext, cos_t, sin_t, gq, gk, g_mat)


def _attn_kernel(q_ref, k_ref, v_ref, g_ref, o_ref, *, n_ctx):
    t = pl.program_id(1)
    rows = q_ref.shape[0]
    n_all = k_ref.shape[0]

    def run(n_keys):
        low = lax.broadcasted_iota(jnp.int32, (rows, LANES), 1) < HEAD_DIM
        n_pairs = N_Q_HEADS // 2
        kv_of = lambda pair: (2 * pair) // (N_Q_HEADS // N_KV_HEADS)

        def scores(pair):
            qp = q_ref[:, pair * LANES:(pair + 1) * LANES]
            zero = jnp.zeros_like(qp)
            q2 = jnp.concatenate([jnp.where(low, qp, zero), jnp.where(low, zero, qp)], axis=0)
            kd = k_ref[0:n_keys, kv_of(pair) * LANES:(kv_of(pair) + 1) * LANES]
            return lax.dot_general(kd, q2, (((1,), (1,)), ((), ())), preferred_element_type=F32)

        def softmax_pv(pair, st):
            vt = v_ref[kv_of(pair) * LANES:(kv_of(pair) + 1) * LANES, 0:n_keys]
            p = jnp.exp2(st - jnp.max(st, axis=0, keepdims=True))
            ot = jnp.dot(vt, p.astype(BF16), preferred_element_type=F32)
            on = ot[:HEAD_DIM, :] * (1.0 / ot[HEAD_DIM:HEAD_DIM + 1, :])
            return jnp.concatenate([on[:, :rows], on[:, rows:]], axis=0).T

        outs = []
        st = scores(0)
        for pair in range(n_pairs):
            st_next = scores(pair + 1) if pair + 1 < n_pairs else None
            outs.append(softmax_pv(pair, st))
            st = st_next
        a = jnp.concatenate(outs, axis=1)
        o_ref[...] = (_rms(a) * g_ref[...]).astype(BF16)

    @pl.when(t == 0)
    def _():
        run(n_ctx)

    @pl.when(t > 0)
    def _():
        run(n_all)


def _attention(q, kd, vd, gain, n_ctx):
    b, l, w = q.shape
    nt = l // ROW_TILE
    kw = kd.shape[-1]
    return pl.pallas_call(
        functools.partial(_attn_kernel, n_ctx=n_ctx),
        grid=(b, nt),
        in_specs=[
            pl.BlockSpec((None, ROW_TILE, w), lambda i, t: (i, t, 0)),
            pl.BlockSpec((None, l, kw), lambda i, t: (i, 0, 0)),
            pl.BlockSpec((None, kw, l), lambda i, t: (i, 0, 0)),
            pl.BlockSpec((1, w), lambda i, t: (0, 0)),
        ],
        out_specs=pl.BlockSpec((None, ROW_TILE, w), lambda i, t: (i, t, 0)),
        out_shape=jax.ShapeDtypeStruct((b, l, w), BF16),
        compiler_params=_params(("parallel", "parallel")),
        name="attention",
    )(q, kd, vd, gain)


def _lru_kernel(u_ref, gg_ref, cw_ref, cb_ref, wg_ref, bg_ref, lam_ref, og_ref, o_ref,
                upad, a_f, b_f, a_r, b_r, *, n_ctx):
    l, w = u_ref.shape
    pad = SUBLANES
    zeros_pad = jnp.zeros((pad, w), F32)
    upad[0:pad, :] = zeros_pad
    upad[pad + l:2 * pad + l, :] = zeros_pad
    for r0 in range(0, l, ROW_TILE):
        upad[pad + r0:pad + r0 + ROW_TILE, :] = u_ref[r0:r0 + ROW_TILE, :]

    neg_lam = -lam_ref[...]
    softplus = jnp.maximum(neg_lam, 0.0) + jnp.log1p(jnp.exp(-jnp.abs(neg_lam)))
    cw = cw_ref[...]
    cb = cb_ref[...]

    for r0 in range(0, l, ROW_TILE):
        row = r0 + lax.broadcasted_iota(jnp.int32, (ROW_TILE, w), 0)
        is_lat = row >= n_ctx
        near_boundary = r0 - CONV_WIDTH < n_ctx < r0 + ROW_TILE + CONV_WIDTH
        uc = jnp.zeros((ROW_TILE, w), F32) + cb
        for j in range(CONV_WIDTH):
            off = j - CONV_LEFT
            tap = upad[pad + r0 + off:pad + r0 + off + ROW_TILE, :]
            if off != 0 and near_boundary:
                tap = jnp.where(((row + off) >= n_ctx) == is_lat, tap, 0.0)
            uc = uc + tap * cw[j:j + 1, :]
        z = jnp.dot(uc.astype(BF16), wg_ref[...], preferred_element_type=F32) + bg_ref[...]
        for d, (a_ref, b_ref) in enumerate(((a_f, b_f), (a_r, b_r))):
            base = 2 * d * w
            r_gate = _sigmoid(z[:, base:base + w])
            i_gate = _sigmoid(z[:, base + w:base + 2 * w])
            log_a = (-LRU_C * r_gate) * softplus[d:d + 1, :]
            a = jnp.exp(log_a)
            one_minus_a2 = -jnp.tanh(log_a) * (a * a + 1.0)
            bb = jnp.sqrt(one_minus_a2) * (i_gate * uc)
            a_ref[r0:r0 + ROW_TILE, :] = a
            b_ref[r0:r0 + ROW_TILE, :] = bb

    sub = lax.broadcasted_iota(jnp.int32, (SUBLANES, w), 0)

    def tile_scan(a_ref, b_ref, blk, carry, reverse):
        rows = pl.ds(pl.multiple_of(blk * SUBLANES, SUBLANES), SUBLANES)
        a = a_ref[rows, :]
        b = b_ref[rows, :]
        s = 1
        while s < SUBLANES:
            keep = (sub < SUBLANES - s) if reverse else (sub >= s)
            shift = SUBLANES - s if reverse else s
            b = b + a * jnp.where(keep, pltpu.roll(b, shift, 0), 0.0)
            a = a * jnp.where(keep, pltpu.roll(a, shift, 0), 1.0)
            s *= 2
        h = b + a * carry
        b_ref[rows, :] = h
        last = h[0:1, :] if reverse else h[SUBLANES - 1:SUBLANES, :]
        return jnp.broadcast_to(last, h.shape)

    n_blk = l // SUBLANES
    c_blk = n_ctx // SUBLANES

    def ctx_body(i, carry):
        cf, cr = carry
        return tile_scan(a_f, b_f, i, cf, False), tile_scan(a_r, b_r, c_blk - 1 - i, cr, True)

    def lat_body(i, carry):
        cf, cr = carry
        return tile_scan(a_f, b_f, i, cf, False), tile_scan(a_r, b_r, n_blk + c_blk - 1 - i, cr, True)

    zero = jnp.zeros((SUBLANES, w), F32)
    carry = lax.fori_loop(0, c_blk, ctx_body, (zero, zero), unroll=2)
    lax.fori_loop(c_blk, n_blk, lat_body, carry, unroll=2)

    gain = og_ref[...]
    for r0 in range(0, l, ROW_TILE):
        rows = slice(r0, r0 + ROW_TILE)
        h = b_f[rows, :] + b_r[rows, :]
        o_ref[rows, :] = (_rms(h * gg_ref[rows, :]) * gain).astype(BF16)


def _lru(u, gg, conv_w, conv_b, w_gate, b_gate, lam, out_g, n_ctx):
    b, l, w = u.shape
    const = lambda shape: pl.BlockSpec(shape, lambda i: (0,) * len(shape))
    seq = pl.BlockSpec((None, l, w), lambda i: (i, 0, 0))
    return pl.pallas_call(
        functools.partial(_lru_kernel, n_ctx=n_ctx),
        grid=(b,),
        in_specs=[seq, seq, const(conv_w.shape), const((1, w)), const(w_gate.shape),
                  const(b_gate.shape), const(lam.shape), const((1, w))],
        out_specs=seq,
        out_shape=jax.ShapeDtypeStruct((b, l, w), BF16),
        scratch_shapes=[pltpu.VMEM((l + 2 * SUBLANES, w), F32)] + [pltpu.VMEM((l, w), F32)] * 4,
        compiler_params=_params(("parallel",)),
        name="rg_lru",
    )(u, gg, conv_w, conv_b, w_gate, b_gate, lam, out_g)


def _route(scores, biased):
    group_score = []
    for g in range(N_GROUPS):
        v = biased[g * GROUP_SIZE:(g + 1) * GROUP_SIZE]
        best = None
        for i in range(GROUP_SIZE):
            for j in range(i + 1, GROUP_SIZE):
                pair = v[i] + v[j]
                best = pair if best is None else jnp.maximum(best, pair)
        group_score.append(best)
    gid = jnp.zeros_like(group_score[0], dtype=jnp.int32)
    gbest = group_score[0]
    for g in range(1, N_GROUPS):
        upd = group_score[g] > gbest
        gid = jnp.where(upd, g, gid)
        gbest = jnp.where(upd, group_score[g], gbest)

    def pick(rows, j):
        out = rows[j]
        for g in range(1, N_GROUPS):
            out = jnp.where(gid == g, rows[g * GROUP_SIZE + j], out)
        return out

    v = [pick(biased, j) for j in range(GROUP_SIZE)]
    s = [pick(scores, j) for j in range(GROUP_SIZE)]
    i1 = jnp.zeros_like(gid)
    m1 = v[0]
    for j in range(1, GROUP_SIZE):
        upd = v[j] > m1
        i1 = jnp.where(upd, j, i1)
        m1 = jnp.where(upd, v[j], m1)
    i2 = jnp.zeros_like(gid)
    m2 = jnp.full_like(m1, -jnp.inf)
    for j in range(GROUP_SIZE):
        upd = (i1 != j) & (v[j] > m2)
        i2 = jnp.where(upd, j, i2)
        m2 = jnp.where(upd, v[j], m2)
    w1 = s[0]
    w2 = s[0]
    for j in range(1, GROUP_SIZE):
        w1 = jnp.where(i1 == j, s[j], w1)
        w2 = jnp.where(i2 == j, s[j], w2)
    den = w1 + w2
    w1 = w1 / den
    w2 = w2 / den
    lo = jnp.minimum(i1, i2)
    hi = jnp.maximum(i1, i2)
    pair = hi - lo - 1
    for k in range(1, GROUP_SIZE - 1):
        pair = pair + jnp.where(lo >= k, GROUP_SIZE - k, 0)
    first_is_lo = i1 < i2
    return gid * N_PAIRS + pair, jnp.where(first_is_lo, w1, w2), jnp.where(first_is_lo, w2, w1)


def _outproj_kernel(lat_ref, ctx_ref, a_ref, r_ref, mod_ref, w_ref, n2_ref, rwh_ref, rwl_ref, rb_ref, tri_ref,
                    x1_ref, h2_ref, route_ref, counts_ref, cnt_ref, *, skip):
    rows = lat_ref.shape[0]
    x = jnp.where(pl.program_id(1) + skip == 0, ctx_ref[...], lat_ref[...])
    half = a_ref.shape[-1]
    mix = (jnp.dot(a_ref[...], w_ref[0:half, :], preferred_element_type=F32)
           + jnp.dot(r_ref[...], w_ref[half:, :], preferred_element_type=F32))
    x1 = x + mod_ref[2:3, :] * mix
    x1_ref[...] = x1
    h2 = _rms(x1) * n2_ref[...]
    h2 = h2 * (1.0 + mod_ref[4:5, :]) + mod_ref[3:4, :]
    h2_ref[...] = h2
    hi, lo = _split_bf16(h2)
    nt = (((1,), (1,)), ((), ()))
    logits = (lax.dot_general(rwh_ref[...], hi, nt, preferred_element_type=F32)
              + lax.dot_general(rwh_ref[...], lo, nt, preferred_element_type=F32)
              + lax.dot_general(rwl_ref[...], hi, nt, preferred_element_type=F32))
    scores = _sigmoid(logits)
    biased = scores + rb_ref[...]
    bucket, w_lo, w_hi = _route([scores[e:e + 1, :] for e in range(N_EXPERTS)],
                                [biased[e:e + 1, :] for e in range(N_EXPERTS)])

    @pl.when((pl.program_id(0) == 0) & (pl.program_id(1) == 0))
    def _():
        cnt_ref[...] = jnp.zeros_like(cnt_ref)

    sub = lax.broadcasted_iota(jnp.int32, (BUCKET_ROWS, rows), 0)
    onehot = jnp.where(sub == bucket, 1.0, 0.0)
    before = jnp.dot(onehot.astype(BF16), tri_ref[...], preferred_element_type=F32) + cnt_ref[...]
    rank = jnp.sum(onehot * before, axis=0, keepdims=True)
    total = cnt_ref[...] + jnp.sum(onehot, axis=1, keepdims=True)
    cnt_ref[...] = total
    counts_ref[...] = total

    route_ref[0:1, :] = bucket.astype(F32)
    route_ref[1:2, :] = rank
    route_ref[2:3, :] = w_lo
    route_ref[3:4, :] = w_hi
    route_ref[4:, :] = jnp.zeros((ROUTE_ROWS - 4, rows), F32)


def _outproj(x_lat, x_ctx, lat_off, attn_n, rec_n, mods, w_out, n2g, rw_hi, rw_lo, rb, tri, skip):
    b, l, _ = attn_n.shape
    d = x_lat.shape[-1]
    nt = l // ROW_TILE - skip
    half = attn_n.shape[-1]
    row = lambda w: pl.BlockSpec((None, ROW_TILE, w), lambda i, t: (i, t + skip, 0))
    out_row = pl.BlockSpec((None, ROW_TILE, d), lambda i, t: (i, t, 0))
    const = lambda shape: pl.BlockSpec(shape, lambda i, t: (0,) * len(shape))
    return pl.pallas_call(
        functools.partial(_outproj_kernel, skip=skip),
        grid=(b, nt),
        in_specs=[
            pl.BlockSpec((None, ROW_TILE, d), lambda i, t: (i, jnp.maximum(t + skip - lat_off, 0), 0)),
            pl.BlockSpec((None, ROW_TILE, d), lambda i, t: (i, 0, 0)),
            row(half), row(half),
            pl.BlockSpec((None, N_MOD, d), lambda i, t: (jnp.where(t + skip == 0, b, i), 0, 0)),
            const(w_out.shape), const((1, d)), const(rw_hi.shape), const(rw_lo.shape), const(rb.shape),
            const(tri.shape),
        ],
        out_specs=[out_row, out_row,
                   pl.BlockSpec((None, None, ROUTE_ROWS, ROW_TILE), lambda i, t: (i, t, 0, 0)),
                   const((BUCKET_ROWS, ROW_TILE))],
        out_shape=[
            jax.ShapeDtypeStruct((b, nt * ROW_TILE, d), F32),
            jax.ShapeDtypeStruct((b, nt * ROW_TILE, d), F32),
            jax.ShapeDtypeStruct((b, nt, ROUTE_ROWS, ROW_TILE), F32),
            jax.ShapeDtypeStruct((BUCKET_ROWS, ROW_TILE), F32),
        ],
        scratch_shapes=[pltpu.VMEM((BUCKET_ROWS, ROW_TILE), F32)],
        compiler_params=_params(("arbitrary", "arbitrary")),
        name="out_proj_router",
    )(x_lat, x_ctx, attn_n, rec_n, mods, w_out, n2g, rw_hi, rw_lo, rb, tri)


def _moe_plan(route, counts, n_tiles):
    tm = ROW_TILE
    bucket = route[:, :, 0, :].reshape(-1).astype(jnp.int32)
    rank = route[:, :, 1, :].reshape(-1).astype(jnp.int32)
    cnt = counts[:N_BUCKETS, 0].astype(jnp.int32)
    tiles_per = (cnt + tm - 1) // tm
    tile_end = jnp.cumsum(tiles_per)
    pos = (tile_end - tiles_per)[bucket] * tm + rank
    total = tile_end[-1]
    tile = jnp.arange(n_tiles, dtype=jnp.int32)
    last = jnp.minimum(tile, total - 1)
    tile_bucket = jnp.sum((tile_end[None, :] <= last[:, None]).astype(jnp.int32), axis=1)
    tile_bucket = jnp.minimum(tile_bucket, N_BUCKETS - 1)
    group, pair = tile_bucket // N_PAIRS, tile_bucket % N_PAIRS
    pairs = [(i, j) for i in range(GROUP_SIZE) for j in range(i + 1, GROUP_SIZE)]
    lo = jnp.array([p[0] for p in pairs], jnp.int32)[pair]
    hi = jnp.array([p[1] for p in pairs], jnp.int32)[pair]
    weights = route[:, :, 2:4, :].transpose(0, 1, 3, 2).reshape(-1, 2)
    w_aug = jnp.pad(weights, ((0, 0), (0, LANES - 2)))
    tile_ea = group * GROUP_SIZE + lo
    tile_eb = group * GROUP_SIZE + hi
    changed = lambda e: jnp.concatenate([jnp.ones((1,), jnp.int32), (e[1:] != e[:-1]).astype(jnp.int32)])
    return (pos, tile_ea, tile_eb, changed(tile_ea), changed(tile_eb), (tile < total).astype(jnp.int32), w_aug)


def _dispatch_kernel(pos_ref, h_ref, w_ref, xs_in_ref, xs_ref, stage, sem):
    del xs_in_ref
    rows, d = h_ref.shape
    i = pl.program_id(0) * pl.num_programs(1) + pl.program_id(1)
    n = pl.num_programs(0) * pl.num_programs(1)
    slot = i % 2

    def drain(s):
        pltpu.make_async_copy(stage.at[s], xs_ref.at[pl.ds(0, rows), :], sem.at[s]).wait()

    @pl.when(i >= 2)
    def _():
        drain(slot)

    stage[slot, :, 0:d] = h_ref[...]
    stage[slot, :, d:] = w_ref[...]
    base = i * rows

    def issue(r, c):
        p = pos_ref[base + r]
        pltpu.make_async_copy(stage.at[slot, pl.ds(r, 1), :], xs_ref.at[pl.ds(p, 1), :], sem.at[slot]).start()
        return c

    lax.fori_loop(0, rows, issue, 0, unroll=8)

    @pl.when(i == n - 1)
    def _():
        drain(1 - slot)
        drain(slot)


def _dispatch(pos, h2, w_aug, n_rows):
    b, l, d = h2.shape
    nt = l // ROW_TILE
    assert b * nt >= 2
    xs0 = jnp.zeros((n_rows, d + LANES), F32)
    return pl.pallas_call(
        _dispatch_kernel,
        grid_spec=pltpu.PrefetchScalarGridSpec(
            num_scalar_prefetch=1,
            grid=(b, nt),
            in_specs=[
                pl.BlockSpec((None, ROW_TILE, d), lambda i, t, pos: (i, t, 0)),
                pl.BlockSpec((ROW_TILE, LANES), lambda i, t, pos: (i * nt + t, 0)),
                pl.BlockSpec(memory_space=pl.ANY),
            ],
            out_specs=pl.BlockSpec(memory_space=pl.ANY),
            scratch_shapes=[pltpu.VMEM((2, ROW_TILE, d + LANES), F32), pltpu.SemaphoreType.DMA((2,))],
        ),
        out_shape=jax.ShapeDtypeStruct((n_rows, d + LANES), F32),
        input_output_aliases={3: 0},
        compiler_params=_params(("arbitrary", "arbitrary")),
        name="moe_dispatch",
    )(pos, h2, w_aug, xs0)


def _moe_pair_kernel(ea_ref, eb_ref, new_a_ref, new_b_ref, valid_ref, xs_ref,
                     gate_a, up_a, down_a, gate_b, up_b, down_b, y_ref, wgu_a, wd_a, wgu_b, wd_b):
    del ea_ref, eb_ref
    i = pl.program_id(0)
    d = y_ref.shape[1]
    d_exp = wd_a.shape[0]

    def load_expert(gate, up, down, wgu, wd):
        wgu[:, :d_exp] = gate[...].astype(BF16)
        wgu[:, d_exp:] = up[...].astype(BF16)
        wd[...] = down[...].astype(BF16)

    @pl.when(new_a_ref[i] != 0)
    def _():
        load_expert(gate_a, up_a, down_a, wgu_a, wd_a)

    @pl.when(new_b_ref[i] != 0)
    def _():
        load_expert(gate_b, up_b, down_b, wgu_b, wd_b)

    @pl.when(valid_ref[i] != 0)
    def _():
        xs = xs_ref[...]
        x = xs[:, :d].astype(BF16)

        def expert(wgu, wd):
            gu = jnp.dot(x, wgu[...], preferred_element_type=F32)
            gate = gu[:, :d_exp]
            hid = gate * _sigmoid(gate) * gu[:, d_exp:]
            return jnp.dot(hid.astype(BF16), wd[...], preferred_element_type=F32)

        y_ref[...] = xs[:, d:d + 1] * expert(wgu_a, wd_a) + xs[:, d + 1:d + 2] * expert(wgu_b, wd_b)

    @pl.when(valid_ref[i] == 0)
    def _():
        y_ref[...] = jnp.zeros_like(y_ref)


def _moe_pairs(xs, tile_ea, tile_eb, new_a, new_b, tile_valid, w_gate, w_up, w_down, layer):
    n_rows, da = xs.shape
    d = da - LANES
    d_exp = w_gate.shape[-1]
    tm = ROW_TILE
    w_in = lambda pick: pl.BlockSpec((None, None, d, d_exp), lambda i, ea, eb, *_: (layer, pick(ea, eb)[i], 0, 0))
    w_out = lambda pick: pl.BlockSpec((None, None, d_exp, d), lambda i, ea, eb, *_: (layer, pick(ea, eb)[i], 0, 0))
    first = lambda ea, eb: ea
    second = lambda ea, eb: eb
    return pl.pallas_call(
        _moe_pair_kernel,
        grid_spec=pltpu.PrefetchScalarGridSpec(
            num_scalar_prefetch=5,
            grid=(n_rows // tm,),
            in_specs=[
                pl.BlockSpec((tm, da), lambda i, *_: (i, 0)),
                w_in(first), w_in(first), w_out(first), w_in(second), w_in(second), w_out(second),
            ],
            out_specs=pl.BlockSpec((tm, d), lambda i, *_: (i, 0)),
            scratch_shapes=[pltpu.VMEM((d, 2 * d_exp), BF16), pltpu.VMEM((d_exp, d), BF16)] * 2,
        ),
        out_shape=jax.ShapeDtypeStruct((n_rows, d), F32),
        compiler_params=_params(("arbitrary",)),
        name="moe_pairs",
    )(tile_ea, tile_eb, new_a, new_b, tile_valid, xs, w_gate, w_up, w_down, w_gate, w_up, w_down)


def _combine_kernel(pos_ref, x_ref, mod_ref, g_ref, ys_ref, o_ref, ybuf, sem, *, final):
    rows = x_ref.shape[0]
    i = pl.program_id(0) * pl.num_programs(1) + pl.program_id(1)
    n = pl.num_programs(0) * pl.num_programs(1)
    slot = i % 2

    def gather(step, s):
        base = step * rows

        def issue(r, c):
            p = pos_ref[base + r]
            pltpu.make_async_copy(ys_ref.at[pl.ds(p, 1), :], ybuf.at[s, pl.ds(r, 1), :], sem.at[s]).start()
            return c

        lax.fori_loop(0, rows, issue, 0, unroll=8)

    @pl.when(i == 0)
    def _():
        gather(0, 0)

    @pl.when(i + 1 < n)
    def _():
        gather(i + 1, 1 - slot)

    pltpu.make_async_copy(ys_ref.at[pl.ds(0, rows), :], ybuf.at[slot], sem.at[slot]).wait()
    x2 = x_ref[...] + mod_ref[5:6, :] * ybuf[slot]
    o_ref[...] = _rms(x2) * g_ref[...] if final else x2


def _combine(pos, x1, mods, gain, ys, has_ctx, final):
    b, l, d = x1.shape
    nt = l // ROW_TILE
    ctx_row = lambda i, t: jnp.where(t == 0, b, i) if has_ctx else i
    return pl.pallas_call(
        functools.partial(_combine_kernel, final=final),
        grid_spec=pltpu.PrefetchScalarGridSpec(
            num_scalar_prefetch=1,
            grid=(b, nt),
            in_specs=[
                pl.BlockSpec((None, ROW_TILE, d), lambda i, t, pos: (i, t, 0)),
                pl.BlockSpec((None, N_MOD, d), lambda i, t, pos: (ctx_row(i, t), 0, 0)),
                pl.BlockSpec((1, d), lambda i, t, pos: (0, 0)),
                pl.BlockSpec(memory_space=pl.ANY),
            ],
            out_specs=pl.BlockSpec((None, ROW_TILE, d), lambda i, t, pos: (i, t, 0)),
            scratch_shapes=[pltpu.VMEM((2, ROW_TILE, d), F32), pltpu.SemaphoreType.DMA((2,))],
        ),
        out_shape=jax.ShapeDtypeStruct((b, l, d), F32),
        compiler_params=_params(("arbitrary", "arbitrary")),
        name="moe_combine",
    )(pos, x1, mods, gain, ys)


def _rope_tables(n_ctx, n_lat):
    rows = n_lat // GRID_W
    r, col = jnp.meshgrid(jnp.arange(rows), jnp.arange(GRID_W), indexing="ij")
    r = r.reshape(-1).astype(F32)
    col = col.reshape(-1).astype(F32)
    half = HEAD_DIM // 2
    inv = ROPE_THETA ** (-jnp.arange(0, half, 2, dtype=F32) / half)
    ang_r = r[:, None] * inv
    ang_c = col[:, None] * inv
    cos_h = jnp.concatenate([jnp.cos(ang_r)] * 2 + [jnp.cos(ang_c)] * 2, axis=-1)
    sin_h = jnp.concatenate([-jnp.sin(ang_r), jnp.sin(ang_r), -jnp.sin(ang_c), jnp.sin(ang_c)], axis=-1)
    cos_t = jnp.concatenate([jnp.ones((n_ctx, HEAD_DIM), F32), cos_h], axis=0)
    sin_t = jnp.concatenate([jnp.zeros((n_ctx, HEAD_DIM), F32), sin_h], axis=0)
    reps = LANES // HEAD_DIM
    return jnp.tile(cos_t, (1, reps)), jnp.tile(sin_t, (1, reps))


def _block_diag(w):
    n, d, e = w.shape
    eye = jnp.eye(n, dtype=w.dtype)
    return (w[:, :, None, :] * eye[:, None, :, None]).reshape(n * d, n * e)


def kernel(x, c, ctx, c_ctx, ada_w, ada_b, norm1_g, w_in, q_norm_g, k_norm_g, conv_w, conv_b, lru_wa, lru_ba, lru_wx, lru_bx, lru_lambda, attn_out_g, lru_out_g, w_out, norm2_g, router_w, router_b, exp_w_gate, exp_w_up, exp_w_down, final_g):
    b, s, d = x.shape
    n_ctx = ctx.shape[1]
    l = n_ctx + s
    depth = ada_w.shape[0]
    nt = l // ROW_TILE
    assert n_ctx == ROW_TILE and s % ROW_TILE == 0

    x_lat, x_ctx, lat_off = x, ctx, 1
    cvec = jnp.zeros((2 * SUBLANES, d), F32).at[:b].set(c).at[b].set(c_ctx)
    mods_all = _ada_mods(cvec, ada_w, ada_b)[:, :b + 1].reshape(depth, b + 1, N_MOD, d)
    cos_t, sin_t = _rope_tables(n_ctx, s)

    head_avg = _block_diag(jnp.full((N_Q_HEADS, HEAD_DIM, HEAD_DIM), 1.0 / HEAD_DIM, F32)).astype(BF16)
    rw_t = router_w.T
    rw_hi = rw_t.astype(BF16)
    rw_lo = (rw_t - rw_hi.astype(F32)).astype(BF16)
    rb = jnp.broadcast_to(router_b[:, None], (N_EXPERTS, ROW_TILE)).astype(F32)
    tri = jnp.triu(jnp.ones((ROW_TILE, ROW_TILE), BF16), 1)

    qw, kw = ATTN_WIDTH, KV_WIDTH
    out = None
    for li in range(depth):
        wi = w_in[li]
        k0, k1 = wi[:, qw:qw + HEAD_DIM], wi[:, qw + HEAD_DIM:qw + kw]
        v0, v1 = wi[:, qw + kw:qw + kw + HEAD_DIM], wi[:, qw + kw + HEAD_DIM:qw + 2 * kw]
        w_ext = jnp.concatenate([wi[:, :qw], k0, k0, k1, k1, v0, v0, v1, v1, wi[:, qw + 2 * kw:]],
                                axis=1).astype(BF16)
        gq = jnp.tile(q_norm_g[li], N_Q_HEADS)[None, :]
        gk = jnp.tile(k_norm_g[li], 2 * N_KV_HEADS)[None, :]
        mods = mods_all[li]
        q, kd, vd, u, gg = _inproj(x_lat, x_ctx, lat_off, mods, norm1_g[li][None, :], w_ext, cos_t, sin_t,
                                   gq, gk, head_avg)
        attn_n = _attention(q, kd, vd, attn_out_g[li][None, :], n_ctx)

        w_gate = jnp.concatenate([_block_diag(lru_wa[li, 0]), _block_diag(lru_wx[li, 0]),
                                  _block_diag(lru_wa[li, 1]), _block_diag(lru_wx[li, 1])], axis=1).astype(BF16)
        b_gate = jnp.concatenate([lru_ba[li, 0], lru_bx[li, 0], lru_ba[li, 1], lru_bx[li, 1]])[None, :]
        rec_n = _lru(u, gg, conv_w[li], conv_b[li][None, :], w_gate, b_gate, lru_lambda[li],
                     lru_out_g[li][None, :], n_ctx)

        skip = n_ctx // ROW_TILE if li == depth - 1 else 0
        x1, h2, route, counts = _outproj(x_lat, x_ctx, lat_off, attn_n, rec_n, mods, w_out[li].astype(BF16),
                                         norm2_g[li][None, :], rw_hi, rw_lo, rb, tri, skip)
        n_tiles = b * (nt - skip) + N_BUCKETS
        pos, tile_ea, tile_eb, new_a, new_b, tile_valid, w_aug = _moe_plan(route, counts, n_tiles)
        xs = _dispatch(pos, h2, w_aug, n_tiles * ROW_TILE)
        ys = _moe_pairs(xs, tile_ea, tile_eb, new_a, new_b, tile_valid, exp_w_gate, exp_w_up, exp_w_down, li)
        if li == depth - 1:
            out = _combine(pos, x1, mods, final_g[None, :], ys, skip == 0, True)
        else:
            x_lat = x_ctx = _combine(pos, x1, mods, final_g[None, :], ys, skip == 0, False)
            lat_off = 0
    return out
```

```python
import functools

import jax
import jax.numpy as jnp
from jax import lax
from jax.experimental import pallas as pl
from jax.experimental.pallas import tpu as pltpu

F32 = jnp.float32
BF16 = jnp.bfloat16

HEAD_DIM = 64
N_Q_HEADS = 8
N_KV_HEADS = 2
ATTN_WIDTH = N_Q_HEADS * HEAD_DIM
KV_WIDTH = N_KV_HEADS * HEAD_DIM
LRU_WIDTH = 512
LRU_BLOCKS = 8
CONV_WIDTH = 4
CONV_LEFT = CONV_WIDTH // 2
LRU_C = 8.0
GRID_W = 64
ROPE_THETA = 10000.0
N_EXPERTS = 16
N_GROUPS = 4
GROUP_SIZE = N_EXPERTS // N_GROUPS
N_MOD = 6
NORM_EPS = 1e-6
ATTN_SCALE = HEAD_DIM ** -0.5
LOG2_E = 1.4426950408889634

LANES = 128
SUBLANES = 8
ROW_TILE = 256
N_PAIRS = GROUP_SIZE * (GROUP_SIZE - 1) // 2
N_BUCKETS = N_GROUPS * N_PAIRS
BUCKET_ROWS = 32
ROUTE_ROWS = 8
VMEM_LIMIT = 60000 * 1024


def _params(sem):
    return pltpu.CompilerParams(dimension_semantics=sem, vmem_limit_bytes=VMEM_LIMIT)


def _sigmoid(x):
    return 1.0 / (1.0 + jnp.exp(-x))


def _rms(x):
    return x * lax.rsqrt(jnp.mean(x * x, axis=-1, keepdims=True) + NORM_EPS)


def _split_bf16(x):
    hi = x.astype(BF16)
    lo = (x - hi.astype(F32)).astype(BF16)
    return hi, lo


def _ada_kernel(c_ref, w_ref, b_ref, o_ref):
    cv = c_ref[...]
    s = cv * _sigmoid(cv)
    o_ref[...] = jnp.dot(s.astype(BF16), w_ref[...].astype(BF16),
                         preferred_element_type=F32) + b_ref[...]


def _ada_mods(cvec, ada_w, ada_b):
    depth, d, n = ada_w.shape
    tn = 512
    rows = cvec.shape[0]
    return pl.pallas_call(
        _ada_kernel,
        grid=(depth, n // tn),
        in_specs=[
            pl.BlockSpec((rows, d), lambda l, j: (0, 0)),
            pl.BlockSpec((None, d, tn), lambda l, j: (l, 0, j)),
            pl.BlockSpec((None, 1, tn), lambda l, j: (l, 0, j)),
        ],
        out_specs=pl.BlockSpec((None, rows, tn), lambda l, j: (l, 0, j)),
        out_shape=jax.ShapeDtypeStruct((depth, rows, n), F32),
        compiler_params=_params(("parallel", "parallel")),
        name="ada_mods",
    )(cvec, ada_w, ada_b.reshape(depth, 1, n))


def _head_rms(t, g_mat, gain):
    hi, lo = _split_bf16(t * t)
    m = (jnp.dot(hi, g_mat, preferred_element_type=F32)
         + jnp.dot(lo, g_mat, preferred_element_type=F32))
    return t * lax.rsqrt(m + NORM_EPS) * gain


def _rope(t, cos_t, sin_t):
    width = t.shape[-1]
    reps = width // LANES
    cos_w = jnp.concatenate([cos_t] * reps, axis=1)
    sin_w = jnp.concatenate([sin_t] * reps, axis=1)
    lane = lax.broadcasted_iota(jnp.int32, t.shape, 1)
    quarter = HEAD_DIM // 4
    first = (lane % (2 * quarter)) < quarter
    partner = jnp.where(first, pltpu.roll(t, width - quarter, 1), pltpu.roll(t, quarter, 1))
    return t * cos_w + partner * sin_w


def _inproj_kernel(lat_ref, ctx_ref, mod_ref, n1_ref, w_ref, cos_ref, sin_ref, gq_ref, gk_ref, gm_ref,
                   q_ref, k_ref, v_ref, u_ref, gg_ref):
    x = jnp.where(pl.program_id(1) == 0, ctx_ref[...], lat_ref[...])
    h = _rms(x) * n1_ref[...]
    h = h * (1.0 + mod_ref[1:2, :]) + mod_ref[0:1, :]
    y = jnp.dot(h.astype(BF16), w_ref[...], preferred_element_type=F32)
    kd = 2 * KV_WIDTH
    o_k, o_v, o_u, o_g = ATTN_WIDTH, ATTN_WIDTH + kd, ATTN_WIDTH + 2 * kd, ATTN_WIDTH + 2 * kd + LRU_WIDTH
    cos_t = cos_ref[...]
    sin_t = sin_ref[...]
    g_mat = gm_ref[...]
    q = _rope(_head_rms(y[:, :o_k], g_mat, gq_ref[...]), cos_t, sin_t)
    q_ref[...] = (q * (ATTN_SCALE * LOG2_E)).astype(BF16)
    k = _rope(_head_rms(y[:, o_k:o_v], g_mat[:kd, :kd], gk_ref[...]), cos_t, sin_t)
    k_ref[...] = k.astype(BF16)
    vt = y[:, o_v:o_u].T
    sub = lax.broadcasted_iota(jnp.int32, vt.shape, 0)
    v_ref[...] = jnp.where(sub % LANES < HEAD_DIM, vt, 1.0).astype(BF16)
    u_ref[...] = y[:, o_u:o_g]
    gb = y[:, o_g:]
    gg_ref[...] = 0.5 * gb * (1.0 + jnp.tanh(0.7978845608028654 * (gb + 0.044715 * gb * gb * gb)))


def _seq_specs(lat_off, d):
    lat = pl.BlockSpec((None, ROW_TILE, d), lambda i, t: (i, jnp.maximum(t - lat_off, 0), 0))
    ctx = pl.BlockSpec((None, ROW_TILE, d), lambda i, t: (i, 0, 0))
    return lat, ctx


def _inproj(x_lat, x_ctx, lat_off, mods, n1g, w_ext, cos_t, sin_t, gq, gk, g_mat):
    b, _, d = x_lat.shape
    l = x_lat.shape[1] + lat_off * x_ctx.shape[1]
    nt = l // ROW_TILE
    n_ext = w_ext.shape[1]
    kd = 2 * KV_WIDTH
    row = lambda w: pl.BlockSpec((None, ROW_TILE, w), lambda i, t: (i, t, 0))
    const = lambda shape: pl.BlockSpec(shape, lambda i, t: (0,) * len(shape))
    return pl.pallas_call(
        _inproj_kernel,
        grid=(b, nt),
        in_specs=[
            *_seq_specs(lat_off, d),
            pl.BlockSpec((None, N_MOD, d), lambda i, t: (jnp.where(t == 0, b, i), 0, 0)),
            const((1, d)),
            const((d, n_ext)),
            pl.BlockSpec((ROW_TILE, LANES), lambda i, t: (t, 0)),
            pl.BlockSpec((ROW_TILE, LANES), lambda i, t: (t, 0)),
            const((1, ATTN_WIDTH)),
            const((1, kd)),
            const((ATTN_WIDTH, ATTN_WIDTH)),
        ],
        out_specs=[row(ATTN_WIDTH), row(kd), pl.BlockSpec((None, kd, ROW_TILE), lambda i, t: (i, 0, t)),
                   row(LRU_WIDTH), row(LRU_WIDTH)],
        out_shape=[
            jax.ShapeDtypeStruct((b, l, ATTN_WIDTH), BF16),
            jax.ShapeDtypeStruct((b, l, kd), BF16),
            jax.ShapeDtypeStruct((b, kd, l), BF16),
            jax.ShapeDtypeStruct((b, l, LRU_WIDTH), F32),
            jax.ShapeDtypeStruct((b, l, LRU_WIDTH), F32),
        ],
        compiler_params=_params(("parallel", "parallel")),
        name="in_proj",
    )(x_lat, x_ctx, mods, n1g, w_ext, cos_t, sin_t, gq, gk, g_mat)


def _attn_kernel(q_ref, k_ref, v_ref, g_ref, o_ref, *, n_ctx):
    t = pl.program_id(1)
    rows = q_ref.shape[0]
    n_all = k_ref.shape[0]

    def run(n_keys):
        low = lax.broadcasted_iota(jnp.int32, (rows, LANES), 1) < HEAD_DIM
        n_pairs = N_Q_HEADS // 2
        kv_of = lambda pair: (2 * pair) // (N_Q_HEADS // N_KV_HEADS)

        def scores(pair):
            qp = q_ref[:, pair * LANES:(pair + 1) * LANES]
            zero = jnp.zeros_like(qp)
            q2 = jnp.concatenate([jnp.where(low, qp, zero), jnp.where(low, zero, qp)], axis=0)
            kd = k_ref[0:n_keys, kv_of(pair) * LANES:(kv_of(pair) + 1) * LANES]
            return lax.dot_general(kd, q2, (((1,), (1,)), ((), ())), preferred_element_type=F32)

        def softmax_pv(pair, st):
            vt = v_ref[kv_of(pair) * LANES:(kv_of(pair) + 1) * LANES, 0:n_keys]
            p = jnp.exp2(st - jnp.max(st, axis=0, keepdims=True))
            ot = jnp.dot(vt, p.astype(BF16), preferred_element_type=F32)
            on = ot[:HEAD_DIM, :] * (1.0 / ot[HEAD_DIM:HEAD_DIM + 1, :])
            return jnp.concatenate([on[:, :rows], on[:, rows:]], axis=0).T

        outs = []
        st = scores(0)
        for pair in range(n_pairs):
            st_next = scores(pair + 1) if pair + 1 < n_pairs else None
            outs.append(softmax_pv(pair, st))
            st = st_next
        a = jnp.concatenate(outs, axis=1)
        o_ref[...] = (_rms(a) * g_ref[...]).astype(BF16)

    @pl.when(t == 0)
    def _():
        run(n_ctx)

    @pl.when(t > 0)
    def _():
        run(n_all)


def _attention(q, kd, vd, gain, n_ctx):
    b, l, w = q.shape
    nt = l // ROW_TILE
    kw = kd.shape[-1]
    return pl.pallas_call(
        functools.partial(_attn_kernel, n_ctx=n_ctx),
        grid=(b, nt),
        in_specs=[
            pl.BlockSpec((None, ROW_TILE, w), lambda i, t: (i, t, 0)),
            pl.BlockSpec((None, l, kw), lambda i, t: (i, 0, 0)),
            pl.BlockSpec((None, kw, l), lambda i, t: (i, 0, 0)),
            pl.BlockSpec((1, w), lambda i, t: (0, 0)),
        ],
        out_specs=pl.BlockSpec((None, ROW_TILE, w), lambda i, t: (i, t, 0)),
        out_shape=jax.ShapeDtypeStruct((b, l, w), BF16),
        compiler_params=_params(("parallel", "parallel")),
        name="attention",
    )(q, kd, vd, gain)


def _lru_kernel(u_ref, gg_ref, cw_ref, cb_ref, wg_ref, bg_ref, lam_ref, og_ref, o_ref,
                upad, a_f, b_f, a_r, b_r, *, n_ctx):
    l, w = u_ref.shape
    pad = SUBLANES
    zeros_pad = jnp.zeros((pad, w), F32)
    upad[0:pad, :] = zeros_pad
    upad[pad + l:2 * pad + l, :] = zeros_pad
    for r0 in range(0, l, ROW_TILE):
        upad[pad + r0:pad + r0 + ROW_TILE, :] = u_ref[r0:r0 + ROW_TILE, :]

    neg_lam = -lam_ref[...]
    softplus = jnp.maximum(neg_lam, 0.0) + jnp.log1p(jnp.exp(-jnp.abs(neg_lam)))
    cw = cw_ref[...]
    cb = cb_ref[...]

    for r0 in range(0, l, ROW_TILE):
        row = r0 + lax.broadcasted_iota(jnp.int32, (ROW_TILE, w), 0)
        is_lat = row >= n_ctx
        near_boundary = r0 - CONV_WIDTH < n_ctx < r0 + ROW_TILE + CONV_WIDTH
        uc = jnp.zeros((ROW_TILE, w), F32) + cb
        for j in range(CONV_WIDTH):
            off = j - CONV_LEFT
            tap = upad[pad + r0 + off:pad + r0 + off + ROW_TILE, :]
            if off != 0 and near_boundary:
                tap = jnp.where(((row + off) >= n_ctx) == is_lat, tap, 0.0)
            uc = uc + tap * cw[j:j + 1, :]
        z = jnp.dot(uc.astype(BF16), wg_ref[...], preferred_element_type=F32) + bg_ref[...]
        for d, (a_ref, b_ref) in enumerate(((a_f, b_f), (a_r, b_r))):
            base = 2 * d * w
            r_gate = _sigmoid(z[:, base:base + w])
            i_gate = _sigmoid(z[:, base + w:base + 2 * w])
            log_a = (-LRU_C * r_gate) * softplus[d:d + 1, :]
            a = jnp.exp(log_a)
            one_minus_a2 = -jnp.tanh(log_a) * (a * a + 1.0)
            bb = jnp.sqrt(one_minus_a2) * (i_gate * uc)
            a_ref[r0:r0 + ROW_TILE, :] = a
            b_ref[r0:r0 + ROW_TILE, :] = bb

    sub = lax.broadcasted_iota(jnp.int32, (SUBLANES, w), 0)

    def tile_scan(a_ref, b_ref, blk, carry, reverse):
        rows = pl.ds(pl.multiple_of(blk * SUBLANES, SUBLANES), SUBLANES)
        a = a_ref[rows, :]
        b = b_ref[rows, :]
        s = 1
        while s < SUBLANES:
            keep = (sub < SUBLANES - s) if reverse else (sub >= s)
            shift = SUBLANES - s if reverse else s
            b = b + a * jnp.where(keep, pltpu.roll(b, shift, 0), 0.0)
            a = a * jnp.where(keep, pltpu.roll(a, shift, 0), 1.0)
            s *= 2
        h = b + a * carry
        b_ref[rows, :] = h
        last = h[0:1, :] if reverse else h[SUBLANES - 1:SUBLANES, :]
        return jnp.broadcast_to(last, h.shape)

    n_blk = l // SUBLANES
    c_blk = n_ctx // SUBLANES

    def ctx_body(i, carry):
        cf, cr = carry
        return tile_scan(a_f, b_f, i, cf, False), tile_scan(a_r, b_r, c_blk - 1 - i, cr, True)

    def lat_body(i, carry):
        cf, cr = carry
        return tile_scan(a_f, b_f, i, cf, False), tile_scan(a_r, b_r, n_blk + c_blk - 1 - i, cr, True)

    zero = jnp.zeros((SUBLANES, w), F32)
    carry = lax.fori_loop(0, c_blk, ctx_body, (zero, zero), unroll=2)
    lax.fori_loop(c_blk, n_blk, lat_body, carry, unroll=2)

    gain = og_ref[...]
    for r0 in range(0, l, ROW_TILE):
        rows = slice(r0, r0 + ROW_TILE)
        h = b_f[rows, :] + b_r[rows, :]
        o_ref[rows, :] = (_rms(h * gg_ref[rows, :]) * gain).astype(BF16)


def _lru(u, gg, conv_w, conv_b, w_gate, b_gate, lam, out_g, n_ctx):
    b, l, w = u.shape
    const = lambda shape: pl.BlockSpec(shape, lambda i: (0,) * len(shape))
    seq = pl.BlockSpec((None, l, w), lambda i: (i, 0, 0))
    return pl.pallas_call(
        functools.partial(_lru_kernel, n_ctx=n_ctx),
        grid=(b,),
        in_specs=[seq, seq, const(conv_w.shape), const((1, w)), const(w_gate.shape),
                  const(b_gate.shape), const(lam.shape), const((1, w))],
        out_specs=seq,
        out_shape=jax.ShapeDtypeStruct((b, l, w), BF16),
        scratch_shapes=[pltpu.VMEM((l + 2 * SUBLANES, w), F32)] + [pltpu.VMEM((l, w), F32)] * 4,
        compiler_params=_params(("parallel",)),
        name="rg_lru",
    )(u, gg, conv_w, conv_b, w_gate, b_gate, lam, out_g)


def _route(scores, biased):
    group_score = []
    for g in range(N_GROUPS):
        v = biased[g * GROUP_SIZE:(g + 1) * GROUP_SIZE]
        best = None
        for i in range(GROUP_SIZE):
            for j in range(i + 1, GROUP_SIZE):
                pair = v[i] + v[j]
                best = pair if best is None else jnp.maximum(best, pair)
        group_score.append(best)
    gid = jnp.zeros_like(group_score[0], dtype=jnp.int32)
    gbest = group_score[0]
    for g in range(1, N_GROUPS):
        upd = group_score[g] > gbest
        gid = jnp.where(upd, g, gid)
        gbest = jnp.where(upd, group_score[g], gbest)

    def pick(rows, j):
        out = rows[j]
        for g in range(1, N_GROUPS):
            out = jnp.where(gid == g, rows[g * GROUP_SIZE + j], out)
        return out

    v = [pick(biased, j) for j in range(GROUP_SIZE)]
    s = [pick(scores, j) for j in range(GROUP_SIZE)]
    i1 = jnp.zeros_like(gid)
    m1 = v[0]
    for j in range(1, GROUP_SIZE):
        upd = v[j] > m1
        i1 = jnp.where(upd, j, i1)
        m1 = jnp.where(upd, v[j], m1)
    i2 = jnp.zeros_like(gid)
    m2 = jnp.full_like(m1, -jnp.inf)
    for j in range(GROUP_SIZE):
        upd = (i1 != j) & (v[j] > m2)
        i2 = jnp.where(upd, j, i2)
        m2 = jnp.where(upd, v[j], m2)
    w1 = s[0]
    w2 = s[0]
    for j in range(1, GROUP_SIZE):
        w1 = jnp.where(i1 == j, s[j], w1)
        w2 = jnp.where(i2 == j, s[j], w2)
    den = w1 + w2
    w1 = w1 / den
    w2 = w2 / den
    lo = jnp.minimum(i1, i2)
    hi = jnp.maximum(i1, i2)
    pair = hi - lo - 1
    for k in range(1, GROUP_SIZE - 1):
        pair = pair + jnp.where(lo >= k, GROUP_SIZE - k, 0)
    first_is_lo = i1 < i2
    return gid * N_PAIRS + pair, jnp.where(first_is_lo, w1, w2), jnp.where(first_is_lo, w2, w1)


def _outproj_kernel(lat_ref, ctx_ref, a_ref, r_ref, mod_ref, w_ref, n2_ref, rwh_ref, rwl_ref, rb_ref, tri_ref,
                    x1_ref, h2_ref, route_ref, counts_ref, cnt_ref, *, skip):
    rows = lat_ref.shape[0]
    x = jnp.where(pl.program_id(1) + skip == 0, ctx_ref[...], lat_ref[...])
    half = a_ref.shape[-1]
    mix = (jnp.dot(a_ref[...], w_ref[0:half, :], preferred_element_type=F32)
           + jnp.dot(r_ref[...], w_ref[half:, :], preferred_element_type=F32))
    x1 = x + mod_ref[2:3, :] * mix
    x1_ref[...] = x1
    h2 = _rms(x1) * n2_ref[...]
    h2 = h2 * (1.0 + mod_ref[4:5, :]) + mod_ref[3:4, :]
    h2_ref[...] = h2
    hi, lo = _split_bf16(h2)
    nt = (((1,), (1,)), ((), ()))
    logits = (lax.dot_general(rwh_ref[...], hi, nt, preferred_element_type=F32)
              + lax.dot_general(rwh_ref[...], lo, nt, preferred_element_type=F32)
              + lax.dot_general(rwl_ref[...], hi, nt, preferred_element_type=F32))
    scores = _sigmoid(logits)
    biased = scores + rb_ref[...]
    bucket, w_lo, w_hi = _route([scores[e:e + 1, :] for e in range(N_EXPERTS)],
                                [biased[e:e + 1, :] for e in range(N_EXPERTS)])

    @pl.when((pl.program_id(0) == 0) & (pl.program_id(1) == 0))
    def _():
        cnt_ref[...] = jnp.zeros_like(cnt_ref)

    sub = lax.broadcasted_iota(jnp.int32, (BUCKET_ROWS, rows), 0)
    onehot = jnp.where(sub == bucket, 1.0, 0.0)
    before = jnp.dot(onehot.astype(BF16), tri_ref[...], preferred_element_type=F32) + cnt_ref[...]
    rank = jnp.sum(onehot * before, axis=0, keepdims=True)
    total = cnt_ref[...] + jnp.sum(onehot, axis=1, keepdims=True)
    cnt_ref[...] = total
    counts_ref[...] = total

    route_ref[0:1, :] = bucket.astype(F32)
    route_ref[1:2, :] = rank
    route_ref[2:3, :] = w_lo
    route_ref[3:4, :] = w_hi
    route_ref[4:, :] = jnp.zeros((ROUTE_ROWS - 4, rows), F32)


def _outproj(x_lat, x_ctx, lat_off, attn_n, rec_n, mods, w_out, n2g, rw_hi, rw_lo, rb, tri, skip):
    b, l, _ = attn_n.shape
    d = x_lat.shape[-1]
    nt = l // ROW_TILE - skip
    half = attn_n.shape[-1]
    row = lambda w: pl.BlockSpec((None, ROW_TILE, w), lambda i, t: (i, t + skip, 0))
    out_row = pl.BlockSpec((None, ROW_TILE, d), lambda i, t: (i, t, 0))
    const = lambda shape: pl.BlockSpec(shape, lambda i, t: (0,) * len(shape))
    return pl.pallas_call(
        functools.partial(_outproj_kernel, skip=skip),
        grid=(b, nt),
        in_specs=[
            pl.BlockSpec((None, ROW_TILE, d), lambda i, t: (i, jnp.maximum(t + skip - lat_off, 0), 0)),
            pl.BlockSpec((None, ROW_TILE, d), lambda i, t: (i, 0, 0)),
            row(half), row(half),
            pl.BlockSpec((None, N_MOD, d), lambda i, t: (jnp.where(t + skip == 0, b, i), 0, 0)),
            const(w_out.shape), const((1, d)), const(rw_hi.shape), const(rw_lo.shape), const(rb.shape),
            const(tri.shape),
        ],
        out_specs=[out_row, out_row,
                   pl.BlockSpec((None, None, ROUTE_ROWS, ROW_TILE), lambda i, t: (i, t, 0, 0)),
                   const((BUCKET_ROWS, ROW_TILE))],
        out_shape=[
            jax.ShapeDtypeStruct((b, nt * ROW_TILE, d), F32),
            jax.ShapeDtypeStruct((b, nt * ROW_TILE, d), F32),
            jax.ShapeDtypeStruct((b, nt, ROUTE_ROWS, ROW_TILE), F32),
            jax.ShapeDtypeStruct((BUCKET_ROWS, ROW_TILE), F32),
        ],
        scratch_shapes=[pltpu.VMEM((BUCKET_ROWS, ROW_TILE), F32)],
        compiler_params=_params(("arbitrary", "arbitrary")),
        name="out_proj_router",
    )(x_lat, x_ctx, attn_n, rec_n, mods, w_out, n2g, rw_hi, rw_lo, rb, tri)


def _moe_plan(route, counts, n_tiles):
    tm = ROW_TILE
    bucket = route[:, :, 0, :].reshape(-1).astype(jnp.int32)
    rank = route[:, :, 1, :].reshape(-1).astype(jnp.int32)
    cnt = counts[:N_BUCKETS, 0].astype(jnp.int32)
    tiles_per = (cnt + tm - 1) // tm
    tile_end = jnp.cumsum(tiles_per)
    pos = (tile_end - tiles_per)[bucket] * tm + rank
    total = tile_end[-1]
    tile = jnp.arange(n_tiles, dtype=jnp.int32)
    last = jnp.minimum(tile, total - 1)
    tile_bucket = jnp.sum((tile_end[None, :] <= last[:, None]).astype(jnp.int32), axis=1)
    tile_bucket = jnp.minimum(tile_bucket, N_BUCKETS - 1)
    group, pair = tile_bucket // N_PAIRS, tile_bucket % N_PAIRS
    pairs = [(i, j) for i in range(GROUP_SIZE) for j in range(i + 1, GROUP_SIZE)]
    lo = jnp.array([p[0] for p in pairs], jnp.int32)[pair]
    hi = jnp.array([p[1] for p in pairs], jnp.int32)[pair]
    weights = route[:, :, 2:4, :].transpose(0, 1, 3, 2).reshape(-1, 2)
    w_aug = jnp.pad(weights, ((0, 0), (0, LANES - 2)))
    tile_ea = group * GROUP_SIZE + lo
    tile_eb = group * GROUP_SIZE + hi
    changed = lambda e: jnp.concatenate([jnp.ones((1,), jnp.int32), (e[1:] != e[:-1]).astype(jnp.int32)])
    return (pos, tile_ea, tile_eb, changed(tile_ea), changed(tile_eb), (tile < total).astype(jnp.int32), w_aug)


def _dispatch_kernel(pos_ref, h_ref, w_ref, xs_in_ref, xs_ref, stage, sem):
    del xs_in_ref
    rows, d = h_ref.shape
    i = pl.program_id(0) * pl.num_programs(1) + pl.program_id(1)
    n = pl.num_programs(0) * pl.num_programs(1)
    slot = i % 2

    def drain(s):
        pltpu.make_async_copy(stage.at[s], xs_ref.at[pl.ds(0, rows), :], sem.at[s]).wait()

    @pl.when(i >= 2)
    def _():
        drain(slot)

    stage[slot, :, 0:d] = h_ref[...]
    stage[slot, :, d:] = w_ref[...]
    base = i * rows

    def issue(g, c):
        r0 = pl.multiple_of(g * SUBLANES, SUBLANES)
        for j in range(SUBLANES):
            p = pos_ref[base + r0 + j]
            pltpu.make_async_copy(stage.at[slot, pl.ds(r0 + j, 1), :], xs_ref.at[pl.ds(p, 1), :],
                                  sem.at[slot]).start()
        return c

    lax.fori_loop(0, rows // SUBLANES, issue, 0)

    @pl.when(i == n - 1)
    def _():
        drain(1 - slot)
        drain(slot)


def _dispatch(pos, h2, w_aug, xs0):
    b, l, d = h2.shape
    nt = l // ROW_TILE
    assert b * nt >= 2
    n_rows = xs0.shape[0]
    return pl.pallas_call(
        _dispatch_kernel,
        grid_spec=pltpu.PrefetchScalarGridSpec(
            num_scalar_prefetch=1,
            grid=(b, nt),
            in_specs=[
                pl.BlockSpec((None, ROW_TILE, d), lambda i, t, pos: (i, t, 0)),
                pl.BlockSpec((ROW_TILE, LANES), lambda i, t, pos: (i * nt + t, 0)),
                pl.BlockSpec(memory_space=pl.ANY),
            ],
            out_specs=pl.BlockSpec(memory_space=pl.ANY),
            scratch_shapes=[pltpu.VMEM((2, ROW_TILE, d + LANES), F32), pltpu.SemaphoreType.DMA((2,))],
        ),
        out_shape=jax.ShapeDtypeStruct((n_rows, d + LANES), F32),
        input_output_aliases={3: 0},
        compiler_params=_params(("arbitrary", "arbitrary")),
        name="moe_dispatch",
    )(pos, h2, w_aug, xs0)


def _moe_pair_kernel(ea_ref, eb_ref, new_a_ref, new_b_ref, valid_ref, xs_ref,
                     gate_a, up_a, down_a, gate_b, up_b, down_b, y_ref, wgu_a, wd_a, wgu_b, wd_b):
    del ea_ref, eb_ref
    i = pl.program_id(0)
    d = y_ref.shape[1]
    d_exp = wd_a.shape[0]

    def load_expert(gate, up, down, wgu, wd):
        wgu[:, :d_exp] = gate[...].astype(BF16)
        wgu[:, d_exp:] = up[...].astype(BF16)
        wd[...] = down[...].astype(BF16)

    @pl.when(new_a_ref[i] != 0)
    def _():
        load_expert(gate_a, up_a, down_a, wgu_a, wd_a)

    @pl.when(new_b_ref[i] != 0)
    def _():
        load_expert(gate_b, up_b, down_b, wgu_b, wd_b)

    @pl.when(valid_ref[i] != 0)
    def _():
        xs = xs_ref[...]
        x = xs[:, :d].astype(BF16)

        def down(gu, wd):
            gate = gu[:, :d_exp]
            hid = gate * _sigmoid(gate) * gu[:, d_exp:]
            return jnp.dot(hid.astype(BF16), wd[...], preferred_element_type=F32)

        gu_a = jnp.dot(x, wgu_a[...], preferred_element_type=F32)
        gu_b = jnp.dot(x, wgu_b[...], preferred_element_type=F32)
        y_ref[...] = xs[:, d:d + 1] * down(gu_a, wd_a) + xs[:, d + 1:d + 2] * down(gu_b, wd_b)

    @pl.when(valid_ref[i] == 0)
    def _():
        y_ref[...] = jnp.zeros_like(y_ref)


def _moe_pairs(xs, tile_ea, tile_eb, new_a, new_b, tile_valid, w_gate, w_up, w_down, layer):
    n_rows, da = xs.shape
    d = da - LANES
    d_exp = w_gate.shape[-1]
    tm = ROW_TILE
    w_in = lambda pick: pl.BlockSpec((None, None, d, d_exp), lambda i, ea, eb, *_: (layer, pick(ea, eb)[i], 0, 0))
    w_out = lambda pick: pl.BlockSpec((None, None, d_exp, d), lambda i, ea, eb, *_: (layer, pick(ea, eb)[i], 0, 0))
    first = lambda ea, eb: ea
    second = lambda ea, eb: eb
    return pl.pallas_call(
        _moe_pair_kernel,
        grid_spec=pltpu.PrefetchScalarGridSpec(
            num_scalar_prefetch=5,
            grid=(n_rows // tm,),
            in_specs=[
                pl.BlockSpec((tm, da), lambda i, *_: (i, 0)),
                w_in(first), w_in(first), w_out(first), w_in(second), w_in(second), w_out(second),
            ],
            out_specs=pl.BlockSpec((tm, d), lambda i, *_: (i, 0)),
            scratch_shapes=[pltpu.VMEM((d, 2 * d_exp), BF16), pltpu.VMEM((d_exp, d), BF16)] * 2,
        ),
        out_shape=jax.ShapeDtypeStruct((n_rows, d), F32),
        compiler_params=_params(("arbitrary",)),
        name="moe_pairs",
    )(tile_ea, tile_eb, new_a, new_b, tile_valid, xs, w_gate, w_up, w_down, w_gate, w_up, w_down)


def _combine_kernel(pos_ref, x_ref, mod_ref, g_ref, ys_ref, o_ref, ybuf, sem, *, final):
    rows = x_ref.shape[0]
    i = pl.program_id(0) * pl.num_programs(1) + pl.program_id(1)
    n = pl.num_programs(0) * pl.num_programs(1)
    slot = i % 2

    def gather(step, s):
        base = step * rows

        def issue(g, c):
            r0 = pl.multiple_of(g * SUBLANES, SUBLANES)
            for j in range(SUBLANES):
                p = pos_ref[base + r0 + j]
                pltpu.make_async_copy(ys_ref.at[pl.ds(p, 1), :], ybuf.at[s, pl.ds(r0 + j, 1), :],
                                      sem.at[s]).start()
            return c

        lax.fori_loop(0, rows // SUBLANES, issue, 0)

    @pl.when(i == 0)
    def _():
        gather(0, 0)

    @pl.when(i + 1 < n)
    def _():
        gather(i + 1, 1 - slot)

    pltpu.make_async_copy(ys_ref.at[pl.ds(0, rows), :], ybuf.at[slot], sem.at[slot]).wait()
    x2 = x_ref[...] + mod_ref[5:6, :] * ybuf[slot]
    o_ref[...] = _rms(x2) * g_ref[...] if final else x2


def _combine(pos, x1, mods, gain, ys, has_ctx, final):
    b, l, d = x1.shape
    nt = l // ROW_TILE
    ctx_row = lambda i, t: jnp.where(t == 0, b, i) if has_ctx else i
    return pl.pallas_call(
        functools.partial(_combine_kernel, final=final),
        grid_spec=pltpu.PrefetchScalarGridSpec(
            num_scalar_prefetch=1,
            grid=(b, nt),
            in_specs=[
                pl.BlockSpec((None, ROW_TILE, d), lambda i, t, pos: (i, t, 0)),
                pl.BlockSpec((None, N_MOD, d), lambda i, t, pos: (ctx_row(i, t), 0, 0)),
                pl.BlockSpec((1, d), lambda i, t, pos: (0, 0)),
                pl.BlockSpec(memory_space=pl.ANY),
            ],
            out_specs=pl.BlockSpec((None, ROW_TILE, d), lambda i, t, pos: (i, t, 0)),
            scratch_shapes=[pltpu.VMEM((2, ROW_TILE, d), F32), pltpu.SemaphoreType.DMA((2,))],
        ),
        out_shape=jax.ShapeDtypeStruct((b, l, d), F32),
        compiler_params=_params(("arbitrary", "arbitrary")),
        name="moe_combine",
    )(pos, x1, mods, gain, ys)


def _rope_tables(n_ctx, n_lat):
    rows = n_lat // GRID_W
    r, col = jnp.meshgrid(jnp.arange(rows), jnp.arange(GRID_W), indexing="ij")
    r = r.reshape(-1).astype(F32)
    col = col.reshape(-1).astype(F32)
    half = HEAD_DIM // 2
    inv = ROPE_THETA ** (-jnp.arange(0, half, 2, dtype=F32) / half)
    ang_r = r[:, None] * inv
    ang_c = col[:, None] * inv
    cos_h = jnp.concatenate([jnp.cos(ang_r)] * 2 + [jnp.cos(ang_c)] * 2, axis=-1)
    sin_h = jnp.concatenate([-jnp.sin(ang_r), jnp.sin(ang_r), -jnp.sin(ang_c), jnp.sin(ang_c)], axis=-1)
    cos_t = jnp.concatenate([jnp.ones((n_ctx, HEAD_DIM), F32), cos_h], axis=0)
    sin_t = jnp.concatenate([jnp.zeros((n_ctx, HEAD_DIM), F32), sin_h], axis=0)
    reps = LANES // HEAD_DIM
    return jnp.tile(cos_t, (1, reps)), jnp.tile(sin_t, (1, reps))


def _block_diag(w):
    n, d, e = w.shape
    eye = jnp.eye(n, dtype=w.dtype)
    return (w[:, :, None, :] * eye[:, None, :, None]).reshape(n * d, n * e)


def kernel(x, c, ctx, c_ctx, ada_w, ada_b, norm1_g, w_in, q_norm_g, k_norm_g, conv_w, conv_b, lru_wa, lru_ba, lru_wx, lru_bx, lru_lambda, attn_out_g, lru_out_g, w_out, norm2_g, router_w, router_b, exp_w_gate, exp_w_up, exp_w_down, final_g):
    b, s, d = x.shape
    n_ctx = ctx.shape[1]
    l = n_ctx + s
    depth = ada_w.shape[0]
    nt = l // ROW_TILE
    assert n_ctx == ROW_TILE and s % ROW_TILE == 0

    x_lat, x_ctx, lat_off = x, ctx, 1
    cvec = jnp.zeros((2 * SUBLANES, d), F32).at[:b].set(c).at[b].set(c_ctx)
    mods_all = _ada_mods(cvec, ada_w, ada_b)[:, :b + 1].reshape(depth, b + 1, N_MOD, d)
    cos_t, sin_t = _rope_tables(n_ctx, s)

    head_avg = _block_diag(jnp.full((N_Q_HEADS, HEAD_DIM, HEAD_DIM), 1.0 / HEAD_DIM, F32)).astype(BF16)
    rw_t = router_w.T
    rw_hi = rw_t.astype(BF16)
    rw_lo = (rw_t - rw_hi.astype(F32)).astype(BF16)
    rb = jnp.broadcast_to(router_b[:, None], (N_EXPERTS, ROW_TILE)).astype(F32)
    tri = jnp.triu(jnp.ones((ROW_TILE, ROW_TILE), BF16), 1)

    n_tiles = b * nt + N_BUCKETS
    xs = jnp.zeros((n_tiles * ROW_TILE, d + LANES), F32)

    qw, kw = ATTN_WIDTH, KV_WIDTH
    out = None
    for li in range(depth):
        wi = w_in[li]
        k0, k1 = wi[:, qw:qw + HEAD_DIM], wi[:, qw + HEAD_DIM:qw + kw]
        v0, v1 = wi[:, qw + kw:qw + kw + HEAD_DIM], wi[:, qw + kw + HEAD_DIM:qw + 2 * kw]
        w_ext = jnp.concatenate([wi[:, :qw], k0, k0, k1, k1, v0, v0, v1, v1, wi[:, qw + 2 * kw:]],
                                axis=1).astype(BF16)
        gq = jnp.tile(q_norm_g[li], N_Q_HEADS)[None, :]
        gk = jnp.tile(k_norm_g[li], 2 * N_KV_HEADS)[None, :]
        mods = mods_all[li]
        q, kd, vd, u, gg = _inproj(x_lat, x_ctx, lat_off, mods, norm1_g[li][None, :], w_ext, cos_t, sin_t,
                                   gq, gk, head_avg)
        attn_n = _attention(q, kd, vd, attn_out_g[li][None, :], n_ctx)

        w_gate = jnp.concatenate([_block_diag(lru_wa[li, 0]), _block_diag(lru_wx[li, 0]),
                                  _block_diag(lru_wa[li, 1]), _block_diag(lru_wx[li, 1])], axis=1).astype(BF16)
        b_gate = jnp.concatenate([lru_ba[li, 0], lru_bx[li, 0], lru_ba[li, 1], lru_bx[li, 1]])[None, :]
        rec_n = _lru(u, gg, conv_w[li], conv_b[li][None, :], w_gate, b_gate, lru_lambda[li],
                     lru_out_g[li][None, :], n_ctx)

        skip = n_ctx // ROW_TILE if li == depth - 1 else 0
        x1, h2, route, counts = _outproj(x_lat, x_ctx, lat_off, attn_n, rec_n, mods, w_out[li].astype(BF16),
                                         norm2_g[li][None, :], rw_hi, rw_lo, rb, tri, skip)
        pos, tile_ea, tile_eb, new_a, new_b, tile_valid, w_aug = _moe_plan(route, counts, n_tiles)
        xs = _dispatch(pos, h2, w_aug, xs)
        ys = _moe_pairs(xs, tile_ea, tile_eb, new_a, new_b, tile_valid, exp_w_gate, exp_w_up, exp_w_down, li)
        if li == depth - 1:
            out = _combine(pos, x1, mods, final_g[None, :], ys, skip == 0, True)
        else:
            x_lat = x_ctx = _combine(pos, x1, mods, final_g[None, :], ys, skip == 0, False)
            lat_off = 0
    return out
```

```python
import functools

import jax
import jax.numpy as jnp
from jax import lax
from jax.experimental import pallas as pl
from jax.experimental.pallas import tpu as pltpu

F32 = jnp.float32
BF16 = jnp.bfloat16

HEAD_DIM = 64
N_Q_HEADS = 8
N_KV_HEADS = 2
ATTN_WIDTH = N_Q_HEADS * HEAD_DIM
KV_WIDTH = N_KV_HEADS * HEAD_DIM
LRU_WIDTH = 512
LRU_BLOCKS = 8
CONV_WIDTH = 4
CONV_LEFT = CONV_WIDTH // 2
LRU_C = 8.0
GRID_W = 64
ROPE_THETA = 10000.0
N_EXPERTS = 16
N_GROUPS = 4
GROUP_SIZE = N_EXPERTS // N_GROUPS
N_MOD = 6
NORM_EPS = 1e-6
ATTN_SCALE = HEAD_DIM ** -0.5
LOG2_E = 1.4426950408889634

LANES = 128
SUBLANES = 8
ROW_TILE = 256
BATCH_STEP = 2
N_PAIRS = GROUP_SIZE * (GROUP_SIZE - 1) // 2
N_BUCKETS = N_GROUPS * N_PAIRS
BUCKET_ROWS = 32
ROUTE_ROWS = 8
VMEM_LIMIT = 60000 * 1024


def _params(sem):
    return pltpu.CompilerParams(dimension_semantics=sem, vmem_limit_bytes=VMEM_LIMIT)


def _sigmoid(x):
    return 1.0 / (1.0 + jnp.exp(-x))


def _rms(x):
    return x * lax.rsqrt(jnp.mean(x * x, axis=-1, keepdims=True) + NORM_EPS)


def _split_bf16(x):
    hi = x.astype(BF16)
    lo = (x - hi.astype(F32)).astype(BF16)
    return hi, lo


def _ada_kernel(c_ref, w_ref, b_ref, o_ref):
    cv = c_ref[...]
    s = cv * _sigmoid(cv)
    o_ref[...] = jnp.dot(s.astype(BF16), w_ref[...].astype(BF16),
                         preferred_element_type=F32) + b_ref[...]


def _ada_mods(cvec, ada_w, ada_b):
    depth, d, n = ada_w.shape
    tn = 512
    rows = cvec.shape[0]
    return pl.pallas_call(
        _ada_kernel,
        grid=(depth, n // tn),
        in_specs=[
            pl.BlockSpec((rows, d), lambda l, j: (0, 0)),
            pl.BlockSpec((None, d, tn), lambda l, j: (l, 0, j)),
            pl.BlockSpec((None, 1, tn), lambda l, j: (l, 0, j)),
        ],
        out_specs=pl.BlockSpec((None, rows, tn), lambda l, j: (l, 0, j)),
        out_shape=jax.ShapeDtypeStruct((depth, rows, n), F32),
        compiler_params=_params(("parallel", "parallel")),
        name="ada_mods",
    )(cvec, ada_w, ada_b.reshape(depth, 1, n))


def _head_rms(t, g_mat, gain):
    hi, lo = _split_bf16(t * t)
    m = (jnp.dot(hi, g_mat, preferred_element_type=F32)
         + jnp.dot(lo, g_mat, preferred_element_type=F32))
    return t * lax.rsqrt(m + NORM_EPS) * gain


def _rope(t, cos_t, sin_t):
    width = t.shape[-1]
    reps = width // LANES
    cos_w = jnp.concatenate([cos_t] * reps, axis=1)
    sin_w = jnp.concatenate([sin_t] * reps, axis=1)
    lane = lax.broadcasted_iota(jnp.int32, t.shape, 1)
    quarter = HEAD_DIM // 4
    first = (lane % (2 * quarter)) < quarter
    partner = jnp.where(first, pltpu.roll(t, width - quarter, 1), pltpu.roll(t, quarter, 1))
    return t * cos_w + partner * sin_w


def _inproj_kernel(lat_ref, ctx_ref, *refs):
    mod_refs = refs[:BATCH_STEP]
    n1_ref, w_ref, cos_ref, sin_ref, gq_ref, gk_ref, gm_ref = refs[BATCH_STEP:BATCH_STEP + 7]
    q_ref, k_ref, v_ref, u_ref, gg_ref = refs[BATCH_STEP + 7:]
    kd = 2 * KV_WIDTH
    o_k, o_v, o_u, o_g = ATTN_WIDTH, ATTN_WIDTH + kd, ATTN_WIDTH + 2 * kd, ATTN_WIDTH + 2 * kd + LRU_WIDTH
    cos_t = cos_ref[...]
    sin_t = sin_ref[...]
    g_mat = gm_ref[...]
    is_ctx = pl.program_id(1) == 0
    for j, mod_ref in enumerate(mod_refs):
        x = jnp.where(is_ctx, ctx_ref[j], lat_ref[j])
        h = _rms(x) * n1_ref[...]
        h = h * (1.0 + mod_ref[1:2, :]) + mod_ref[0:1, :]
        y = jnp.dot(h.astype(BF16), w_ref[...], preferred_element_type=F32)
        q = _rope(_head_rms(y[:, :o_k], g_mat, gq_ref[...]), cos_t, sin_t)
        q_ref[j] = (q * (ATTN_SCALE * LOG2_E)).astype(BF16)
        k = _rope(_head_rms(y[:, o_k:o_v], g_mat[:kd, :kd], gk_ref[...]), cos_t, sin_t)
        k_ref[j] = k.astype(BF16)
        vt = y[:, o_v:o_u].T
        sub = lax.broadcasted_iota(jnp.int32, vt.shape, 0)
        v_ref[j] = jnp.where(sub % LANES < HEAD_DIM, vt, 1.0).astype(BF16)
        u_ref[j] = y[:, o_u:o_g]
        gb = y[:, o_g:]
        gg_ref[j] = 0.5 * gb * (1.0 + jnp.tanh(0.7978845608028654 * (gb + 0.044715 * gb * gb * gb)))


def _seq_specs(lat_off, d):
    lat = pl.BlockSpec((BATCH_STEP, ROW_TILE, d), lambda i, t: (i, jnp.maximum(t - lat_off, 0), 0))
    ctx = pl.BlockSpec((BATCH_STEP, ROW_TILE, d), lambda i, t: (i, 0, 0))
    return lat, ctx


def _mod_specs(n_batch, d, tile_of=lambda t: t):
    return [pl.BlockSpec((None, N_MOD, d),
                         lambda i, t, *_, j=j: (jnp.where(tile_of(t) == 0, n_batch, BATCH_STEP * i + j), 0, 0))
            for j in range(BATCH_STEP)]


def _inproj(x_lat, x_ctx, lat_off, mods, n1g, w_ext, cos_t, sin_t, gq, gk, g_mat):
    b, _, d = x_lat.shape
    l = x_lat.shape[1] + lat_off * x_ctx.shape[1]
    nt = l // ROW_TILE
    n_ext = w_ext.shape[1]
    kd = 2 * KV_WIDTH
    row = lambda w: pl.BlockSpec((BATCH_STEP, ROW_TILE, w), lambda i, t: (i, t, 0))
    const = lambda shape: pl.BlockSpec(shape, lambda i, t: (0,) * len(shape))
    return pl.pallas_call(
        _inproj_kernel,
        grid=(b // BATCH_STEP, nt),
        in_specs=[
            *_seq_specs(lat_off, d),
            *_mod_specs(b, d),
            const((1, d)),
            const((d, n_ext)),
            pl.BlockSpec((ROW_TILE, LANES), lambda i, t: (t, 0)),
            pl.BlockSpec((ROW_TILE, LANES), lambda i, t: (t, 0)),
            const((1, ATTN_WIDTH)),
            const((1, kd)),
            const((ATTN_WIDTH, ATTN_WIDTH)),
        ],
        out_specs=[row(ATTN_WIDTH), row(kd), pl.BlockSpec((BATCH_STEP, kd, ROW_TILE), lambda i, t: (i, 0, t)),
                   row(LRU_WIDTH), row(LRU_WIDTH)],
        out_shape=[
            jax.ShapeDtypeStruct((b, l, ATTN_WIDTH), BF16),
            jax.ShapeDtypeStruct((b, l, kd), BF16),
            jax.ShapeDtypeStruct((b, kd, l), BF16),
            jax.ShapeDtypeStruct((b, l, LRU_WIDTH), F32),
            jax.ShapeDtypeStruct((b, l, LRU_WIDTH), F32),
        ],
        compiler_params=_params(("parallel", "parallel")),
        name="in_proj",
    )(x_lat, x_ctx, *([mods] * BATCH_STEP), n1g, w_ext, cos_t, sin_t, gq, gk, g_mat)


def _attn_kernel(q_ref, k_ref, v_ref, g_ref, o_ref, *, n_ctx):
    t = pl.program_id(1)
    rows = q_ref.shape[0]
    n_all = k_ref.shape[0]

    def run(n_keys):
        low = lax.broadcasted_iota(jnp.int32, (rows, LANES), 1) < HEAD_DIM
        n_pairs = N_Q_HEADS // 2
        kv_of = lambda pair: (2 * pair) // (N_Q_HEADS // N_KV_HEADS)

        def scores(pair):
            qp = q_ref[:, pair * LANES:(pair + 1) * LANES]
            zero = jnp.zeros_like(qp)
            q2 = jnp.concatenate([jnp.where(low, qp, zero), jnp.where(low, zero, qp)], axis=0)
            kd = k_ref[0:n_keys, kv_of(pair) * LANES:(kv_of(pair) + 1) * LANES]
            return lax.dot_general(kd, q2, (((1,), (1,)), ((), ())), preferred_element_type=F32)

        def softmax_pv(pair, st):
            vt = v_ref[kv_of(pair) * LANES:(kv_of(pair) + 1) * LANES, 0:n_keys]
            p = jnp.exp2(st - jnp.max(st, axis=0, keepdims=True))
            ot = jnp.dot(vt, p.astype(BF16), preferred_element_type=F32)
            on = ot[:HEAD_DIM, :] * (1.0 / ot[HEAD_DIM:HEAD_DIM + 1, :])
            return jnp.concatenate([on[:, :rows], on[:, rows:]], axis=0).T

        outs = []
        st = scores(0)
        for pair in range(n_pairs):
            st_next = scores(pair + 1) if pair + 1 < n_pairs else None
            outs.append(softmax_pv(pair, st))
            st = st_next
        a = jnp.concatenate(outs, axis=1)
        o_ref[...] = (_rms(a) * g_ref[...]).astype(BF16)

    @pl.when(t == 0)
    def _():
        run(n_ctx)

    @pl.when(t > 0)
    def _():
        run(n_all)


def _attention(q, kd, vd, gain, n_ctx):
    b, l, w = q.shape
    nt = l // ROW_TILE
    kw = kd.shape[-1]
    return pl.pallas_call(
        functools.partial(_attn_kernel, n_ctx=n_ctx),
        grid=(b, nt),
        in_specs=[
            pl.BlockSpec((None, ROW_TILE, w), lambda i, t: (i, t, 0)),
            pl.BlockSpec((None, l, kw), lambda i, t: (i, 0, 0)),
            pl.BlockSpec((None, kw, l), lambda i, t: (i, 0, 0)),
            pl.BlockSpec((1, w), lambda i, t: (0, 0)),
        ],
        out_specs=pl.BlockSpec((None, ROW_TILE, w), lambda i, t: (i, t, 0)),
        out_shape=jax.ShapeDtypeStruct((b, l, w), BF16),
        compiler_params=_params(("parallel", "parallel")),
        name="attention",
    )(q, kd, vd, gain)


def _lru_kernel(u_ref, gg_ref, cw_ref, cb_ref, wg_ref, bg_ref, lam_ref, og_ref, o_ref,
                upad, a_f, b_f, a_r, b_r, *, n_ctx):
    l, w = u_ref.shape
    pad = SUBLANES
    zeros_pad = jnp.zeros((pad, w), F32)
    upad[0:pad, :] = zeros_pad
    upad[pad + l:2 * pad + l, :] = zeros_pad
    for r0 in range(0, l, ROW_TILE):
        upad[pad + r0:pad + r0 + ROW_TILE, :] = u_ref[r0:r0 + ROW_TILE, :]

    neg_lam = -lam_ref[...]
    softplus = jnp.maximum(neg_lam, 0.0) + jnp.log1p(jnp.exp(-jnp.abs(neg_lam)))
    cw = cw_ref[...]
    cb = cb_ref[...]

    for r0 in range(0, l, ROW_TILE):
        row = r0 + lax.broadcasted_iota(jnp.int32, (ROW_TILE, w), 0)
        is_lat = row >= n_ctx
        near_boundary = r0 - CONV_WIDTH < n_ctx < r0 + ROW_TILE + CONV_WIDTH
        uc = jnp.zeros((ROW_TILE, w), F32) + cb
        for j in range(CONV_WIDTH):
            off = j - CONV_LEFT
            tap = upad[pad + r0 + off:pad + r0 + off + ROW_TILE, :]
            if off != 0 and near_boundary:
                tap = jnp.where(((row + off) >= n_ctx) == is_lat, tap, 0.0)
            uc = uc + tap * cw[j:j + 1, :]
        z = jnp.dot(uc.astype(BF16), wg_ref[...], preferred_element_type=F32) + bg_ref[...]
        for d, (a_ref, b_ref) in enumerate(((a_f, b_f), (a_r, b_r))):
            base = 2 * d * w
            r_gate = _sigmoid(z[:, base:base + w])
            i_gate = _sigmoid(z[:, base + w:base + 2 * w])
            log_a = (-LRU_C * r_gate) * softplus[d:d + 1, :]
            a = jnp.exp(log_a)
            one_minus_a2 = -jnp.tanh(log_a) * (a * a + 1.0)
            bb = jnp.sqrt(one_minus_a2) * (i_gate * uc)
            a_ref[r0:r0 + ROW_TILE, :] = a
            b_ref[r0:r0 + ROW_TILE, :] = bb

    sub = lax.broadcasted_iota(jnp.int32, (SUBLANES, w), 0)

    def tile_scan(a_ref, b_ref, blk, carry, reverse):
        rows = pl.ds(pl.multiple_of(blk * SUBLANES, SUBLANES), SUBLANES)
        a = a_ref[rows, :]
        b = b_ref[rows, :]
        s = 1
        while s < SUBLANES:
            keep = (sub < SUBLANES - s) if reverse else (sub >= s)
            shift = SUBLANES - s if reverse else s
            b = b + a * jnp.where(keep, pltpu.roll(b, shift, 0), 0.0)
            a = a * jnp.where(keep, pltpu.roll(a, shift, 0), 1.0)
            s *= 2
        h = b + a * carry
        b_ref[rows, :] = h
        last = h[0:1, :] if reverse else h[SUBLANES - 1:SUBLANES, :]
        return jnp.broadcast_to(last, h.shape)

    n_blk = l // SUBLANES
    c_blk = n_ctx // SUBLANES

    def ctx_body(i, carry):
        cf, cr = carry
        return tile_scan(a_f, b_f, i, cf, False), tile_scan(a_r, b_r, c_blk - 1 - i, cr, True)

    def lat_body(i, carry):
        cf, cr = carry
        return tile_scan(a_f, b_f, i, cf, False), tile_scan(a_r, b_r, n_blk + c_blk - 1 - i, cr, True)

    zero = jnp.zeros((SUBLANES, w), F32)
    carry = lax.fori_loop(0, c_blk, ctx_body, (zero, zero), unroll=2)
    lax.fori_loop(c_blk, n_blk, lat_body, carry, unroll=2)

    gain = og_ref[...]
    for r0 in range(0, l, ROW_TILE):
        rows = slice(r0, r0 + ROW_TILE)
        h = b_f[rows, :] + b_r[rows, :]
        o_ref[rows, :] = (_rms(h * gg_ref[rows, :]) * gain).astype(BF16)


def _lru(u, gg, conv_w, conv_b, w_gate, b_gate, lam, out_g, n_ctx):
    b, l, w = u.shape
    const = lambda shape: pl.BlockSpec(shape, lambda i: (0,) * len(shape))
    seq = pl.BlockSpec((None, l, w), lambda i: (i, 0, 0))
    return pl.pallas_call(
        functools.partial(_lru_kernel, n_ctx=n_ctx),
        grid=(b,),
        in_specs=[seq, seq, const(conv_w.shape), const((1, w)), const(w_gate.shape),
                  const(b_gate.shape), const(lam.shape), const((1, w))],
        out_specs=seq,
        out_shape=jax.ShapeDtypeStruct((b, l, w), BF16),
        scratch_shapes=[pltpu.VMEM((l + 2 * SUBLANES, w), F32)] + [pltpu.VMEM((l, w), F32)] * 4,
        compiler_params=_params(("parallel",)),
        name="rg_lru",
    )(u, gg, conv_w, conv_b, w_gate, b_gate, lam, out_g)


def _route(scores, biased):
    group_score = []
    for g in range(N_GROUPS):
        v = biased[g * GROUP_SIZE:(g + 1) * GROUP_SIZE]
        best = None
        for i in range(GROUP_SIZE):
            for j in range(i + 1, GROUP_SIZE):
                pair = v[i] + v[j]
                best = pair if best is None else jnp.maximum(best, pair)
        group_score.append(best)
    gid = jnp.zeros_like(group_score[0], dtype=jnp.int32)
    gbest = group_score[0]
    for g in range(1, N_GROUPS):
        upd = group_score[g] > gbest
        gid = jnp.where(upd, g, gid)
        gbest = jnp.where(upd, group_score[g], gbest)

    def pick(rows, j):
        out = rows[j]
        for g in range(1, N_GROUPS):
            out = jnp.where(gid == g, rows[g * GROUP_SIZE + j], out)
        return out

    v = [pick(biased, j) for j in range(GROUP_SIZE)]
    s = [pick(scores, j) for j in range(GROUP_SIZE)]
    i1 = jnp.zeros_like(gid)
    m1 = v[0]
    for j in range(1, GROUP_SIZE):
        upd = v[j] > m1
        i1 = jnp.where(upd, j, i1)
        m1 = jnp.where(upd, v[j], m1)
    i2 = jnp.zeros_like(gid)
    m2 = jnp.full_like(m1, -jnp.inf)
    for j in range(GROUP_SIZE):
        upd = (i1 != j) & (v[j] > m2)
        i2 = jnp.where(upd, j, i2)
        m2 = jnp.where(upd, v[j], m2)
    w1 = s[0]
    w2 = s[0]
    for j in range(1, GROUP_SIZE):
        w1 = jnp.where(i1 == j, s[j], w1)
        w2 = jnp.where(i2 == j, s[j], w2)
    den = w1 + w2
    w1 = w1 / den
    w2 = w2 / den
    lo = jnp.minimum(i1, i2)
    hi = jnp.maximum(i1, i2)
    pair = hi - lo - 1
    for k in range(1, GROUP_SIZE - 1):
        pair = pair + jnp.where(lo >= k, GROUP_SIZE - k, 0)
    first_is_lo = i1 < i2
    return gid * N_PAIRS + pair, jnp.where(first_is_lo, w1, w2), jnp.where(first_is_lo, w2, w1)


def _outproj_kernel(lat_ref, ctx_ref, a_ref, r_ref, *refs, skip):
    mod_refs = refs[:BATCH_STEP]
    w_ref, n2_ref, rwh_ref, rwl_ref, rb_ref, tri_ref = refs[BATCH_STEP:BATCH_STEP + 6]
    x1_ref, h2_ref, route_ref, counts_ref, cnt_ref = refs[BATCH_STEP + 6:]
    tile_rows = lat_ref.shape[1]
    rows = BATCH_STEP * tile_rows
    is_ctx = pl.program_id(1) + skip == 0
    half = a_ref.shape[-1]
    h2_parts = []
    for j, mod_ref in enumerate(mod_refs):
        x = jnp.where(is_ctx, ctx_ref[j], lat_ref[j])
        mix = (jnp.dot(a_ref[j], w_ref[0:half, :], preferred_element_type=F32)
               + jnp.dot(r_ref[j], w_ref[half:, :], preferred_element_type=F32))
        x1 = x + mod_ref[2:3, :] * mix
        x1_ref[j] = x1
        h2 = _rms(x1) * n2_ref[...]
        h2 = h2 * (1.0 + mod_ref[4:5, :]) + mod_ref[3:4, :]
        h2_ref[j] = h2
        h2_parts.append(h2)
    h2 = jnp.concatenate(h2_parts, axis=0)
    hi, lo = _split_bf16(h2)
    nt = (((1,), (1,)), ((), ()))
    logits = (lax.dot_general(rwh_ref[...], hi, nt, preferred_element_type=F32)
              + lax.dot_general(rwh_ref[...], lo, nt, preferred_element_type=F32)
              + lax.dot_general(rwl_ref[...], hi, nt, preferred_element_type=F32))
    scores = _sigmoid(logits)
    biased = scores + rb_ref[...]
    bucket, w_lo, w_hi = _route([scores[e:e + 1, :] for e in range(N_EXPERTS)],
                                [biased[e:e + 1, :] for e in range(N_EXPERTS)])

    @pl.when((pl.program_id(0) == 0) & (pl.program_id(1) == 0))
    def _():
        cnt_ref[...] = jnp.zeros_like(cnt_ref)

    sub = lax.broadcasted_iota(jnp.int32, (BUCKET_ROWS, rows), 0)
    onehot = jnp.where(sub == bucket, 1.0, 0.0)
    before = jnp.dot(onehot.astype(BF16), tri_ref[...], preferred_element_type=F32) + cnt_ref[...]
    rank = jnp.sum(onehot * before, axis=0, keepdims=True)
    total = cnt_ref[...] + jnp.sum(onehot, axis=1, keepdims=True)
    cnt_ref[...] = total
    counts_ref[...] = total

    for j in range(BATCH_STEP):
        cols = slice(j * tile_rows, (j + 1) * tile_rows)
        route_ref[j, 0:1, :] = bucket.astype(F32)[:, cols]
        route_ref[j, 1:2, :] = rank[:, cols]
        route_ref[j, 2:3, :] = w_lo[:, cols]
        route_ref[j, 3:4, :] = w_hi[:, cols]
        route_ref[j, 4:, :] = jnp.zeros((ROUTE_ROWS - 4, tile_rows), F32)


def _outproj(x_lat, x_ctx, lat_off, attn_n, rec_n, mods, w_out, n2g, rw_hi, rw_lo, rb, tri, skip):
    b, l, _ = attn_n.shape
    d = x_lat.shape[-1]
    nt = l // ROW_TILE - skip
    half = attn_n.shape[-1]
    row = lambda w: pl.BlockSpec((BATCH_STEP, ROW_TILE, w), lambda i, t: (i, t + skip, 0))
    out_row = pl.BlockSpec((BATCH_STEP, ROW_TILE, d), lambda i, t: (i, t, 0))
    const = lambda shape: pl.BlockSpec(shape, lambda i, t: (0,) * len(shape))
    step_rows = BATCH_STEP * ROW_TILE
    return pl.pallas_call(
        functools.partial(_outproj_kernel, skip=skip),
        grid=(b // BATCH_STEP, nt),
        in_specs=[
            pl.BlockSpec((BATCH_STEP, ROW_TILE, d), lambda i, t: (i, jnp.maximum(t + skip - lat_off, 0), 0)),
            pl.BlockSpec((BATCH_STEP, ROW_TILE, d), lambda i, t: (i, 0, 0)),
            row(half), row(half),
            *_mod_specs(b, d, lambda t: t + skip),
            const(w_out.shape), const((1, d)), const(rw_hi.shape), const(rw_lo.shape), const(rb.shape),
            const(tri.shape),
        ],
        out_specs=[out_row, out_row,
                   pl.BlockSpec((BATCH_STEP, None, ROUTE_ROWS, ROW_TILE), lambda i, t: (i, t, 0, 0)),
                   const((BUCKET_ROWS, step_rows))],
        out_shape=[
            jax.ShapeDtypeStruct((b, nt * ROW_TILE, d), F32),
            jax.ShapeDtypeStruct((b, nt * ROW_TILE, d), F32),
            jax.ShapeDtypeStruct((b, nt, ROUTE_ROWS, ROW_TILE), F32),
            jax.ShapeDtypeStruct((BUCKET_ROWS, step_rows), F32),
        ],
        scratch_shapes=[pltpu.VMEM((BUCKET_ROWS, step_rows), F32)],
        compiler_params=_params(("arbitrary", "arbitrary")),
        name="out_proj_router",
    )(x_lat, x_ctx, attn_n, rec_n, *([mods] * BATCH_STEP), w_out, n2g, rw_hi, rw_lo, rb, tri)


def _moe_plan(route, counts, n_tiles):
    tm = ROW_TILE
    bucket = route[:, :, 0, :].reshape(-1).astype(jnp.int32)
    rank = route[:, :, 1, :].reshape(-1).astype(jnp.int32)
    cnt = counts[:N_BUCKETS, 0].astype(jnp.int32)
    tiles_per = (cnt + tm - 1) // tm
    tile_end = jnp.cumsum(tiles_per)
    pos = (tile_end - tiles_per)[bucket] * tm + rank
    total = tile_end[-1]
    tile = jnp.arange(n_tiles, dtype=jnp.int32)
    last = jnp.minimum(tile, total - 1)
    tile_bucket = jnp.sum((tile_end[None, :] <= last[:, None]).astype(jnp.int32), axis=1)
    tile_bucket = jnp.minimum(tile_bucket, N_BUCKETS - 1)
    group, pair = tile_bucket // N_PAIRS, tile_bucket % N_PAIRS
    pairs = [(i, j) for i in range(GROUP_SIZE) for j in range(i + 1, GROUP_SIZE)]
    lo = jnp.array([p[0] for p in pairs], jnp.int32)[pair]
    hi = jnp.array([p[1] for p in pairs], jnp.int32)[pair]
    weights = route[:, :, 2:4, :].transpose(0, 1, 3, 2).reshape(-1, 2)
    w_aug = jnp.pad(weights, ((0, 0), (0, LANES - 2)))
    tile_ea = group * GROUP_SIZE + lo
    tile_eb = group * GROUP_SIZE + hi
    changed = lambda e: jnp.concatenate([jnp.ones((1,), jnp.int32), (e[1:] != e[:-1]).astype(jnp.int32)])
    return (pos, tile_ea, tile_eb, changed(tile_ea), changed(tile_eb), (tile < total).astype(jnp.int32), w_aug)


def _dispatch_kernel(pos_ref, h_ref, w_ref, xs_in_ref, xs_ref, stage, sem):
    del xs_in_ref
    rows, d = h_ref.shape
    i = pl.program_id(0) * pl.num_programs(1) + pl.program_id(1)
    n = pl.num_programs(0) * pl.num_programs(1)
    slot = i % 2

    def drain(s):
        pltpu.make_async_copy(stage.at[s], xs_ref.at[pl.ds(0, rows), :], sem.at[s]).wait()

    @pl.when(i >= 2)
    def _():
        drain(slot)

    stage[slot, :, 0:d] = h_ref[...]
    stage[slot, :, d:] = w_ref[...]
    base = i * rows

    def issue(g, c):
        r0 = pl.multiple_of(g * SUBLANES, SUBLANES)
        for j in range(SUBLANES):
            p = pos_ref[base + r0 + j]
            pltpu.make_async_copy(stage.at[slot, pl.ds(r0 + j, 1), :], xs_ref.at[pl.ds(p, 1), :],
                                  sem.at[slot]).start()
        return c

    lax.fori_loop(0, rows // SUBLANES, issue, 0)

    @pl.when(i == n - 1)
    def _():
        drain(1 - slot)
        drain(slot)


def _dispatch(pos, h2, w_aug, xs0):
    b, l, d = h2.shape
    nt = l // ROW_TILE
    assert b * nt >= 2
    n_rows = xs0.shape[0]
    return pl.pallas_call(
        _dispatch_kernel,
        grid_spec=pltpu.PrefetchScalarGridSpec(
            num_scalar_prefetch=1,
            grid=(b, nt),
            in_specs=[
                pl.BlockSpec((None, ROW_TILE, d), lambda i, t, pos: (i, t, 0)),
                pl.BlockSpec((ROW_TILE, LANES), lambda i, t, pos: (i * nt + t, 0)),
                pl.BlockSpec(memory_space=pl.ANY),
            ],
            out_specs=pl.BlockSpec(memory_space=pl.ANY),
            scratch_shapes=[pltpu.VMEM((2, ROW_TILE, d + LANES), F32), pltpu.SemaphoreType.DMA((2,))],
        ),
        out_shape=jax.ShapeDtypeStruct((n_rows, d + LANES), F32),
        input_output_aliases={3: 0},
        compiler_params=_params(("arbitrary", "arbitrary")),
        name="moe_dispatch",
    )(pos, h2, w_aug, xs0)


def _moe_pair_kernel(ea_ref, eb_ref, new_a_ref, new_b_ref, valid_ref, xs_ref,
                     gate_a, up_a, down_a, gate_b, up_b, down_b, y_ref, wgu_a, wd_a, wgu_b, wd_b):
    del ea_ref, eb_ref
    i = pl.program_id(0)
    d = y_ref.shape[1]
    d_exp = wd_a.shape[0]

    def load_expert(gate, up, down, wgu, wd):
        wgu[:, :d_exp] = gate[...].astype(BF16)
        wgu[:, d_exp:] = up[...].astype(BF16)
        wd[...] = down[...].astype(BF16)

    @pl.when(new_a_ref[i] != 0)
    def _():
        load_expert(gate_a, up_a, down_a, wgu_a, wd_a)

    @pl.when(new_b_ref[i] != 0)
    def _():
        load_expert(gate_b, up_b, down_b, wgu_b, wd_b)

    @pl.when(valid_ref[i] != 0)
    def _():
        xs = xs_ref[...]
        x = xs[:, :d].astype(BF16)

        def down(gu, wd):
            gate = gu[:, :d_exp]
            hid = gate * _sigmoid(gate) * gu[:, d_exp:]
            return jnp.dot(hid.astype(BF16), wd[...], preferred_element_type=F32)

        gu_a = jnp.dot(x, wgu_a[...], preferred_element_type=F32)
        gu_b = jnp.dot(x, wgu_b[...], preferred_element_type=F32)
        y_ref[...] = xs[:, d:d + 1] * down(gu_a, wd_a) + xs[:, d + 1:d + 2] * down(gu_b, wd_b)

    @pl.when(valid_ref[i] == 0)
    def _():
        y_ref[...] = jnp.zeros_like(y_ref)


def _moe_pairs(xs, tile_ea, tile_eb, new_a, new_b, tile_valid, w_gate, w_up, w_down, layer):
    n_rows, da = xs.shape
    d = da - LANES
    d_exp = w_gate.shape[-1]
    tm = ROW_TILE
    w_in = lambda pick: pl.BlockSpec((None, None, d, d_exp), lambda i, ea, eb, *_: (layer, pick(ea, eb)[i], 0, 0))
    w_out = lambda pick: pl.BlockSpec((None, None, d_exp, d), lambda i, ea, eb, *_: (layer, pick(ea, eb)[i], 0, 0))
    first = lambda ea, eb: ea
    second = lambda ea, eb: eb
    return pl.pallas_call(
        _moe_pair_kernel,
        grid_spec=pltpu.PrefetchScalarGridSpec(
            num_scalar_prefetch=5,
            grid=(n_rows // tm,),
            in_specs=[
                pl.BlockSpec((tm, da), lambda i, *_: (i, 0)),
                w_in(first), w_in(first), w_out(first), w_in(second), w_in(second), w_out(second),
            ],
            out_specs=pl.BlockSpec((tm, d), lambda i, *_: (i, 0)),
            scratch_shapes=[pltpu.VMEM((d, 2 * d_exp), BF16), pltpu.VMEM((d_exp, d), BF16)] * 2,
        ),
        out_shape=jax.ShapeDtypeStruct((n_rows, d), F32),
        compiler_params=_params(("arbitrary",)),
        name="moe_pairs",
    )(tile_ea, tile_eb, new_a, new_b, tile_valid, xs, w_gate, w_up, w_down, w_gate, w_up, w_down)


def _combine_kernel(pos_ref, x_ref, mod_ref, g_ref, ys_ref, o_ref, ybuf, sem, *, final):
    rows = x_ref.shape[0]
    i = pl.program_id(0) * pl.num_programs(1) + pl.program_id(1)
    n = pl.num_programs(0) * pl.num_programs(1)
    slot = i % 2

    def gather(step, s):
        base = step * rows

        def issue(g, c):
            r0 = pl.multiple_of(g * SUBLANES, SUBLANES)
            for j in range(SUBLANES):
                p = pos_ref[base + r0 + j]
                pltpu.make_async_copy(ys_ref.at[pl.ds(p, 1), :], ybuf.at[s, pl.ds(r0 + j, 1), :],
                                      sem.at[s]).start()
            return c

        lax.fori_loop(0, rows // SUBLANES, issue, 0)

    @pl.when(i == 0)
    def _():
        gather(0, 0)

    @pl.when(i + 1 < n)
    def _():
        gather(i + 1, 1 - slot)

    pltpu.make_async_copy(ys_ref.at[pl.ds(0, rows), :], ybuf.at[slot], sem.at[slot]).wait()
    x2 = x_ref[...] + mod_ref[5:6, :] * ybuf[slot]
    o_ref[...] = _rms(x2) * g_ref[...] if final else x2


def _combine(pos, x1, mods, gain, ys, has_ctx, final):
    b, l, d = x1.shape
    nt = l // ROW_TILE
    ctx_row = lambda i, t: jnp.where(t == 0, b, i) if has_ctx else i
    return pl.pallas_call(
        functools.partial(_combine_kernel, final=final),
        grid_spec=pltpu.PrefetchScalarGridSpec(
            num_scalar_prefetch=1,
            grid=(b, nt),
            in_specs=[
                pl.BlockSpec((None, ROW_TILE, d), lambda i, t, pos: (i, t, 0)),
                pl.BlockSpec((None, N_MOD, d), lambda i, t, pos: (ctx_row(i, t), 0, 0)),
                pl.BlockSpec((1, d), lambda i, t, pos: (0, 0)),
                pl.BlockSpec(memory_space=pl.ANY),
            ],
            out_specs=pl.BlockSpec((None, ROW_TILE, d), lambda i, t, pos: (i, t, 0)),
            scratch_shapes=[pltpu.VMEM((2, ROW_TILE, d), F32), pltpu.SemaphoreType.DMA((2,))],
        ),
        out_shape=jax.ShapeDtypeStruct((b, l, d), F32),
        compiler_params=_params(("arbitrary", "arbitrary")),
        name="moe_combine",
    )(pos, x1, mods, gain, ys)


def _rope_tables(n_ctx, n_lat):
    rows = n_lat // GRID_W
    r, col = jnp.meshgrid(jnp.arange(rows), jnp.arange(GRID_W), indexing="ij")
    r = r.reshape(-1).astype(F32)
    col = col.reshape(-1).astype(F32)
    half = HEAD_DIM // 2
    inv = ROPE_THETA ** (-jnp.arange(0, half, 2, dtype=F32) / half)
    ang_r = r[:, None] * inv
    ang_c = col[:, None] * inv
    cos_h = jnp.concatenate([jnp.cos(ang_r)] * 2 + [jnp.cos(ang_c)] * 2, axis=-1)
    sin_h = jnp.concatenate([-jnp.sin(ang_r), jnp.sin(ang_r), -jnp.sin(ang_c), jnp.sin(ang_c)], axis=-1)
    cos_t = jnp.concatenate([jnp.ones((n_ctx, HEAD_DIM), F32), cos_h], axis=0)
    sin_t = jnp.concatenate([jnp.zeros((n_ctx, HEAD_DIM), F32), sin_h], axis=0)
    reps = LANES // HEAD_DIM
    return jnp.tile(cos_t, (1, reps)), jnp.tile(sin_t, (1, reps))


def _block_diag(w):
    n, d, e = w.shape
    eye = jnp.eye(n, dtype=w.dtype)
    return (w[:, :, None, :] * eye[:, None, :, None]).reshape(n * d, n * e)


def kernel(x, c, ctx, c_ctx, ada_w, ada_b, norm1_g, w_in, q_norm_g, k_norm_g, conv_w, conv_b, lru_wa, lru_ba, lru_wx, lru_bx, lru_lambda, attn_out_g, lru_out_g, w_out, norm2_g, router_w, router_b, exp_w_gate, exp_w_up, exp_w_down, final_g):
    b, s, d = x.shape
    n_ctx = ctx.shape[1]
    l = n_ctx + s
    depth = ada_w.shape[0]
    nt = l // ROW_TILE
    assert n_ctx == ROW_TILE and s % ROW_TILE == 0 and b % BATCH_STEP == 0

    x_lat, x_ctx, lat_off = x, ctx, 1
    cvec = jnp.zeros((2 * SUBLANES, d), F32).at[:b].set(c).at[b].set(c_ctx)
    mods_all = _ada_mods(cvec, ada_w, ada_b)[:, :b + 1].reshape(depth, b + 1, N_MOD, d)
    cos_t, sin_t = _rope_tables(n_ctx, s)

    head_avg = _block_diag(jnp.full((N_Q_HEADS, HEAD_DIM, HEAD_DIM), 1.0 / HEAD_DIM, F32)).astype(BF16)
    rw_t = router_w.T
    rw_hi = rw_t.astype(BF16)
    rw_lo = (rw_t - rw_hi.astype(F32)).astype(BF16)
    step_rows = BATCH_STEP * ROW_TILE
    rb = jnp.broadcast_to(router_b[:, None], (N_EXPERTS, step_rows)).astype(F32)
    tri = jnp.triu(jnp.ones((step_rows, step_rows), BF16), 1)

    n_tiles = b * nt + N_BUCKETS
    xs = jnp.zeros((n_tiles * ROW_TILE, d + LANES), F32)

    qw, kw = ATTN_WIDTH, KV_WIDTH
    out = None
    for li in range(depth):
        wi = w_in[li]
        k0, k1 = wi[:, qw:qw + HEAD_DIM], wi[:, qw + HEAD_DIM:qw + kw]
        v0, v1 = wi[:, qw + kw:qw + kw + HEAD_DIM], wi[:, qw + kw + HEAD_DIM:qw + 2 * kw]
        w_ext = jnp.concatenate([wi[:, :qw], k0, k0, k1, k1, v0, v0, v1, v1, wi[:, qw + 2 * kw:]],
                                axis=1).astype(BF16)
        gq = jnp.tile(q_norm_g[li], N_Q_HEADS)[None, :]
        gk = jnp.tile(k_norm_g[li], 2 * N_KV_HEADS)[None, :]
        mods = mods_all[li]
        q, kd, vd, u, gg = _inproj(x_lat, x_ctx, lat_off, mods, norm1_g[li][None, :], w_ext, cos_t, sin_t,
                                   gq, gk, head_avg)
        attn_n = _attention(q, kd, vd, attn_out_g[li][None, :], n_ctx)

        w_gate = jnp.concatenate([_block_diag(lru_wa[li, 0]), _block_diag(lru_wx[li, 0]),
                                  _block_diag(lru_wa[li, 1]), _block_diag(lru_wx[li, 1])], axis=1).astype(BF16)
        b_gate = jnp.concatenate([lru_ba[li, 0], lru_bx[li, 0], lru_ba[li, 1], lru_bx[li, 1]])[None, :]
        rec_n = _lru(u, gg, conv_w[li], conv_b[li][None, :], w_gate, b_gate, lru_lambda[li],
                     lru_out_g[li][None, :], n_ctx)

        skip = n_ctx // ROW_TILE if li == depth - 1 else 0
        x1, h2, route, counts = _outproj(x_lat, x_ctx, lat_off, attn_n, rec_n, mods, w_out[li].astype(BF16),
                                         norm2_g[li][None, :], rw_hi, rw_lo, rb, tri, skip)
        pos, tile_ea, tile_eb, new_a, new_b, tile_valid, w_aug = _moe_plan(route, counts, n_tiles)
        xs = _dispatch(pos, h2, w_aug, xs)
        ys = _moe_pairs(xs, tile_ea, tile_eb, new_a, new_b, tile_valid, exp_w_gate, exp_w_up, exp_w_down, li)
        if li == depth - 1:
            out = _combine(pos, x1, mods, final_g[None, :], ys, skip == 0, True)
        else:
            x_lat = x_ctx = _combine(pos, x1, mods, final_g[None, :], ys, skip == 0, False)
            lat_off = 0
    return out
```

```python
import functools

import jax
import jax.numpy as jnp
from jax import lax
from jax.experimental import pallas as pl
from jax.experimental.pallas import tpu as pltpu

F32 = jnp.float32
BF16 = jnp.bfloat16

HEAD_DIM = 64
N_Q_HEADS = 8
N_KV_HEADS = 2
ATTN_WIDTH = N_Q_HEADS * HEAD_DIM
KV_WIDTH = N_KV_HEADS * HEAD_DIM
LRU_WIDTH = 512
LRU_BLOCKS = 8
CONV_WIDTH = 4
CONV_LEFT = CONV_WIDTH // 2
LRU_C = 8.0
GRID_W = 64
ROPE_THETA = 10000.0
N_EXPERTS = 16
N_GROUPS = 4
GROUP_SIZE = N_EXPERTS // N_GROUPS
N_MOD = 6
NORM_EPS = 1e-6
ATTN_SCALE = HEAD_DIM ** -0.5
LOG2_E = 1.4426950408889634

LANES = 128
SUBLANES = 8
ROW_TILE = 256
BATCH_STEP = 2
N_PAIRS = GROUP_SIZE * (GROUP_SIZE - 1) // 2
N_BUCKETS = N_GROUPS * N_PAIRS
BUCKET_ROWS = 32
ROUTE_ROWS = 8
VMEM_LIMIT = 60000 * 1024


def _params(sem):
    return pltpu.CompilerParams(dimension_semantics=sem, vmem_limit_bytes=VMEM_LIMIT)


def _sigmoid(x):
    return 1.0 / (1.0 + jnp.exp(-x))


def _rms(x):
    return x * lax.rsqrt(jnp.mean(x * x, axis=-1, keepdims=True) + NORM_EPS)


def _split_bf16(x):
    hi = x.astype(BF16)
    lo = (x - hi.astype(F32)).astype(BF16)
    return hi, lo


def _ada_kernel(c_ref, w_ref, b_ref, o_ref):
    cv = c_ref[...]
    s = cv * _sigmoid(cv)
    o_ref[...] = jnp.dot(s.astype(BF16), w_ref[...].astype(BF16),
                         preferred_element_type=F32) + b_ref[...]


def _ada_mods(cvec, ada_w, ada_b):
    depth, d, n = ada_w.shape
    tn = 512
    rows = cvec.shape[0]
    return pl.pallas_call(
        _ada_kernel,
        grid=(depth, n // tn),
        in_specs=[
            pl.BlockSpec((rows, d), lambda l, j: (0, 0)),
            pl.BlockSpec((None, d, tn), lambda l, j: (l, 0, j)),
            pl.BlockSpec((None, 1, tn), lambda l, j: (l, 0, j)),
        ],
        out_specs=pl.BlockSpec((None, rows, tn), lambda l, j: (l, 0, j)),
        out_shape=jax.ShapeDtypeStruct((depth, rows, n), F32),
        compiler_params=_params(("parallel", "parallel")),
        name="ada_mods",
    )(cvec, ada_w, ada_b.reshape(depth, 1, n))


def _head_rms(t, g_mat, gain):
    hi, lo = _split_bf16(t * t)
    m = (jnp.dot(hi, g_mat, preferred_element_type=F32)
         + jnp.dot(lo, g_mat, preferred_element_type=F32))
    return t * lax.rsqrt(m + NORM_EPS) * gain


def _rope(t, cos_t, sin_t):
    width = t.shape[-1]
    reps = width // LANES
    cos_w = jnp.concatenate([cos_t] * reps, axis=1)
    sin_w = jnp.concatenate([sin_t] * reps, axis=1)
    lane = lax.broadcasted_iota(jnp.int32, t.shape, 1)
    quarter = HEAD_DIM // 4
    first = (lane % (2 * quarter)) < quarter
    partner = jnp.where(first, pltpu.roll(t, width - quarter, 1), pltpu.roll(t, quarter, 1))
    return t * cos_w + partner * sin_w


def _inproj_kernel(lat_ref, ctx_ref, *refs):
    mod_refs = refs[:BATCH_STEP]
    n1_ref, w_ref, cos_ref, sin_ref, gq_ref, gk_ref, gm_ref = refs[BATCH_STEP:BATCH_STEP + 7]
    q_ref, k_ref, v_ref, u_ref, gg_ref = refs[BATCH_STEP + 7:]
    kd = 2 * KV_WIDTH
    o_k, o_v, o_u, o_g = ATTN_WIDTH, ATTN_WIDTH + kd, ATTN_WIDTH + 2 * kd, ATTN_WIDTH + 2 * kd + LRU_WIDTH
    cos_t = cos_ref[...]
    sin_t = sin_ref[...]
    g_mat = gm_ref[...]
    is_ctx = pl.program_id(1) == 0
    for j, mod_ref in enumerate(mod_refs):
        x = jnp.where(is_ctx, ctx_ref[j], lat_ref[j])
        h = _rms(x) * n1_ref[...]
        h = h * (1.0 + mod_ref[1:2, :]) + mod_ref[0:1, :]
        y = jnp.dot(h.astype(BF16), w_ref[...], preferred_element_type=F32)
        q = _rope(_head_rms(y[:, :o_k], g_mat, gq_ref[...]), cos_t, sin_t)
        q_ref[j] = (q * (ATTN_SCALE * LOG2_E)).astype(BF16)
        k = _rope(_head_rms(y[:, o_k:o_v], g_mat[:kd, :kd], gk_ref[...]), cos_t, sin_t)
        k_ref[j] = k.astype(BF16)
        vt = y[:, o_v:o_u].T
        sub = lax.broadcasted_iota(jnp.int32, vt.shape, 0)
        v_ref[j] = jnp.where(sub % LANES < HEAD_DIM, vt, 1.0).astype(BF16)
        u_ref[j] = y[:, o_u:o_g]
        gb = y[:, o_g:]
        gg_ref[j] = 0.5 * gb * (1.0 + jnp.tanh(0.7978845608028654 * (gb + 0.044715 * gb * gb * gb)))


def _seq_specs(lat_off, d):
    lat = pl.BlockSpec((BATCH_STEP, ROW_TILE, d), lambda i, t: (i, jnp.maximum(t - lat_off, 0), 0))
    ctx = pl.BlockSpec((BATCH_STEP, ROW_TILE, d), lambda i, t: (i, 0, 0))
    return lat, ctx


def _mod_specs(n_batch, d, tile_of=lambda t: t):
    return [pl.BlockSpec((None, N_MOD, d),
                         lambda i, t, *_, j=j: (jnp.where(tile_of(t) == 0, n_batch, BATCH_STEP * i + j), 0, 0))
            for j in range(BATCH_STEP)]


def _inproj(x_lat, x_ctx, lat_off, mods, n1g, w_ext, cos_t, sin_t, gq, gk, g_mat):
    b, _, d = x_lat.shape
    l = x_lat.shape[1] + lat_off * x_ctx.shape[1]
    nt = l // ROW_TILE
    n_ext = w_ext.shape[1]
    kd = 2 * KV_WIDTH
    row = lambda w: pl.BlockSpec((BATCH_STEP, ROW_TILE, w), lambda i, t: (i, t, 0))
    const = lambda shape: pl.BlockSpec(shape, lambda i, t: (0,) * len(shape))
    return pl.pallas_call(
        _inproj_kernel,
        grid=(b // BATCH_STEP, nt),
        in_specs=[
            *_seq_specs(lat_off, d),
            *_mod_specs(b, d),
            const((1, d)),
            const((d, n_ext)),
            pl.BlockSpec((ROW_TILE, LANES), lambda i, t: (t, 0)),
            pl.BlockSpec((ROW_TILE, LANES), lambda i, t: (t, 0)),
            const((1, ATTN_WIDTH)),
            const((1, kd)),
            const((ATTN_WIDTH, ATTN_WIDTH)),
        ],
        out_specs=[row(ATTN_WIDTH), row(kd), pl.BlockSpec((BATCH_STEP, kd, ROW_TILE), lambda i, t: (i, 0, t)),
                   row(LRU_WIDTH), row(LRU_WIDTH)],
        out_shape=[
            jax.ShapeDtypeStruct((b, l, ATTN_WIDTH), BF16),
            jax.ShapeDtypeStruct((b, l, kd), BF16),
            jax.ShapeDtypeStruct((b, kd, l), BF16),
            jax.ShapeDtypeStruct((b, l, LRU_WIDTH), F32),
            jax.ShapeDtypeStruct((b, l, LRU_WIDTH), F32),
        ],
        compiler_params=_params(("parallel", "parallel")),
        name="in_proj",
    )(x_lat, x_ctx, *([mods] * BATCH_STEP), n1g, w_ext, cos_t, sin_t, gq, gk, g_mat)


def _attn_kernel(q_ref, k_ref, v_ref, g_ref, o_ref, *, n_ctx):
    t = pl.program_id(1)
    rows = q_ref.shape[0]
    n_all = k_ref.shape[0]

    def run(n_keys):
        low = lax.broadcasted_iota(jnp.int32, (rows, LANES), 1) < HEAD_DIM
        n_pairs = N_Q_HEADS // 2
        kv_of = lambda pair: (2 * pair) // (N_Q_HEADS // N_KV_HEADS)

        def scores(pair):
            qp = q_ref[:, pair * LANES:(pair + 1) * LANES]
            zero = jnp.zeros_like(qp)
            q2 = jnp.concatenate([jnp.where(low, qp, zero), jnp.where(low, zero, qp)], axis=0)
            kd = k_ref[0:n_keys, kv_of(pair) * LANES:(kv_of(pair) + 1) * LANES]
            return lax.dot_general(kd, q2, (((1,), (1,)), ((), ())), preferred_element_type=F32)

        def softmax_pv(pair, st):
            vt = v_ref[kv_of(pair) * LANES:(kv_of(pair) + 1) * LANES, 0:n_keys]
            p = jnp.exp2(st - jnp.max(st, axis=0, keepdims=True))
            ot = jnp.dot(vt, p.astype(BF16), preferred_element_type=F32)
            on = ot[:HEAD_DIM, :] * (1.0 / ot[HEAD_DIM:HEAD_DIM + 1, :])
            return jnp.concatenate([on[:, :rows], on[:, rows:]], axis=0).T

        outs = []
        st = scores(0)
        for pair in range(n_pairs):
            st_next = scores(pair + 1) if pair + 1 < n_pairs else None
            outs.append(softmax_pv(pair, st))
            st = st_next
        a = jnp.concatenate(outs, axis=1)
        o_ref[...] = (_rms(a) * g_ref[...]).astype(BF16)

    @pl.when(t == 0)
    def _():
        run(n_ctx)

    @pl.when(t > 0)
    def _():
        run(n_all)


def _attention(q, kd, vd, gain, n_ctx):
    b, l, w = q.shape
    nt = l // ROW_TILE
    kw = kd.shape[-1]
    return pl.pallas_call(
        functools.partial(_attn_kernel, n_ctx=n_ctx),
        grid=(b, nt),
        in_specs=[
            pl.BlockSpec((None, ROW_TILE, w), lambda i, t: (i, t, 0)),
            pl.BlockSpec((None, l, kw), lambda i, t: (i, 0, 0)),
            pl.BlockSpec((None, kw, l), lambda i, t: (i, 0, 0)),
            pl.BlockSpec((1, w), lambda i, t: (0, 0)),
        ],
        out_specs=pl.BlockSpec((None, ROW_TILE, w), lambda i, t: (i, t, 0)),
        out_shape=jax.ShapeDtypeStruct((b, l, w), BF16),
        compiler_params=_params(("parallel", "parallel")),
        name="attention",
    )(q, kd, vd, gain)


def _lru_kernel(u_ref, gg_ref, cw_ref, cb_ref, wg_ref, bg_ref, lam_ref, og_ref, o_ref,
                upad, a_f, b_f, a_r, b_r, *, n_ctx):
    l, w = u_ref.shape
    pad = SUBLANES
    zeros_pad = jnp.zeros((pad, w), F32)
    upad[0:pad, :] = zeros_pad
    upad[pad + l:2 * pad + l, :] = zeros_pad
    for r0 in range(0, l, ROW_TILE):
        upad[pad + r0:pad + r0 + ROW_TILE, :] = u_ref[r0:r0 + ROW_TILE, :]

    neg_lam = -lam_ref[...]
    softplus = jnp.maximum(neg_lam, 0.0) + jnp.log1p(jnp.exp(-jnp.abs(neg_lam)))
    half_decay = (-0.5 * LRU_C) * softplus
    cw = cw_ref[...]
    cb = cb_ref[...]

    for r0 in range(0, l, ROW_TILE):
        row = r0 + lax.broadcasted_iota(jnp.int32, (ROW_TILE, w), 0)
        is_lat = row >= n_ctx
        near_boundary = r0 - CONV_WIDTH < n_ctx < r0 + ROW_TILE + CONV_WIDTH
        uc = jnp.zeros((ROW_TILE, w), F32) + cb
        for j in range(CONV_WIDTH):
            off = j - CONV_LEFT
            tap = upad[pad + r0 + off:pad + r0 + off + ROW_TILE, :]
            if off != 0 and near_boundary:
                tap = jnp.where(((row + off) >= n_ctx) == is_lat, tap, 0.0)
            uc = uc + tap * cw[j:j + 1, :]
        z = jnp.dot(uc.astype(BF16), wg_ref[...], preferred_element_type=F32) + bg_ref[...]
        half_uc = 0.5 * uc
        for d, (a_ref, b_ref) in enumerate(((a_f, b_f), (a_r, b_r))):
            base = 2 * d * w
            t_r = jnp.tanh(z[:, base:base + w])
            t_i = jnp.tanh(z[:, base + w:base + 2 * w])
            log_a = t_r * half_decay[d:d + 1, :] + half_decay[d:d + 1, :]
            a = jnp.exp(log_a)
            one_minus_a2 = -jnp.tanh(log_a) * (a * a + 1.0)
            bb = jnp.sqrt(one_minus_a2) * ((t_i + 1.0) * half_uc)
            a_ref[r0:r0 + ROW_TILE, :] = a
            b_ref[r0:r0 + ROW_TILE, :] = bb

    sub = lax.broadcasted_iota(jnp.int32, (SUBLANES, w), 0)

    def tile_scan(a_ref, b_ref, blk, carry, reverse):
        rows = pl.ds(pl.multiple_of(blk * SUBLANES, SUBLANES), SUBLANES)
        a = a_ref[rows, :]
        b = b_ref[rows, :]
        s = 1
        while s < SUBLANES:
            keep = (sub < SUBLANES - s) if reverse else (sub >= s)
            shift = SUBLANES - s if reverse else s
            b = b + a * jnp.where(keep, pltpu.roll(b, shift, 0), 0.0)
            a = a * jnp.where(keep, pltpu.roll(a, shift, 0), 1.0)
            s *= 2
        h = b + a * carry
        b_ref[rows, :] = h
        last = h[0:1, :] if reverse else h[SUBLANES - 1:SUBLANES, :]
        return jnp.broadcast_to(last, h.shape)

    n_blk = l // SUBLANES
    c_blk = n_ctx // SUBLANES

    def ctx_body(i, carry):
        cf, cr = carry
        return tile_scan(a_f, b_f, i, cf, False), tile_scan(a_r, b_r, c_blk - 1 - i, cr, True)

    def lat_body(i, carry):
        cf, cr = carry
        return tile_scan(a_f, b_f, i, cf, False), tile_scan(a_r, b_r, n_blk + c_blk - 1 - i, cr, True)

    zero = jnp.zeros((SUBLANES, w), F32)
    carry = lax.fori_loop(0, c_blk, ctx_body, (zero, zero), unroll=2)
    lax.fori_loop(c_blk, n_blk, lat_body, carry, unroll=2)

    gain = og_ref[...]
    for r0 in range(0, l, ROW_TILE):
        rows = slice(r0, r0 + ROW_TILE)
        h = b_f[rows, :] + b_r[rows, :]
        o_ref[rows, :] = (_rms(h * gg_ref[rows, :]) * gain).astype(BF16)


def _lru(u, gg, conv_w, conv_b, w_gate, b_gate, lam, out_g, n_ctx):
    b, l, w = u.shape
    const = lambda shape: pl.BlockSpec(shape, lambda i: (0,) * len(shape))
    seq = pl.BlockSpec((None, l, w), lambda i: (i, 0, 0))
    return pl.pallas_call(
        functools.partial(_lru_kernel, n_ctx=n_ctx),
        grid=(b,),
        in_specs=[seq, seq, const(conv_w.shape), const((1, w)), const(w_gate.shape),
                  const(b_gate.shape), const(lam.shape), const((1, w))],
        out_specs=seq,
        out_shape=jax.ShapeDtypeStruct((b, l, w), BF16),
        scratch_shapes=[pltpu.VMEM((l + 2 * SUBLANES, w), F32)] + [pltpu.VMEM((l, w), F32)] * 4,
        compiler_params=_params(("parallel",)),
        name="rg_lru",
    )(u, gg, conv_w, conv_b, w_gate, b_gate, lam, out_g)


def _route(scores, biased):
    group_score = []
    for g in range(N_GROUPS):
        v = biased[g * GROUP_SIZE:(g + 1) * GROUP_SIZE]
        best = None
        for i in range(GROUP_SIZE):
            for j in range(i + 1, GROUP_SIZE):
                pair = v[i] + v[j]
                best = pair if best is None else jnp.maximum(best, pair)
        group_score.append(best)
    gid = jnp.zeros_like(group_score[0], dtype=jnp.int32)
    gbest = group_score[0]
    for g in range(1, N_GROUPS):
        upd = group_score[g] > gbest
        gid = jnp.where(upd, g, gid)
        gbest = jnp.where(upd, group_score[g], gbest)

    def pick(rows, j):
        out = rows[j]
        for g in range(1, N_GROUPS):
            out = jnp.where(gid == g, rows[g * GROUP_SIZE + j], out)
        return out

    v = [pick(biased, j) for j in range(GROUP_SIZE)]
    s = [pick(scores, j) for j in range(GROUP_SIZE)]
    i1 = jnp.zeros_like(gid)
    m1 = v[0]
    for j in range(1, GROUP_SIZE):
        upd = v[j] > m1
        i1 = jnp.where(upd, j, i1)
        m1 = jnp.where(upd, v[j], m1)
    i2 = jnp.zeros_like(gid)
    m2 = jnp.full_like(m1, -jnp.inf)
    for j in range(GROUP_SIZE):
        upd = (i1 != j) & (v[j] > m2)
        i2 = jnp.where(upd, j, i2)
        m2 = jnp.where(upd, v[j], m2)
    w1 = s[0]
    w2 = s[0]
    for j in range(1, GROUP_SIZE):
        w1 = jnp.where(i1 == j, s[j], w1)
        w2 = jnp.where(i2 == j, s[j], w2)
    den = w1 + w2
    w1 = w1 / den
    w2 = w2 / den
    lo = jnp.minimum(i1, i2)
    hi = jnp.maximum(i1, i2)
    pair = hi - lo - 1
    for k in range(1, GROUP_SIZE - 1):
        pair = pair + jnp.where(lo >= k, GROUP_SIZE - k, 0)
    first_is_lo = i1 < i2
    return gid * N_PAIRS + pair, jnp.where(first_is_lo, w1, w2), jnp.where(first_is_lo, w2, w1)


def _outproj_kernel(lat_ref, ctx_ref, a_ref, r_ref, *refs, skip):
    mod_refs = refs[:BATCH_STEP]
    w_ref, n2_ref, rwh_ref, rwl_ref, rb_ref, tri_ref = refs[BATCH_STEP:BATCH_STEP + 6]
    x1_ref, h2_ref, route_ref, counts_ref, cnt_ref = refs[BATCH_STEP + 6:]
    tile_rows = lat_ref.shape[1]
    rows = BATCH_STEP * tile_rows
    is_ctx = pl.program_id(1) + skip == 0
    half = a_ref.shape[-1]
    h2_parts = []
    for j, mod_ref in enumerate(mod_refs):
        x = jnp.where(is_ctx, ctx_ref[j], lat_ref[j])
        mix = (jnp.dot(a_ref[j], w_ref[0:half, :], preferred_element_type=F32)
               + jnp.dot(r_ref[j], w_ref[half:, :], preferred_element_type=F32))
        x1 = x + mod_ref[2:3, :] * mix
        x1_ref[j] = x1
        h2 = _rms(x1) * n2_ref[...]
        h2 = h2 * (1.0 + mod_ref[4:5, :]) + mod_ref[3:4, :]
        h2_ref[j] = h2
        h2_parts.append(h2)
    h2 = jnp.concatenate(h2_parts, axis=0)
    hi, lo = _split_bf16(h2)
    nt = (((1,), (1,)), ((), ()))
    logits = (lax.dot_general(rwh_ref[...], hi, nt, preferred_element_type=F32)
              + lax.dot_general(rwh_ref[...], lo, nt, preferred_element_type=F32)
              + lax.dot_general(rwl_ref[...], hi, nt, preferred_element_type=F32))
    scores = _sigmoid(logits)
    biased = scores + rb_ref[...]
    bucket, w_lo, w_hi = _route([scores[e:e + 1, :] for e in range(N_EXPERTS)],
                                [biased[e:e + 1, :] for e in range(N_EXPERTS)])

    @pl.when((pl.program_id(0) == 0) & (pl.program_id(1) == 0))
    def _():
        cnt_ref[...] = jnp.zeros_like(cnt_ref)

    sub = lax.broadcasted_iota(jnp.int32, (BUCKET_ROWS, rows), 0)
    onehot = jnp.where(sub == bucket, 1.0, 0.0)
    before = jnp.dot(onehot.astype(BF16), tri_ref[...], preferred_element_type=F32) + cnt_ref[...]
    rank = jnp.sum(onehot * before, axis=0, keepdims=True)
    total = cnt_ref[...] + jnp.sum(onehot, axis=1, keepdims=True)
    cnt_ref[...] = total
    counts_ref[...] = total

    for j in range(BATCH_STEP):
        cols = slice(j * tile_rows, (j + 1) * tile_rows)
        route_ref[j, 0:1, :] = bucket.astype(F32)[:, cols]
        route_ref[j, 1:2, :] = rank[:, cols]
        route_ref[j, 2:3, :] = w_lo[:, cols]
        route_ref[j, 3:4, :] = w_hi[:, cols]
        route_ref[j, 4:, :] = jnp.zeros((ROUTE_ROWS - 4, tile_rows), F32)


def _outproj(x_lat, x_ctx, lat_off, attn_n, rec_n, mods, w_out, n2g, rw_hi, rw_lo, rb, tri, skip):
    b, l, _ = attn_n.shape
    d = x_lat.shape[-1]
    nt = l // ROW_TILE - skip
    half = attn_n.shape[-1]
    row = lambda w: pl.BlockSpec((BATCH_STEP, ROW_TILE, w), lambda i, t: (i, t + skip, 0))
    out_row = pl.BlockSpec((BATCH_STEP, ROW_TILE, d), lambda i, t: (i, t, 0))
    const = lambda shape: pl.BlockSpec(shape, lambda i, t: (0,) * len(shape))
    step_rows = BATCH_STEP * ROW_TILE
    return pl.pallas_call(
        functools.partial(_outproj_kernel, skip=skip),
        grid=(b // BATCH_STEP, nt),
        in_specs=[
            pl.BlockSpec((BATCH_STEP, ROW_TILE, d), lambda i, t: (i, jnp.maximum(t + skip - lat_off, 0), 0)),
            pl.BlockSpec((BATCH_STEP, ROW_TILE, d), lambda i, t: (i, 0, 0)),
            row(half), row(half),
            *_mod_specs(b, d, lambda t: t + skip),
            const(w_out.shape), const((1, d)), const(rw_hi.shape), const(rw_lo.shape), const(rb.shape),
            const(tri.shape),
        ],
        out_specs=[out_row, out_row,
                   pl.BlockSpec((BATCH_STEP, None, ROUTE_ROWS, ROW_TILE), lambda i, t: (i, t, 0, 0)),
                   const((BUCKET_ROWS, step_rows))],
        out_shape=[
            jax.ShapeDtypeStruct((b, nt * ROW_TILE, d), F32),
            jax.ShapeDtypeStruct((b, nt * ROW_TILE, d), F32),
            jax.ShapeDtypeStruct((b, nt, ROUTE_ROWS, ROW_TILE), F32),
            jax.ShapeDtypeStruct((BUCKET_ROWS, step_rows), F32),
        ],
        scratch_shapes=[pltpu.VMEM((BUCKET_ROWS, step_rows), F32)],
        compiler_params=_params(("arbitrary", "arbitrary")),
        name="out_proj_router",
    )(x_lat, x_ctx, attn_n, rec_n, *([mods] * BATCH_STEP), w_out, n2g, rw_hi, rw_lo, rb, tri)


def _moe_plan(route, counts, n_tiles):
    tm = ROW_TILE
    bucket = route[:, :, 0, :].reshape(-1).astype(jnp.int32)
    rank = route[:, :, 1, :].reshape(-1).astype(jnp.int32)
    cnt = counts[:N_BUCKETS, 0].astype(jnp.int32)
    tiles_per = (cnt + tm - 1) // tm
    tile_end = jnp.cumsum(tiles_per)
    pos = (tile_end - tiles_per)[bucket] * tm + rank
    total = tile_end[-1]
    tile = jnp.arange(n_tiles, dtype=jnp.int32)
    last = jnp.minimum(tile, total - 1)
    tile_bucket = jnp.sum((tile_end[None, :] <= last[:, None]).astype(jnp.int32), axis=1)
    tile_bucket = jnp.minimum(tile_bucket, N_BUCKETS - 1)
    group, pair = tile_bucket // N_PAIRS, tile_bucket % N_PAIRS
    pairs = [(i, j) for i in range(GROUP_SIZE) for j in range(i + 1, GROUP_SIZE)]
    lo = jnp.array([p[0] for p in pairs], jnp.int32)[pair]
    hi = jnp.array([p[1] for p in pairs], jnp.int32)[pair]
    weights = route[:, :, 2:4, :].transpose(0, 1, 3, 2).reshape(-1, 2)
    w_aug = jnp.pad(weights, ((0, 0), (0, LANES - 2)))
    tile_ea = group * GROUP_SIZE + lo
    tile_eb = group * GROUP_SIZE + hi
    changed = lambda e: jnp.concatenate([jnp.ones((1,), jnp.int32), (e[1:] != e[:-1]).astype(jnp.int32)])
    return (pos, tile_ea, tile_eb, changed(tile_ea), changed(tile_eb), (tile < total).astype(jnp.int32), w_aug)


def _token_base(i, t, j, nt, rows):
    return ((BATCH_STEP * i + j) * nt + t) * rows


def _dispatch_kernel(pos_ref, h_ref, *refs):
    w_refs = refs[:BATCH_STEP]
    xs_ref, stage, sem = refs[BATCH_STEP + 1:]
    _, rows, d = h_ref.shape
    nt = pl.num_programs(1)
    step = pl.program_id(0) * nt + pl.program_id(1)
    n = pl.num_programs(0) * nt
    slot = step % 2

    def drain(s):
        pltpu.make_async_copy(stage.at[s], xs_ref.at[pl.ds(0, BATCH_STEP * rows), :], sem.at[s]).wait()

    @pl.when(step >= 2)
    def _():
        drain(slot)

    for j in range(BATCH_STEP):
        stage[slot, j * rows:(j + 1) * rows, 0:d] = h_ref[j]
        stage[slot, j * rows:(j + 1) * rows, d:] = w_refs[j][...]
        base = _token_base(pl.program_id(0), pl.program_id(1), j, nt, rows)

        def issue(g, c, j=j, base=base):
            r0 = pl.multiple_of(g * SUBLANES, SUBLANES)
            for k in range(SUBLANES):
                p = pos_ref[base + r0 + k]
                pltpu.make_async_copy(stage.at[slot, pl.ds(j * rows + r0 + k, 1), :],
                                      xs_ref.at[pl.ds(p, 1), :], sem.at[slot]).start()
            return c

        lax.fori_loop(0, rows // SUBLANES, issue, 0)

    @pl.when(step == n - 1)
    def _():
        drain(1 - slot)
        drain(slot)


def _dispatch(pos, h2, w_aug, xs0):
    b, l, d = h2.shape
    nt = l // ROW_TILE
    assert (b // BATCH_STEP) * nt >= 2
    n_rows = xs0.shape[0]
    w_spec = lambda j: pl.BlockSpec((ROW_TILE, LANES), lambda i, t, pos: ((BATCH_STEP * i + j) * nt + t, 0))
    return pl.pallas_call(
        _dispatch_kernel,
        grid_spec=pltpu.PrefetchScalarGridSpec(
            num_scalar_prefetch=1,
            grid=(b // BATCH_STEP, nt),
            in_specs=[
                pl.BlockSpec((BATCH_STEP, ROW_TILE, d), lambda i, t, pos: (i, t, 0)),
                *[w_spec(j) for j in range(BATCH_STEP)],
                pl.BlockSpec(memory_space=pl.ANY),
            ],
            out_specs=pl.BlockSpec(memory_space=pl.ANY),
            scratch_shapes=[pltpu.VMEM((2, BATCH_STEP * ROW_TILE, d + LANES), F32),
                            pltpu.SemaphoreType.DMA((2,))],
        ),
        out_shape=jax.ShapeDtypeStruct((n_rows, d + LANES), F32),
        input_output_aliases={2 + BATCH_STEP: 0},
        compiler_params=_params(("arbitrary", "arbitrary")),
        name="moe_dispatch",
    )(pos, h2, *([w_aug] * BATCH_STEP), xs0)


def _moe_pair_kernel(ea_ref, eb_ref, new_a_ref, new_b_ref, valid_ref, xs_ref,
                     gate_a, up_a, down_a, gate_b, up_b, down_b, y_ref, wgu_a, wd_a, wgu_b, wd_b):
    del ea_ref, eb_ref
    i = pl.program_id(0)
    d = y_ref.shape[1]
    d_exp = wd_a.shape[0]

    def load_expert(gate, up, down, wgu, wd):
        wgu[:, :d_exp] = gate[...].astype(BF16)
        wgu[:, d_exp:] = up[...].astype(BF16)
        wd[...] = down[...].astype(BF16)

    @pl.when(new_a_ref[i] != 0)
    def _():
        load_expert(gate_a, up_a, down_a, wgu_a, wd_a)

    @pl.when(new_b_ref[i] != 0)
    def _():
        load_expert(gate_b, up_b, down_b, wgu_b, wd_b)

    @pl.when(valid_ref[i] != 0)
    def _():
        xs = xs_ref[...]
        x = xs[:, :d].astype(BF16)

        def down(gu, wd):
            gate = gu[:, :d_exp]
            hid = gate * _sigmoid(gate) * gu[:, d_exp:]
            return jnp.dot(hid.astype(BF16), wd[...], preferred_element_type=F32)

        gu_a = jnp.dot(x, wgu_a[...], preferred_element_type=F32)
        gu_b = jnp.dot(x, wgu_b[...], preferred_element_type=F32)
        y_ref[...] = xs[:, d:d + 1] * down(gu_a, wd_a) + xs[:, d + 1:d + 2] * down(gu_b, wd_b)

    @pl.when(valid_ref[i] == 0)
    def _():
        y_ref[...] = jnp.zeros_like(y_ref)


def _moe_pairs(xs, tile_ea, tile_eb, new_a, new_b, tile_valid, w_gate, w_up, w_down, layer):
    n_rows, da = xs.shape
    d = da - LANES
    d_exp = w_gate.shape[-1]
    tm = ROW_TILE
    w_in = lambda pick: pl.BlockSpec((None, None, d, d_exp), lambda i, ea, eb, *_: (layer, pick(ea, eb)[i], 0, 0))
    w_out = lambda pick: pl.BlockSpec((None, None, d_exp, d), lambda i, ea, eb, *_: (layer, pick(ea, eb)[i], 0, 0))
    first = lambda ea, eb: ea
    second = lambda ea, eb: eb
    return pl.pallas_call(
        _moe_pair_kernel,
        grid_spec=pltpu.PrefetchScalarGridSpec(
            num_scalar_prefetch=5,
            grid=(n_rows // tm,),
            in_specs=[
                pl.BlockSpec((tm, da), lambda i, *_: (i, 0)),
                w_in(first), w_in(first), w_out(first), w_in(second), w_in(second), w_out(second),
            ],
            out_specs=pl.BlockSpec((tm, d), lambda i, *_: (i, 0)),
            scratch_shapes=[pltpu.VMEM((d, 2 * d_exp), BF16), pltpu.VMEM((d_exp, d), BF16)] * 2,
        ),
        out_shape=jax.ShapeDtypeStruct((n_rows, d), F32),
        compiler_params=_params(("arbitrary",)),
        name="moe_pairs",
    )(tile_ea, tile_eb, new_a, new_b, tile_valid, xs, w_gate, w_up, w_down, w_gate, w_up, w_down)


def _combine_kernel(pos_ref, x_ref, *refs, final):
    mod_refs = refs[:BATCH_STEP]
    g_ref, ys_ref, o_ref, ybuf, sem = refs[BATCH_STEP:]
    _, rows, _ = x_ref.shape
    nt = pl.num_programs(1)
    step = pl.program_id(0) * nt + pl.program_id(1)
    n = pl.num_programs(0) * nt
    slot = step % 2

    def gather(at_step, s):
        i, t = at_step // nt, at_step % nt
        for j in range(BATCH_STEP):
            base = _token_base(i, t, j, nt, rows)

            def issue(g, c, j=j, base=base):
                r0 = pl.multiple_of(g * SUBLANES, SUBLANES)
                for k in range(SUBLANES):
                    p = pos_ref[base + r0 + k]
                    pltpu.make_async_copy(ys_ref.at[pl.ds(p, 1), :],
                                          ybuf.at[s, pl.ds(j * rows + r0 + k, 1), :], sem.at[s]).start()
                return c

            lax.fori_loop(0, rows // SUBLANES, issue, 0)

    @pl.when(step == 0)
    def _():
        gather(0, 0)

    @pl.when(step + 1 < n)
    def _():
        gather(step + 1, 1 - slot)

    pltpu.make_async_copy(ys_ref.at[pl.ds(0, BATCH_STEP * rows), :], ybuf.at[slot], sem.at[slot]).wait()
    for j, mod_ref in enumerate(mod_refs):
        x2 = x_ref[j] + mod_ref[5:6, :] * ybuf[slot, j * rows:(j + 1) * rows, :]
        o_ref[j] = _rms(x2) * g_ref[...] if final else x2


def _combine(pos, x1, mods, gain, ys, has_ctx, final):
    b, l, d = x1.shape
    nt = l // ROW_TILE
    row = pl.BlockSpec((BATCH_STEP, ROW_TILE, d), lambda i, t, pos: (i, t, 0))
    return pl.pallas_call(
        functools.partial(_combine_kernel, final=final),
        grid_spec=pltpu.PrefetchScalarGridSpec(
            num_scalar_prefetch=1,
            grid=(b // BATCH_STEP, nt),
            in_specs=[
                row,
                *_mod_specs(b, d, (lambda t: t) if has_ctx else (lambda t: t + 1)),
                pl.BlockSpec((1, d), lambda i, t, pos: (0, 0)),
                pl.BlockSpec(memory_space=pl.ANY),
            ],
            out_specs=row,
            scratch_shapes=[pltpu.VMEM((2, BATCH_STEP * ROW_TILE, d), F32), pltpu.SemaphoreType.DMA((2,))],
        ),
        out_shape=jax.ShapeDtypeStruct((b, l, d), F32),
        compiler_params=_params(("arbitrary", "arbitrary")),
        name="moe_combine",
    )(pos, x1, *([mods] * BATCH_STEP), gain, ys)


def _rope_tables(n_ctx, n_lat):
    rows = n_lat // GRID_W
    r, col = jnp.meshgrid(jnp.arange(rows), jnp.arange(GRID_W), indexing="ij")
    r = r.reshape(-1).astype(F32)
    col = col.reshape(-1).astype(F32)
    half = HEAD_DIM // 2
    inv = ROPE_THETA ** (-jnp.arange(0, half, 2, dtype=F32) / half)
    ang_r = r[:, None] * inv
    ang_c = col[:, None] * inv
    cos_h = jnp.concatenate([jnp.cos(ang_r)] * 2 + [jnp.cos(ang_c)] * 2, axis=-1)
    sin_h = jnp.concatenate([-jnp.sin(ang_r), jnp.sin(ang_r), -jnp.sin(ang_c), jnp.sin(ang_c)], axis=-1)
    cos_t = jnp.concatenate([jnp.ones((n_ctx, HEAD_DIM), F32), cos_h], axis=0)
    sin_t = jnp.concatenate([jnp.zeros((n_ctx, HEAD_DIM), F32), sin_h], axis=0)
    reps = LANES // HEAD_DIM
    return jnp.tile(cos_t, (1, reps)), jnp.tile(sin_t, (1, reps))


def _block_diag(w):
    n, d, e = w.shape
    eye = jnp.eye(n, dtype=w.dtype)
    return (w[:, :, None, :] * eye[:, None, :, None]).reshape(n * d, n * e)


def kernel(x, c, ctx, c_ctx, ada_w, ada_b, norm1_g, w_in, q_norm_g, k_norm_g, conv_w, conv_b, lru_wa, lru_ba, lru_wx, lru_bx, lru_lambda, attn_out_g, lru_out_g, w_out, norm2_g, router_w, router_b, exp_w_gate, exp_w_up, exp_w_down, final_g):
    b, s, d = x.shape
    n_ctx = ctx.shape[1]
    l = n_ctx + s
    depth = ada_w.shape[0]
    nt = l // ROW_TILE
    assert n_ctx == ROW_TILE and s % ROW_TILE == 0 and b % BATCH_STEP == 0

    x_lat, x_ctx, lat_off = x, ctx, 1
    cvec = jnp.zeros((2 * SUBLANES, d), F32).at[:b].set(c).at[b].set(c_ctx)
    mods_all = _ada_mods(cvec, ada_w, ada_b)[:, :b + 1].reshape(depth, b + 1, N_MOD, d)
    cos_t, sin_t = _rope_tables(n_ctx, s)

    head_avg = _block_diag(jnp.full((N_Q_HEADS, HEAD_DIM, HEAD_DIM), 1.0 / HEAD_DIM, F32)).astype(BF16)
    rw_t = router_w.T
    rw_hi = rw_t.astype(BF16)
    rw_lo = (rw_t - rw_hi.astype(F32)).astype(BF16)
    step_rows = BATCH_STEP * ROW_TILE
    rb = jnp.broadcast_to(router_b[:, None], (N_EXPERTS, step_rows)).astype(F32)
    tri = jnp.triu(jnp.ones((step_rows, step_rows), BF16), 1)

    n_tiles = b * nt + N_BUCKETS
    xs = jnp.zeros((n_tiles * ROW_TILE, d + LANES), F32)

    qw, kw = ATTN_WIDTH, KV_WIDTH
    out = None
    for li in range(depth):
        wi = w_in[li]
        k0, k1 = wi[:, qw:qw + HEAD_DIM], wi[:, qw + HEAD_DIM:qw + kw]
        v0, v1 = wi[:, qw + kw:qw + kw + HEAD_DIM], wi[:, qw + kw + HEAD_DIM:qw + 2 * kw]
        w_ext = jnp.concatenate([wi[:, :qw], k0, k0, k1, k1, v0, v0, v1, v1, wi[:, qw + 2 * kw:]],
                                axis=1).astype(BF16)
        gq = jnp.tile(q_norm_g[li], N_Q_HEADS)[None, :]
        gk = jnp.tile(k_norm_g[li], 2 * N_KV_HEADS)[None, :]
        mods = mods_all[li]
        q, kd, vd, u, gg = _inproj(x_lat, x_ctx, lat_off, mods, norm1_g[li][None, :], w_ext, cos_t, sin_t,
                                   gq, gk, head_avg)
        attn_n = _attention(q, kd, vd, attn_out_g[li][None, :], n_ctx)

        w_gate = jnp.concatenate([_block_diag(lru_wa[li, 0]), _block_diag(lru_wx[li, 0]),
                                  _block_diag(lru_wa[li, 1]), _block_diag(lru_wx[li, 1])], axis=1)
        w_gate = (0.5 * w_gate).astype(BF16)
        b_gate = 0.5 * jnp.concatenate([lru_ba[li, 0], lru_bx[li, 0], lru_ba[li, 1], lru_bx[li, 1]])[None, :]
        rec_n = _lru(u, gg, conv_w[li], conv_b[li][None, :], w_gate, b_gate, lru_lambda[li],
                     lru_out_g[li][None, :], n_ctx)

        skip = n_ctx // ROW_TILE if li == depth - 1 else 0
        x1, h2, route, counts = _outproj(x_lat, x_ctx, lat_off, attn_n, rec_n, mods, w_out[li].astype(BF16),
                                         norm2_g[li][None, :], rw_hi, rw_lo, rb, tri, skip)
        pos, tile_ea, tile_eb, new_a, new_b, tile_valid, w_aug = _moe_plan(route, counts, n_tiles)
        xs = _dispatch(pos, h2, w_aug, xs)
        ys = _moe_pairs(xs, tile_ea, tile_eb, new_a, new_b, tile_valid, exp_w_gate, exp_w_up, exp_w_down, li)
        if li == depth - 1:
            out = _combine(pos, x1, mods, final_g[None, :], ys, skip == 0, True)
        else:
            x_lat = x_ctx = _combine(pos, x1, mods, final_g[None, :], ys, skip == 0, False)
            lat_off = 0
    return out
```

```python
import functools

import jax
import jax.numpy as jnp
from jax import lax
from jax.experimental import pallas as pl
from jax.experimental.pallas import tpu as pltpu

F32 = jnp.float32
BF16 = jnp.bfloat16

HEAD_DIM = 64
N_Q_HEADS = 8
N_KV_HEADS = 2
ATTN_WIDTH = N_Q_HEADS * HEAD_DIM
KV_WIDTH = N_KV_HEADS * HEAD_DIM
LRU_WIDTH = 512
LRU_BLOCKS = 8
CONV_WIDTH = 4
CONV_LEFT = CONV_WIDTH // 2
LRU_C = 8.0
GRID_W = 64
ROPE_THETA = 10000.0
N_EXPERTS = 16
N_GROUPS = 4
GROUP_SIZE = N_EXPERTS // N_GROUPS
N_MOD = 6
NORM_EPS = 1e-6
ATTN_SCALE = HEAD_DIM ** -0.5
LOG2_E = 1.4426950408889634

LANES = 128
SUBLANES = 8
ROW_TILE = 256
BATCH_STEP = 2
N_PAIRS = GROUP_SIZE * (GROUP_SIZE - 1) // 2
N_BUCKETS = N_GROUPS * N_PAIRS
BUCKET_ROWS = 32
ROUTE_ROWS = 8
VMEM_LIMIT = 60000 * 1024


def _params(sem):
    return pltpu.CompilerParams(dimension_semantics=sem, vmem_limit_bytes=VMEM_LIMIT)


def _sigmoid(x):
    return 1.0 / (1.0 + jnp.exp(-x))


def _rms(x):
    return x * lax.rsqrt(jnp.mean(x * x, axis=-1, keepdims=True) + NORM_EPS)


def _split_bf16(x):
    hi = x.astype(BF16)
    lo = (x - hi.astype(F32)).astype(BF16)
    return hi, lo


def _ada_kernel(c_ref, w_ref, b_ref, o_ref):
    cv = c_ref[...]
    s = cv * _sigmoid(cv)
    o_ref[...] = jnp.dot(s.astype(BF16), w_ref[...].astype(BF16),
                         preferred_element_type=F32) + b_ref[...]


def _ada_mods(cvec, ada_w, ada_b):
    depth, d, n = ada_w.shape
    tn = 512
    rows = cvec.shape[0]
    return pl.pallas_call(
        _ada_kernel,
        grid=(depth, n // tn),
        in_specs=[
            pl.BlockSpec((rows, d), lambda l, j: (0, 0)),
            pl.BlockSpec((None, d, tn), lambda l, j: (l, 0, j)),
            pl.BlockSpec((None, 1, tn), lambda l, j: (l, 0, j)),
        ],
        out_specs=pl.BlockSpec((None, rows, tn), lambda l, j: (l, 0, j)),
        out_shape=jax.ShapeDtypeStruct((depth, rows, n), F32),
        compiler_params=_params(("parallel", "parallel")),
        name="ada_mods",
    )(cvec, ada_w, ada_b.reshape(depth, 1, n))


def _head_rms(t, g_mat, gain):
    hi, lo = _split_bf16(t * t)
    m = (jnp.dot(hi, g_mat, preferred_element_type=F32)
         + jnp.dot(lo, g_mat, preferred_element_type=F32))
    return t * lax.rsqrt(m + NORM_EPS) * gain


def _rope(t, cos_t, sin_t):
    width = t.shape[-1]
    reps = width // LANES
    cos_w = jnp.concatenate([cos_t] * reps, axis=1)
    sin_w = jnp.concatenate([sin_t] * reps, axis=1)
    lane = lax.broadcasted_iota(jnp.int32, t.shape, 1)
    quarter = HEAD_DIM // 4
    first = (lane % (2 * quarter)) < quarter
    partner = jnp.where(first, pltpu.roll(t, width - quarter, 1), pltpu.roll(t, quarter, 1))
    return t * cos_w + partner * sin_w


def _inproj_kernel(lat_ref, ctx_ref, *refs):
    mod_refs = refs[:BATCH_STEP]
    n1_ref, w_ref, cos_ref, sin_ref, gq_ref, gk_ref, gm_ref = refs[BATCH_STEP:BATCH_STEP + 7]
    q_ref, k_ref, v_ref, u_ref, gg_ref = refs[BATCH_STEP + 7:]
    kd = 2 * KV_WIDTH
    o_k, o_v, o_u, o_g = ATTN_WIDTH, ATTN_WIDTH + kd, ATTN_WIDTH + 2 * kd, ATTN_WIDTH + 2 * kd + LRU_WIDTH
    cos_t = cos_ref[...]
    sin_t = sin_ref[...]
    g_mat = gm_ref[...]
    is_ctx = pl.program_id(1) == 0
    for j, mod_ref in enumerate(mod_refs):
        x = jnp.where(is_ctx, ctx_ref[j], lat_ref[j])
        h = _rms(x) * n1_ref[...]
        h = h * (1.0 + mod_ref[1:2, :]) + mod_ref[0:1, :]
        y = jnp.dot(h.astype(BF16), w_ref[...], preferred_element_type=F32)
        q = _rope(_head_rms(y[:, :o_k], g_mat, gq_ref[...]), cos_t, sin_t)
        q_ref[j] = (q * (ATTN_SCALE * LOG2_E)).astype(BF16)
        k = _rope(_head_rms(y[:, o_k:o_v], g_mat[:kd, :kd], gk_ref[...]), cos_t, sin_t)
        k_ref[j] = k.astype(BF16)
        vt = y[:, o_v:o_u].T
        sub = lax.broadcasted_iota(jnp.int32, vt.shape, 0)
        v_ref[j] = jnp.where(sub % LANES < HEAD_DIM, vt, 1.0).astype(BF16)
        u_ref[j] = y[:, o_u:o_g]
        gb = y[:, o_g:]
        gg_ref[j] = 0.5 * gb * (1.0 + jnp.tanh(0.7978845608028654 * (gb + 0.044715 * gb * gb * gb)))


def _seq_specs(lat_off, d):
    lat = pl.BlockSpec((BATCH_STEP, ROW_TILE, d), lambda i, t: (i, jnp.maximum(t - lat_off, 0), 0))
    ctx = pl.BlockSpec((BATCH_STEP, ROW_TILE, d), lambda i, t: (i, 0, 0))
    return lat, ctx


def _mod_specs(n_batch, d, tile_of=lambda t: t):
    return [pl.BlockSpec((None, N_MOD, d),
                         lambda i, t, *_, j=j: (jnp.where(tile_of(t) == 0, n_batch, BATCH_STEP * i + j), 0, 0))
            for j in range(BATCH_STEP)]


def _inproj(x_lat, x_ctx, lat_off, mods, n1g, w_ext, cos_t, sin_t, gq, gk, g_mat):
    b, _, d = x_lat.shape
    l = x_lat.shape[1] + lat_off * x_ctx.shape[1]
    nt = l // ROW_TILE
    n_ext = w_ext.shape[1]
    kd = 2 * KV_WIDTH
    row = lambda w: pl.BlockSpec((BATCH_STEP, ROW_TILE, w), lambda i, t: (i, t, 0))
    const = lambda shape: pl.BlockSpec(shape, lambda i, t: (0,) * len(shape))
    return pl.pallas_call(
        _inproj_kernel,
        grid=(b // BATCH_STEP, nt),
        in_specs=[
            *_seq_specs(lat_off, d),
            *_mod_specs(b, d),
            const((1, d)),
            const((d, n_ext)),
            pl.BlockSpec((ROW_TILE, LANES), lambda i, t: (t, 0)),
            pl.BlockSpec((ROW_TILE, LANES), lambda i, t: (t, 0)),
            const((1, ATTN_WIDTH)),
            const((1, kd)),
            const((ATTN_WIDTH, ATTN_WIDTH)),
        ],
        out_specs=[row(ATTN_WIDTH), row(kd), pl.BlockSpec((BATCH_STEP, kd, ROW_TILE), lambda i, t: (i, 0, t)),
                   row(LRU_WIDTH), row(LRU_WIDTH)],
        out_shape=[
            jax.ShapeDtypeStruct((b, l, ATTN_WIDTH), BF16),
            jax.ShapeDtypeStruct((b, l, kd), BF16),
            jax.ShapeDtypeStruct((b, kd, l), BF16),
            jax.ShapeDtypeStruct((b, l, LRU_WIDTH), F32),
            jax.ShapeDtypeStruct((b, l, LRU_WIDTH), F32),
        ],
        compiler_params=_params(("parallel", "parallel")),
        name="in_proj",
    )(x_lat, x_ctx, *([mods] * BATCH_STEP), n1g, w_ext, cos_t, sin_t, gq, gk, g_mat)


def _attn_kernel(q_ref, k_ref, v_ref, g_ref, o_ref, *, n_ctx, skip):
    t = pl.program_id(1) + skip
    rows = q_ref.shape[1]
    n_all = k_ref.shape[1]

    def run(n_keys):
        low = lax.broadcasted_iota(jnp.int32, (rows, LANES), 1) < HEAD_DIM
        n_pairs = N_Q_HEADS // 2
        kv_of = lambda pair: (2 * pair) // (N_Q_HEADS // N_KV_HEADS)

        def scores(j, pair):
            qp = q_ref[j, :, pair * LANES:(pair + 1) * LANES]
            zero = jnp.zeros_like(qp)
            q2 = jnp.concatenate([jnp.where(low, qp, zero), jnp.where(low, zero, qp)], axis=0)
            kd = k_ref[j, 0:n_keys, kv_of(pair) * LANES:(kv_of(pair) + 1) * LANES]
            return lax.dot_general(kd, q2, (((1,), (1,)), ((), ())), preferred_element_type=F32)

        def softmax_pv(j, pair, st):
            vt = v_ref[j, kv_of(pair) * LANES:(kv_of(pair) + 1) * LANES, 0:n_keys]
            p = jnp.exp2(st - jnp.max(st, axis=0, keepdims=True))
            ot = jnp.dot(vt, p.astype(BF16), preferred_element_type=F32)
            on = ot[:HEAD_DIM, :] * (1.0 / ot[HEAD_DIM:HEAD_DIM + 1, :])
            return jnp.concatenate([on[:, :rows], on[:, rows:]], axis=0).T

        units = [(j, pair) for pair in range(n_pairs) for j in range(BATCH_STEP)]
        outs = [[] for _ in range(BATCH_STEP)]
        st = scores(*units[0])
        for u, (j, pair) in enumerate(units):
            st_next = scores(*units[u + 1]) if u + 1 < len(units) else None
            outs[j].append(softmax_pv(j, pair, st))
            st = st_next
        for j in range(BATCH_STEP):
            a = jnp.concatenate(outs[j], axis=1)
            o_ref[j] = (_rms(a) * g_ref[...]).astype(BF16)

    @pl.when(t == 0)
    def _():
        run(n_ctx)

    @pl.when(t > 0)
    def _():
        run(n_all)


def _attention(q, kd, vd, gain, n_ctx, skip):
    b, l, w = q.shape
    nt = l // ROW_TILE - skip
    kw = kd.shape[-1]
    return pl.pallas_call(
        functools.partial(_attn_kernel, n_ctx=n_ctx, skip=skip),
        grid=(b // BATCH_STEP, nt),
        in_specs=[
            pl.BlockSpec((BATCH_STEP, ROW_TILE, w), lambda i, t: (i, t + skip, 0)),
            pl.BlockSpec((BATCH_STEP, l, kw), lambda i, t: (i, 0, 0)),
            pl.BlockSpec((BATCH_STEP, kw, l), lambda i, t: (i, 0, 0)),
            pl.BlockSpec((1, w), lambda i, t: (0, 0)),
        ],
        out_specs=pl.BlockSpec((BATCH_STEP, ROW_TILE, w), lambda i, t: (i, t, 0)),
        out_shape=jax.ShapeDtypeStruct((b, nt * ROW_TILE, w), BF16),
        compiler_params=_params(("parallel", "parallel")),
        name="attention",
    )(q, kd, vd, gain)


def _lru_kernel(u_ref, gg_ref, cw_ref, cb_ref, wg_ref, bg_ref, lam_ref, og_ref, o_ref,
                upad, a_f, b_f, a_r, b_r, *, n_ctx):
    l, w = u_ref.shape
    pad = SUBLANES
    zeros_pad = jnp.zeros((pad, w), F32)
    upad[0:pad, :] = zeros_pad
    upad[pad + l:2 * pad + l, :] = zeros_pad
    for r0 in range(0, l, ROW_TILE):
        upad[pad + r0:pad + r0 + ROW_TILE, :] = u_ref[r0:r0 + ROW_TILE, :]

    neg_lam = -lam_ref[...]
    softplus = jnp.maximum(neg_lam, 0.0) + jnp.log1p(jnp.exp(-jnp.abs(neg_lam)))
    half_decay = (-0.5 * LRU_C) * softplus
    cw = cw_ref[...]
    cb = cb_ref[...]

    for r0 in range(0, l, ROW_TILE):
        row = r0 + lax.broadcasted_iota(jnp.int32, (ROW_TILE, w), 0)
        is_lat = row >= n_ctx
        near_boundary = r0 - CONV_WIDTH < n_ctx < r0 + ROW_TILE + CONV_WIDTH
        uc = jnp.zeros((ROW_TILE, w), F32) + cb
        for j in range(CONV_WIDTH):
            off = j - CONV_LEFT
            tap = upad[pad + r0 + off:pad + r0 + off + ROW_TILE, :]
            if off != 0 and near_boundary:
                tap = jnp.where(((row + off) >= n_ctx) == is_lat, tap, 0.0)
            uc = uc + tap * cw[j:j + 1, :]
        z = jnp.dot(uc.astype(BF16), wg_ref[...], preferred_element_type=F32) + bg_ref[...]
        half_uc = 0.5 * uc
        for d, (a_ref, b_ref) in enumerate(((a_f, b_f), (a_r, b_r))):
            base = 2 * d * w
            t_r = jnp.tanh(z[:, base:base + w])
            t_i = jnp.tanh(z[:, base + w:base + 2 * w])
            log_a = t_r * half_decay[d:d + 1, :] + half_decay[d:d + 1, :]
            a = jnp.exp(log_a)
            one_minus_a2 = -jnp.tanh(log_a) * (a * a + 1.0)
            bb = jnp.sqrt(one_minus_a2) * ((t_i + 1.0) * half_uc)
            a_ref[r0:r0 + ROW_TILE, :] = a
            b_ref[r0:r0 + ROW_TILE, :] = bb

    sub = lax.broadcasted_iota(jnp.int32, (SUBLANES, w), 0)

    def tile_scan(a_ref, b_ref, blk, carry, reverse):
        rows = pl.ds(pl.multiple_of(blk * SUBLANES, SUBLANES), SUBLANES)
        a = a_ref[rows, :]
        b = b_ref[rows, :]
        s = 1
        while s < SUBLANES:
            keep = (sub < SUBLANES - s) if reverse else (sub >= s)
            shift = SUBLANES - s if reverse else s
            b = b + a * jnp.where(keep, pltpu.roll(b, shift, 0), 0.0)
            a = a * jnp.where(keep, pltpu.roll(a, shift, 0), 1.0)
            s *= 2
        h = b + a * carry
        b_ref[rows, :] = h
        last = h[0:1, :] if reverse else h[SUBLANES - 1:SUBLANES, :]
        return jnp.broadcast_to(last, h.shape)

    n_blk = l // SUBLANES
    c_blk = n_ctx // SUBLANES

    def ctx_body(i, carry):
        cf, cr = carry
        return tile_scan(a_f, b_f, i, cf, False), tile_scan(a_r, b_r, c_blk - 1 - i, cr, True)

    def lat_body(i, carry):
        cf, cr = carry
        return tile_scan(a_f, b_f, i, cf, False), tile_scan(a_r, b_r, n_blk + c_blk - 1 - i, cr, True)

    zero = jnp.zeros((SUBLANES, w), F32)
    carry = lax.fori_loop(0, c_blk, ctx_body, (zero, zero), unroll=2)
    lax.fori_loop(c_blk, n_blk, lat_body, carry, unroll=2)

    gain = og_ref[...]
    for r0 in range(0, l, ROW_TILE):
        rows = slice(r0, r0 + ROW_TILE)
        h = b_f[rows, :] + b_r[rows, :]
        o_ref[rows, :] = (_rms(h * gg_ref[rows, :]) * gain).astype(BF16)


def _lru(u, gg, conv_w, conv_b, w_gate, b_gate, lam, out_g, n_ctx):
    b, l, w = u.shape
    const = lambda shape: pl.BlockSpec(shape, lambda i: (0,) * len(shape))
    seq = pl.BlockSpec((None, l, w), lambda i: (i, 0, 0))
    return pl.pallas_call(
        functools.partial(_lru_kernel, n_ctx=n_ctx),
        grid=(b,),
        in_specs=[seq, seq, const(conv_w.shape), const((1, w)), const(w_gate.shape),
                  const(b_gate.shape), const(lam.shape), const((1, w))],
        out_specs=seq,
        out_shape=jax.ShapeDtypeStruct((b, l, w), BF16),
        scratch_shapes=[pltpu.VMEM((l + 2 * SUBLANES, w), F32)] + [pltpu.VMEM((l, w), F32)] * 4,
        compiler_params=_params(("parallel",)),
        name="rg_lru",
    )(u, gg, conv_w, conv_b, w_gate, b_gate, lam, out_g)


def _route(scores, biased):
    group_score = []
    for g in range(N_GROUPS):
        v = biased[g * GROUP_SIZE:(g + 1) * GROUP_SIZE]
        best = None
        for i in range(GROUP_SIZE):
            for j in range(i + 1, GROUP_SIZE):
                pair = v[i] + v[j]
                best = pair if best is None else jnp.maximum(best, pair)
        group_score.append(best)
    gid = jnp.zeros_like(group_score[0], dtype=jnp.int32)
    gbest = group_score[0]
    for g in range(1, N_GROUPS):
        upd = group_score[g] > gbest
        gid = jnp.where(upd, g, gid)
        gbest = jnp.where(upd, group_score[g], gbest)

    def pick(rows, j):
        out = rows[j]
        for g in range(1, N_GROUPS):
            out = jnp.where(gid == g, rows[g * GROUP_SIZE + j], out)
        return out

    v = [pick(biased, j) for j in range(GROUP_SIZE)]
    s = [pick(scores, j) for j in range(GROUP_SIZE)]
    i1 = jnp.zeros_like(gid)
    m1 = v[0]
    for j in range(1, GROUP_SIZE):
        upd = v[j] > m1
        i1 = jnp.where(upd, j, i1)
        m1 = jnp.where(upd, v[j], m1)
    i2 = jnp.zeros_like(gid)
    m2 = jnp.full_like(m1, -jnp.inf)
    for j in range(GROUP_SIZE):
        upd = (i1 != j) & (v[j] > m2)
        i2 = jnp.where(upd, j, i2)
        m2 = jnp.where(upd, v[j], m2)
    w1 = s[0]
    w2 = s[0]
    for j in range(1, GROUP_SIZE):
        w1 = jnp.where(i1 == j, s[j], w1)
        w2 = jnp.where(i2 == j, s[j], w2)
    den = w1 + w2
    w1 = w1 / den
    w2 = w2 / den
    lo = jnp.minimum(i1, i2)
    hi = jnp.maximum(i1, i2)
    pair = hi - lo - 1
    for k in range(1, GROUP_SIZE - 1):
        pair = pair + jnp.where(lo >= k, GROUP_SIZE - k, 0)
    first_is_lo = i1 < i2
    return gid * N_PAIRS + pair, jnp.where(first_is_lo, w1, w2), jnp.where(first_is_lo, w2, w1)


def _outproj_kernel(lat_ref, ctx_ref, a_ref, r_ref, *refs, skip):
    mod_refs = refs[:BATCH_STEP]
    w_ref, n2_ref, rwh_ref, rwl_ref, rb_ref, tri_ref = refs[BATCH_STEP:BATCH_STEP + 6]
    x1_ref, h2_ref, route_ref, counts_ref, cnt_ref = refs[BATCH_STEP + 6:]
    tile_rows = lat_ref.shape[1]
    rows = BATCH_STEP * tile_rows
    is_ctx = pl.program_id(1) + skip == 0
    half = a_ref.shape[-1]
    h2_parts = []
    for j, mod_ref in enumerate(mod_refs):
        x = jnp.where(is_ctx, ctx_ref[j], lat_ref[j])
        mix = (jnp.dot(a_ref[j], w_ref[0:half, :], preferred_element_type=F32)
               + jnp.dot(r_ref[j], w_ref[half:, :], preferred_element_type=F32))
        x1 = x + mod_ref[2:3, :] * mix
        x1_ref[j] = x1
        h2 = _rms(x1) * n2_ref[...]
        h2 = h2 * (1.0 + mod_ref[4:5, :]) + mod_ref[3:4, :]
        h2_ref[j] = h2
        h2_parts.append(h2)
    h2 = jnp.concatenate(h2_parts, axis=0)
    hi, lo = _split_bf16(h2)
    nt = (((1,), (1,)), ((), ()))
    logits = (lax.dot_general(rwh_ref[...], hi, nt, preferred_element_type=F32)
              + lax.dot_general(rwh_ref[...], lo, nt, preferred_element_type=F32)
              + lax.dot_general(rwl_ref[...], hi, nt, preferred_element_type=F32))
    scores = _sigmoid(logits)
    biased = scores + rb_ref[...]
    bucket, w_lo, w_hi = _route([scores[e:e + 1, :] for e in range(N_EXPERTS)],
                                [biased[e:e + 1, :] for e in range(N_EXPERTS)])

    @pl.when((pl.program_id(0) == 0) & (pl.program_id(1) == 0))
    def _():
        cnt_ref[...] = jnp.zeros_like(cnt_ref)

    sub = lax.broadcasted_iota(jnp.int32, (BUCKET_ROWS, rows), 0)
    onehot = jnp.where(sub == bucket, 1.0, 0.0)
    before = jnp.dot(onehot.astype(BF16), tri_ref[...], preferred_element_type=F32) + cnt_ref[...]
    rank = jnp.sum(onehot * before, axis=0, keepdims=True)
    total = cnt_ref[...] + jnp.sum(onehot, axis=1, keepdims=True)
    cnt_ref[...] = total
    counts_ref[...] = total

    for j in range(BATCH_STEP):
        cols = slice(j * tile_rows, (j + 1) * tile_rows)
        route_ref[j, 0:1, :] = bucket.astype(F32)[:, cols]
        route_ref[j, 1:2, :] = rank[:, cols]
        route_ref[j, 2:3, :] = w_lo[:, cols]
        route_ref[j, 3:4, :] = w_hi[:, cols]
        route_ref[j, 4:, :] = jnp.zeros((ROUTE_ROWS - 4, tile_rows), F32)


def _outproj(x_lat, x_ctx, lat_off, attn_n, rec_n, mods, w_out, n2g, rw_hi, rw_lo, rb, tri, skip):
    b, l, _ = rec_n.shape
    d = x_lat.shape[-1]
    nt = l // ROW_TILE - skip
    half = attn_n.shape[-1]
    row = lambda w: pl.BlockSpec((BATCH_STEP, ROW_TILE, w), lambda i, t: (i, t + skip, 0))
    out_row_of = lambda w: pl.BlockSpec((BATCH_STEP, ROW_TILE, w), lambda i, t: (i, t, 0))
    out_row = out_row_of(d)
    const = lambda shape: pl.BlockSpec(shape, lambda i, t: (0,) * len(shape))
    step_rows = BATCH_STEP * ROW_TILE
    return pl.pallas_call(
        functools.partial(_outproj_kernel, skip=skip),
        grid=(b // BATCH_STEP, nt),
        in_specs=[
            pl.BlockSpec((BATCH_STEP, ROW_TILE, d), lambda i, t: (i, jnp.maximum(t + skip - lat_off, 0), 0)),
            pl.BlockSpec((BATCH_STEP, ROW_TILE, d), lambda i, t: (i, 0, 0)),
            out_row_of(half), row(half),
            *_mod_specs(b, d, lambda t: t + skip),
            const(w_out.shape), const((1, d)), const(rw_hi.shape), const(rw_lo.shape), const(rb.shape),
            const(tri.shape),
        ],
        out_specs=[out_row, out_row,
                   pl.BlockSpec((BATCH_STEP, None, ROUTE_ROWS, ROW_TILE), lambda i, t: (i, t, 0, 0)),
                   const((BUCKET_ROWS, step_rows))],
        out_shape=[
            jax.ShapeDtypeStruct((b, nt * ROW_TILE, d), F32),
            jax.ShapeDtypeStruct((b, nt * ROW_TILE, d), F32),
            jax.ShapeDtypeStruct((b, nt, ROUTE_ROWS, ROW_TILE), F32),
            jax.ShapeDtypeStruct((BUCKET_ROWS, step_rows), F32),
        ],
        scratch_shapes=[pltpu.VMEM((BUCKET_ROWS, step_rows), F32)],
        compiler_params=_params(("arbitrary", "arbitrary")),
        name="out_proj_router",
    )(x_lat, x_ctx, attn_n, rec_n, *([mods] * BATCH_STEP), w_out, n2g, rw_hi, rw_lo, rb, tri)


def _moe_plan(route, counts, n_tiles):
    tm = ROW_TILE
    bucket = route[:, :, 0, :].reshape(-1).astype(jnp.int32)
    rank = route[:, :, 1, :].reshape(-1).astype(jnp.int32)
    cnt = counts[:N_BUCKETS, 0].astype(jnp.int32)
    tiles_per = (cnt + tm - 1) // tm
    tile_end = jnp.cumsum(tiles_per)
    pos = (tile_end - tiles_per)[bucket] * tm + rank
    total = tile_end[-1]
    tile = jnp.arange(n_tiles, dtype=jnp.int32)
    last = jnp.minimum(tile, total - 1)
    tile_bucket = jnp.sum((tile_end[None, :] <= last[:, None]).astype(jnp.int32), axis=1)
    tile_bucket = jnp.minimum(tile_bucket, N_BUCKETS - 1)
    group, pair = tile_bucket // N_PAIRS, tile_bucket % N_PAIRS
    pairs = [(i, j) for i in range(GROUP_SIZE) for j in range(i + 1, GROUP_SIZE)]
    lo = jnp.array([p[0] for p in pairs], jnp.int32)[pair]
    hi = jnp.array([p[1] for p in pairs], jnp.int32)[pair]
    weights = route[:, :, 2:4, :].transpose(0, 1, 3, 2).reshape(-1, 2)
    w_aug = jnp.pad(weights, ((0, 0), (0, LANES - 2)))
    tile_ea = group * GROUP_SIZE + lo
    tile_eb = group * GROUP_SIZE + hi
    changed = lambda e: jnp.concatenate([jnp.ones((1,), jnp.int32), (e[1:] != e[:-1]).astype(jnp.int32)])
    return (pos, tile_ea, tile_eb, changed(tile_ea), changed(tile_eb), (tile < total).astype(jnp.int32), w_aug)


def _token_base(i, t, j, nt, rows):
    return ((BATCH_STEP * i + j) * nt + t) * rows


def _dispatch_kernel(pos_ref, h_ref, *refs):
    w_refs = refs[:BATCH_STEP]
    xs_ref, stage, sem = refs[BATCH_STEP + 1:]
    _, rows, d = h_ref.shape
    nt = pl.num_programs(1)
    step = pl.program_id(0) * nt + pl.program_id(1)
    n = pl.num_programs(0) * nt
    slot = step % 2

    def drain(s):
        pltpu.make_async_copy(stage.at[s], xs_ref.at[pl.ds(0, BATCH_STEP * rows), :], sem.at[s]).wait()

    @pl.when(step >= 2)
    def _():
        drain(slot)

    for j in range(BATCH_STEP):
        stage[slot, j * rows:(j + 1) * rows, 0:d] = h_ref[j]
        stage[slot, j * rows:(j + 1) * rows, d:] = w_refs[j][...]
        base = _token_base(pl.program_id(0), pl.program_id(1), j, nt, rows)

        def issue(g, c, j=j, base=base):
            r0 = pl.multiple_of(g * SUBLANES, SUBLANES)
            for k in range(SUBLANES):
                p = pos_ref[base + r0 + k]
                pltpu.make_async_copy(stage.at[slot, pl.ds(j * rows + r0 + k, 1), :],
                                      xs_ref.at[pl.ds(p, 1), :], sem.at[slot]).start()
            return c

        lax.fori_loop(0, rows // SUBLANES, issue, 0)

    @pl.when(step == n - 1)
    def _():
        drain(1 - slot)
        drain(slot)


def _dispatch(pos, h2, w_aug, xs0):
    b, l, d = h2.shape
    nt = l // ROW_TILE
    assert (b // BATCH_STEP) * nt >= 2
    n_rows = xs0.shape[0]
    w_spec = lambda j: pl.BlockSpec((ROW_TILE, LANES), lambda i, t, pos: ((BATCH_STEP * i + j) * nt + t, 0))
    return pl.pallas_call(
        _dispatch_kernel,
        grid_spec=pltpu.PrefetchScalarGridSpec(
            num_scalar_prefetch=1,
            grid=(b // BATCH_STEP, nt),
            in_specs=[
                pl.BlockSpec((BATCH_STEP, ROW_TILE, d), lambda i, t, pos: (i, t, 0)),
                *[w_spec(j) for j in range(BATCH_STEP)],
                pl.BlockSpec(memory_space=pl.ANY),
            ],
            out_specs=pl.BlockSpec(memory_space=pl.ANY),
            scratch_shapes=[pltpu.VMEM((2, BATCH_STEP * ROW_TILE, d + LANES), F32),
                            pltpu.SemaphoreType.DMA((2,))],
        ),
        out_shape=jax.ShapeDtypeStruct((n_rows, d + LANES), F32),
        input_output_aliases={2 + BATCH_STEP: 0},
        compiler_params=_params(("arbitrary", "arbitrary")),
        name="moe_dispatch",
    )(pos, h2, *([w_aug] * BATCH_STEP), xs0)


def _moe_pair_kernel(ea_ref, eb_ref, new_a_ref, new_b_ref, valid_ref, xs_ref,
                     gate_a, up_a, down_a, gate_b, up_b, down_b, y_ref, wgu_a, wd_a, wgu_b, wd_b):
    del ea_ref, eb_ref
    i = pl.program_id(0)
    d = y_ref.shape[1]
    d_exp = wd_a.shape[0]

    def load_expert(gate, up, down, wgu, wd):
        wgu[:, :d_exp] = gate[...].astype(BF16)
        wgu[:, d_exp:] = up[...].astype(BF16)
        wd[...] = down[...].astype(BF16)

    @pl.when(new_a_ref[i] != 0)
    def _():
        load_expert(gate_a, up_a, down_a, wgu_a, wd_a)

    @pl.when(new_b_ref[i] != 0)
    def _():
        load_expert(gate_b, up_b, down_b, wgu_b, wd_b)

    @pl.when(valid_ref[i] != 0)
    def _():
        xs = xs_ref[...]
        x = xs[:, :d].astype(BF16)

        def down(gu, wd):
            gate = gu[:, :d_exp]
            hid = gate * _sigmoid(gate) * gu[:, d_exp:]
            return jnp.dot(hid.astype(BF16), wd[...], preferred_element_type=F32)

        gu_a = jnp.dot(x, wgu_a[...], preferred_element_type=F32)
        gu_b = jnp.dot(x, wgu_b[...], preferred_element_type=F32)
        y_ref[...] = xs[:, d:d + 1] * down(gu_a, wd_a) + xs[:, d + 1:d + 2] * down(gu_b, wd_b)

    @pl.when(valid_ref[i] == 0)
    def _():
        y_ref[...] = jnp.zeros_like(y_ref)


def _moe_pairs(xs, tile_ea, tile_eb, new_a, new_b, tile_valid, w_gate, w_up, w_down, layer):
    n_rows, da = xs.shape
    d = da - LANES
    d_exp = w_gate.shape[-1]
    tm = ROW_TILE
    w_in = lambda pick: pl.BlockSpec((None, None, d, d_exp), lambda i, ea, eb, *_: (layer, pick(ea, eb)[i], 0, 0))
    w_out = lambda pick: pl.BlockSpec((None, None, d_exp, d), lambda i, ea, eb, *_: (layer, pick(ea, eb)[i], 0, 0))
    first = lambda ea, eb: ea
    second = lambda ea, eb: eb
    return pl.pallas_call(
        _moe_pair_kernel,
        grid_spec=pltpu.PrefetchScalarGridSpec(
            num_scalar_prefetch=5,
            grid=(n_rows // tm,),
            in_specs=[
                pl.BlockSpec((tm, da), lambda i, *_: (i, 0)),
                w_in(first), w_in(first), w_out(first), w_in(second), w_in(second), w_out(second),
            ],
            out_specs=pl.BlockSpec((tm, d), lambda i, *_: (i, 0)),
            scratch_shapes=[pltpu.VMEM((d, 2 * d_exp), BF16), pltpu.VMEM((d_exp, d), BF16)] * 2,
        ),
        out_shape=jax.ShapeDtypeStruct((n_rows, d), F32),
        compiler_params=_params(("arbitrary",)),
        name="moe_pairs",
    )(tile_ea, tile_eb, new_a, new_b, tile_valid, xs, w_gate, w_up, w_down, w_gate, w_up, w_down)


def _combine_kernel(pos_ref, x_ref, *refs, final):
    mod_refs = refs[:BATCH_STEP]
    g_ref, ys_ref, o_ref, ybuf, sem = refs[BATCH_STEP:]
    _, rows, _ = x_ref.shape
    nt = pl.num_programs(1)
    step = pl.program_id(0) * nt + pl.program_id(1)
    n = pl.num_programs(0) * nt
    slot = step % 2

    def gather(at_step, s):
        i, t = at_step // nt, at_step % nt
        for j in range(BATCH_STEP):
            base = _token_base(i, t, j, nt, rows)

            def issue(g, c, j=j, base=base):
                r0 = pl.multiple_of(g * SUBLANES, SUBLANES)
                for k in range(SUBLANES):
                    p = pos_ref[base + r0 + k]
                    pltpu.make_async_copy(ys_ref.at[pl.ds(p, 1), :],
                                          ybuf.at[s, pl.ds(j * rows + r0 + k, 1), :], sem.at[s]).start()
                return c

            lax.fori_loop(0, rows // SUBLANES, issue, 0)

    @pl.when(step == 0)
    def _():
        gather(0, 0)

    @pl.when(step + 1 < n)
    def _():
        gather(step + 1, 1 - slot)

    pltpu.make_async_copy(ys_ref.at[pl.ds(0, BATCH_STEP * rows), :], ybuf.at[slot], sem.at[slot]).wait()
    for j, mod_ref in enumerate(mod_refs):
        x2 = x_ref[j] + mod_ref[5:6, :] * ybuf[slot, j * rows:(j + 1) * rows, :]
        o_ref[j] = _rms(x2) * g_ref[...] if final else x2


def _combine(pos, x1, mods, gain, ys, has_ctx, final):
    b, l, d = x1.shape
    nt = l // ROW_TILE
    row = pl.BlockSpec((BATCH_STEP, ROW_TILE, d), lambda i, t, pos: (i, t, 0))
    return pl.pallas_call(
        functools.partial(_combine_kernel, final=final),
        grid_spec=pltpu.PrefetchScalarGridSpec(
            num_scalar_prefetch=1,
            grid=(b // BATCH_STEP, nt),
            in_specs=[
                row,
                *_mod_specs(b, d, (lambda t: t) if has_ctx else (lambda t: t + 1)),
                pl.BlockSpec((1, d), lambda i, t, pos: (0, 0)),
                pl.BlockSpec(memory_space=pl.ANY),
            ],
            out_specs=row,
            scratch_shapes=[pltpu.VMEM((2, BATCH_STEP * ROW_TILE, d), F32), pltpu.SemaphoreType.DMA((2,))],
        ),
        out_shape=jax.ShapeDtypeStruct((b, l, d), F32),
        compiler_params=_params(("arbitrary", "arbitrary")),
        name="moe_combine",
    )(pos, x1, *([mods] * BATCH_STEP), gain, ys)


def _rope_tables(n_ctx, n_lat):
    rows = n_lat // GRID_W
    r, col = jnp.meshgrid(jnp.arange(rows), jnp.arange(GRID_W), indexing="ij")
    r = r.reshape(-1).astype(F32)
    col = col.reshape(-1).astype(F32)
    half = HEAD_DIM // 2
    inv = ROPE_THETA ** (-jnp.arange(0, half, 2, dtype=F32) / half)
    ang_r = r[:, None] * inv
    ang_c = col[:, None] * inv
    cos_h = jnp.concatenate([jnp.cos(ang_r)] * 2 + [jnp.cos(ang_c)] * 2, axis=-1)
    sin_h = jnp.concatenate([-jnp.sin(ang_r), jnp.sin(ang_r), -jnp.sin(ang_c), jnp.sin(ang_c)], axis=-1)
    cos_t = jnp.concatenate([jnp.ones((n_ctx, HEAD_DIM), F32), cos_h], axis=0)
    sin_t = jnp.concatenate([jnp.zeros((n_ctx, HEAD_DIM), F32), sin_h], axis=0)
    reps = LANES // HEAD_DIM
    return jnp.tile(cos_t, (1, reps)), jnp.tile(sin_t, (1, reps))


def _block_diag(w):
    n, d, e = w.shape
    eye = jnp.eye(n, dtype=w.dtype)
    return (w[:, :, None, :] * eye[:, None, :, None]).reshape(n * d, n * e)


def kernel(x, c, ctx, c_ctx, ada_w, ada_b, norm1_g, w_in, q_norm_g, k_norm_g, conv_w, conv_b, lru_wa, lru_ba, lru_wx, lru_bx, lru_lambda, attn_out_g, lru_out_g, w_out, norm2_g, router_w, router_b, exp_w_gate, exp_w_up, exp_w_down, final_g):
    b, s, d = x.shape
    n_ctx = ctx.shape[1]
    l = n_ctx + s
    depth = ada_w.shape[0]
    nt = l // ROW_TILE
    assert n_ctx == ROW_TILE and s % ROW_TILE == 0 and b % BATCH_STEP == 0

    x_lat, x_ctx, lat_off = x, ctx, 1
    cvec = jnp.zeros((2 * SUBLANES, d), F32).at[:b].set(c).at[b].set(c_ctx)
    mods_all = _ada_mods(cvec, ada_w, ada_b)[:, :b + 1].reshape(depth, b + 1, N_MOD, d)
    cos_t, sin_t = _rope_tables(n_ctx, s)

    head_avg = _block_diag(jnp.full((N_Q_HEADS, HEAD_DIM, HEAD_DIM), 1.0 / HEAD_DIM, F32)).astype(BF16)
    rw_t = router_w.T
    rw_hi = rw_t.astype(BF16)
    rw_lo = (rw_t - rw_hi.astype(F32)).astype(BF16)
    step_rows = BATCH_STEP * ROW_TILE
    rb = jnp.broadcast_to(router_b[:, None], (N_EXPERTS, step_rows)).astype(F32)
    tri = jnp.triu(jnp.ones((step_rows, step_rows), BF16), 1)

    n_tiles = b * nt + N_BUCKETS
    xs = jnp.zeros((n_tiles * ROW_TILE, d + LANES), F32)

    qw, kw = ATTN_WIDTH, KV_WIDTH
    out = None
    for li in range(depth):
        wi = w_in[li]
        k0, k1 = wi[:, qw:qw + HEAD_DIM], wi[:, qw + HEAD_DIM:qw + kw]
        v0, v1 = wi[:, qw + kw:qw + kw + HEAD_DIM], wi[:, qw + kw + HEAD_DIM:qw + 2 * kw]
        w_ext = jnp.concatenate([wi[:, :qw], k0, k0, k1, k1, v0, v0, v1, v1, wi[:, qw + 2 * kw:]],
                                axis=1).astype(BF16)
        gq = jnp.tile(q_norm_g[li], N_Q_HEADS)[None, :]
        gk = jnp.tile(k_norm_g[li], 2 * N_KV_HEADS)[None, :]
        mods = mods_all[li]
        q, kd, vd, u, gg = _inproj(x_lat, x_ctx, lat_off, mods, norm1_g[li][None, :], w_ext, cos_t, sin_t,
                                   gq, gk, head_avg)
        skip = n_ctx // ROW_TILE if li == depth - 1 else 0
        attn_n = _attention(q, kd, vd, attn_out_g[li][None, :], n_ctx, skip)

        w_gate = jnp.concatenate([_block_diag(lru_wa[li, 0]), _block_diag(lru_wx[li, 0]),
                                  _block_diag(lru_wa[li, 1]), _block_diag(lru_wx[li, 1])], axis=1)
        w_gate = (0.5 * w_gate).astype(BF16)
        b_gate = 0.5 * jnp.concatenate([lru_ba[li, 0], lru_bx[li, 0], lru_ba[li, 1], lru_bx[li, 1]])[None, :]
        rec_n = _lru(u, gg, conv_w[li], conv_b[li][None, :], w_gate, b_gate, lru_lambda[li],
                     lru_out_g[li][None, :], n_ctx)

        x1, h2, route, counts = _outproj(x_lat, x_ctx, lat_off, attn_n, rec_n, mods, w_out[li].astype(BF16),
                                         norm2_g[li][None, :], rw_hi, rw_lo, rb, tri, skip)
        pos, tile_ea, tile_eb, new_a, new_b, tile_valid, w_aug = _moe_plan(route, counts, n_tiles)
        xs = _dispatch(pos, h2, w_aug, xs)
        ys = _moe_pairs(xs, tile_ea, tile_eb, new_a, new_b, tile_valid, exp_w_gate, exp_w_up, exp_w_down, li)
        if li == depth - 1:
            out = _combine(pos, x1, mods, final_g[None, :], ys, skip == 0, True)
        else:
            x_lat = x_ctx = _combine(pos, x1, mods, final_g[None, :], ys, skip == 0, False)
            lat_off = 0
    return out
```

```python
import functools

import jax
import jax.numpy as jnp
from jax import lax
from jax.experimental import pallas as pl
from jax.experimental.pallas import tpu as pltpu

F32 = jnp.float32
BF16 = jnp.bfloat16

HEAD_DIM = 64
N_Q_HEADS = 8
N_KV_HEADS = 2
ATTN_WIDTH = N_Q_HEADS * HEAD_DIM
KV_WIDTH = N_KV_HEADS * HEAD_DIM
LRU_WIDTH = 512
LRU_BLOCKS = 8
CONV_WIDTH = 4
CONV_LEFT = CONV_WIDTH // 2
LRU_C = 8.0
GRID_W = 64
ROPE_THETA = 10000.0
N_EXPERTS = 16
N_GROUPS = 4
GROUP_SIZE = N_EXPERTS // N_GROUPS
N_MOD = 6
NORM_EPS = 1e-6
ATTN_SCALE = HEAD_DIM ** -0.5
LOG2_E = 1.4426950408889634

LANES = 128
SUBLANES = 8
ROW_TILE = 256
BATCH_STEP = 2
N_PAIRS = GROUP_SIZE * (GROUP_SIZE - 1) // 2
N_BUCKETS = N_GROUPS * N_PAIRS
BUCKET_ROWS = 32
ROUTE_ROWS = 8
VMEM_LIMIT = 60000 * 1024


def _params(sem):
    return pltpu.CompilerParams(dimension_semantics=sem, vmem_limit_bytes=VMEM_LIMIT)


def _sigmoid(x):
    return 1.0 / (1.0 + jnp.exp(-x))


def _rms(x):
    return x * lax.rsqrt(jnp.mean(x * x, axis=-1, keepdims=True) + NORM_EPS)


def _split_bf16(x):
    hi = x.astype(BF16)
    lo = (x - hi.astype(F32)).astype(BF16)
    return hi, lo


def _ada_kernel(c_ref, w_ref, b_ref, o_ref):
    cv = c_ref[...]
    s = cv * _sigmoid(cv)
    o_ref[...] = jnp.dot(s.astype(BF16), w_ref[...].astype(BF16),
                         preferred_element_type=F32) + b_ref[...]


def _ada_mods(cvec, ada_w, ada_b):
    depth, d, n = ada_w.shape
    tn = 512
    rows = cvec.shape[0]
    return pl.pallas_call(
        _ada_kernel,
        grid=(depth, n // tn),
        in_specs=[
            pl.BlockSpec((rows, d), lambda l, j: (0, 0)),
            pl.BlockSpec((None, d, tn), lambda l, j: (l, 0, j)),
            pl.BlockSpec((None, 1, tn), lambda l, j: (l, 0, j)),
        ],
        out_specs=pl.BlockSpec((None, rows, tn), lambda l, j: (l, 0, j)),
        out_shape=jax.ShapeDtypeStruct((depth, rows, n), F32),
        compiler_params=_params(("parallel", "parallel")),
        name="ada_mods",
    )(cvec, ada_w, ada_b.reshape(depth, 1, n))


def _head_rms(t, g_mat, gain):
    hi, lo = _split_bf16(t * t)
    m = (jnp.dot(hi, g_mat, preferred_element_type=F32)
         + jnp.dot(lo, g_mat, preferred_element_type=F32))
    return t * lax.rsqrt(m + NORM_EPS) * gain


def _rope(t, cos_t, sin_t):
    width = t.shape[-1]
    reps = width // LANES
    cos_w = jnp.concatenate([cos_t] * reps, axis=1)
    sin_w = jnp.concatenate([sin_t] * reps, axis=1)
    lane = lax.broadcasted_iota(jnp.int32, t.shape, 1)
    quarter = HEAD_DIM // 4
    first = (lane % (2 * quarter)) < quarter
    partner = jnp.where(first, pltpu.roll(t, width - quarter, 1), pltpu.roll(t, quarter, 1))
    return t * cos_w + partner * sin_w


def _inproj_kernel(lat_ref, ctx_ref, *refs):
    mod_refs = refs[:BATCH_STEP]
    n1_ref, w_ref, cos_ref, sin_ref, gq_ref, gk_ref, gm_ref = refs[BATCH_STEP:BATCH_STEP + 7]
    q_ref, k_ref, v_ref, u_ref, gg_ref = refs[BATCH_STEP + 7:]
    kd = 2 * KV_WIDTH
    o_k, o_v, o_u, o_g = ATTN_WIDTH, ATTN_WIDTH + kd, ATTN_WIDTH + 2 * kd, ATTN_WIDTH + 2 * kd + LRU_WIDTH
    cos_t = cos_ref[...]
    sin_t = sin_ref[...]
    g_mat = gm_ref[...]
    is_ctx = pl.program_id(1) == 0
    for j, mod_ref in enumerate(mod_refs):
        x = jnp.where(is_ctx, ctx_ref[j], lat_ref[j])
        h = _rms(x) * n1_ref[...]
        h = h * (1.0 + mod_ref[1:2, :]) + mod_ref[0:1, :]
        y = jnp.dot(h.astype(BF16), w_ref[...], preferred_element_type=F32)
        q = _rope(_head_rms(y[:, :o_k], g_mat, gq_ref[...]), cos_t, sin_t)
        q_ref[j] = (q * (ATTN_SCALE * LOG2_E)).astype(BF16)
        k = _rope(_head_rms(y[:, o_k:o_v], g_mat[:kd, :kd], gk_ref[...]), cos_t, sin_t)
        k_ref[j] = k.astype(BF16)
        vt = y[:, o_v:o_u].T
        sub = lax.broadcasted_iota(jnp.int32, vt.shape, 0)
        v_ref[j] = jnp.where(sub % LANES < HEAD_DIM, vt, 1.0).astype(BF16)
        u_ref[j] = y[:, o_u:o_g]
        gb = y[:, o_g:]
        gg_ref[j] = 0.5 * gb * (1.0 + jnp.tanh(0.7978845608028654 * (gb + 0.044715 * gb * gb * gb)))


def _seq_specs(lat_off, d):
    lat = pl.BlockSpec((BATCH_STEP, ROW_TILE, d), lambda i, t: (i, jnp.maximum(t - lat_off, 0), 0))
    ctx = pl.BlockSpec((BATCH_STEP, ROW_TILE, d), lambda i, t: (i, 0, 0))
    return lat, ctx


def _mod_specs(n_batch, d, tile_of=lambda t: t):
    return [pl.BlockSpec((None, N_MOD, d),
                         lambda i, t, *_, j=j: (jnp.where(tile_of(t) == 0, n_batch, BATCH_STEP * i + j), 0, 0))
            for j in range(BATCH_STEP)]


def _inproj(x_lat, x_ctx, lat_off, mods, n1g, w_ext, cos_t, sin_t, gq, gk, g_mat):
    b, _, d = x_lat.shape
    l = x_lat.shape[1] + lat_off * x_ctx.shape[1]
    nt = l // ROW_TILE
    n_ext = w_ext.shape[1]
    kd = 2 * KV_WIDTH
    row = lambda w: pl.BlockSpec((BATCH_STEP, ROW_TILE, w), lambda i, t: (i, t, 0))
    const = lambda shape: pl.BlockSpec(shape, lambda i, t: (0,) * len(shape))
    return pl.pallas_call(
        _inproj_kernel,
        grid=(b // BATCH_STEP, nt),
        in_specs=[
            *_seq_specs(lat_off, d),
            *_mod_specs(b, d),
            const((1, d)),
            const((d, n_ext)),
            pl.BlockSpec((ROW_TILE, LANES), lambda i, t: (t, 0)),
            pl.BlockSpec((ROW_TILE, LANES), lambda i, t: (t, 0)),
            const((1, ATTN_WIDTH)),
            const((1, kd)),
            const((ATTN_WIDTH, ATTN_WIDTH)),
        ],
        out_specs=[row(ATTN_WIDTH), row(kd), pl.BlockSpec((BATCH_STEP, kd, ROW_TILE), lambda i, t: (i, 0, t)),
                   row(LRU_WIDTH), row(LRU_WIDTH)],
        out_shape=[
            jax.ShapeDtypeStruct((b, l, ATTN_WIDTH), BF16),
            jax.ShapeDtypeStruct((b, l, kd), BF16),
            jax.ShapeDtypeStruct((b, kd, l), BF16),
            jax.ShapeDtypeStruct((b, l, LRU_WIDTH), F32),
            jax.ShapeDtypeStruct((b, l, LRU_WIDTH), F32),
        ],
        compiler_params=_params(("parallel", "parallel")),
        name="in_proj",
    )(x_lat, x_ctx, *([mods] * BATCH_STEP), n1g, w_ext, cos_t, sin_t, gq, gk, g_mat)


def _attn_kernel(q_ref, k_ref, v_ref, g_ref, o_ref, *, n_ctx, skip):
    t = pl.program_id(1) + skip
    rows = q_ref.shape[1]
    n_all = k_ref.shape[1]

    def run(n_keys):
        low = lax.broadcasted_iota(jnp.int32, (rows, LANES), 1) < HEAD_DIM
        n_pairs = N_Q_HEADS // 2
        kv_of = lambda pair: (2 * pair) // (N_Q_HEADS // N_KV_HEADS)

        def scores(j, pair):
            qp = q_ref[j, :, pair * LANES:(pair + 1) * LANES]
            zero = jnp.zeros_like(qp)
            q2 = jnp.concatenate([jnp.where(low, qp, zero), jnp.where(low, zero, qp)], axis=0)
            kd = k_ref[j, 0:n_keys, kv_of(pair) * LANES:(kv_of(pair) + 1) * LANES]
            return lax.dot_general(kd, q2, (((1,), (1,)), ((), ())), preferred_element_type=F32)

        def softmax_pv(j, pair, st):
            vt = v_ref[j, kv_of(pair) * LANES:(kv_of(pair) + 1) * LANES, 0:n_keys]
            p = jnp.exp2(st - jnp.max(st, axis=0, keepdims=True))
            ot = jnp.dot(vt, p.astype(BF16), preferred_element_type=F32)
            on = ot[:HEAD_DIM, :] * (1.0 / ot[HEAD_DIM:HEAD_DIM + 1, :])
            return jnp.concatenate([on[:, :rows], on[:, rows:]], axis=0).T

        units = [(j, pair) for pair in range(n_pairs) for j in range(BATCH_STEP)]
        outs = [[] for _ in range(BATCH_STEP)]
        st = scores(*units[0])
        for u, (j, pair) in enumerate(units):
            st_next = scores(*units[u + 1]) if u + 1 < len(units) else None
            outs[j].append(softmax_pv(j, pair, st))
            st = st_next
        for j in range(BATCH_STEP):
            a = jnp.concatenate(outs[j], axis=1)
            o_ref[j] = (_rms(a) * g_ref[...]).astype(BF16)

    @pl.when(t == 0)
    def _():
        run(n_ctx)

    @pl.when(t > 0)
    def _():
        run(n_all)


def _attention(q, kd, vd, gain, n_ctx, skip):
    b, l, w = q.shape
    nt = l // ROW_TILE - skip
    kw = kd.shape[-1]
    return pl.pallas_call(
        functools.partial(_attn_kernel, n_ctx=n_ctx, skip=skip),
        grid=(b // BATCH_STEP, nt),
        in_specs=[
            pl.BlockSpec((BATCH_STEP, ROW_TILE, w), lambda i, t: (i, t + skip, 0)),
            pl.BlockSpec((BATCH_STEP, l, kw), lambda i, t: (i, 0, 0)),
            pl.BlockSpec((BATCH_STEP, kw, l), lambda i, t: (i, 0, 0)),
            pl.BlockSpec((1, w), lambda i, t: (0, 0)),
        ],
        out_specs=pl.BlockSpec((BATCH_STEP, ROW_TILE, w), lambda i, t: (i, t, 0)),
        out_shape=jax.ShapeDtypeStruct((b, nt * ROW_TILE, w), BF16),
        compiler_params=_params(("parallel", "parallel")),
        name="attention",
    )(q, kd, vd, gain)


def _lru_kernel(u_ref, gg_ref, cw_ref, cb_ref, wg_ref, bg_ref, lam_ref, og_ref, o_ref,
                upad, a_f, b_f, a_r, b_r, *, n_ctx):
    l, w = u_ref.shape
    pad = SUBLANES
    zeros_pad = jnp.zeros((pad, w), F32)
    upad[0:pad, :] = zeros_pad
    upad[pad + l:2 * pad + l, :] = zeros_pad
    for r0 in range(0, l, ROW_TILE):
        upad[pad + r0:pad + r0 + ROW_TILE, :] = u_ref[r0:r0 + ROW_TILE, :]

    neg_lam = -lam_ref[...]
    softplus = jnp.maximum(neg_lam, 0.0) + jnp.log1p(jnp.exp(-jnp.abs(neg_lam)))
    half_decay = (-0.5 * LRU_C) * softplus
    cw = cw_ref[...]
    cb = cb_ref[...]

    for r0 in range(0, l, ROW_TILE):
        row = r0 + lax.broadcasted_iota(jnp.int32, (ROW_TILE, w), 0)
        is_lat = row >= n_ctx
        near_boundary = r0 - CONV_WIDTH < n_ctx < r0 + ROW_TILE + CONV_WIDTH
        uc = jnp.zeros((ROW_TILE, w), F32) + cb
        for j in range(CONV_WIDTH):
            off = j - CONV_LEFT
            tap = upad[pad + r0 + off:pad + r0 + off + ROW_TILE, :]
            if off != 0 and near_boundary:
                tap = jnp.where(((row + off) >= n_ctx) == is_lat, tap, 0.0)
            uc = uc + tap * cw[j:j + 1, :]
        z = jnp.dot(uc.astype(BF16), wg_ref[...], preferred_element_type=F32) + bg_ref[...]
        half_uc = 0.5 * uc
        for d, (a_ref, b_ref) in enumerate(((a_f, b_f), (a_r, b_r))):
            base = 2 * d * w
            t_r = jnp.tanh(z[:, base:base + w])
            t_i = jnp.tanh(z[:, base + w:base + 2 * w])
            log_a = t_r * half_decay[d:d + 1, :] + half_decay[d:d + 1, :]
            a = jnp.exp(log_a)
            one_minus_a2 = -jnp.tanh(log_a) * (a * a + 1.0)
            bb = jnp.sqrt(one_minus_a2) * ((t_i + 1.0) * half_uc)
            a_ref[r0:r0 + ROW_TILE, :] = a
            b_ref[r0:r0 + ROW_TILE, :] = bb

    sub = lax.broadcasted_iota(jnp.int32, (SUBLANES, w), 0)

    def tile_scan(a_ref, b_ref, blk, carry, reverse):
        rows = pl.ds(pl.multiple_of(blk * SUBLANES, SUBLANES), SUBLANES)
        a = a_ref[rows, :]
        b = b_ref[rows, :]
        s = 1
        while s < SUBLANES:
            keep = (sub < SUBLANES - s) if reverse else (sub >= s)
            shift = SUBLANES - s if reverse else s
            b = b + a * jnp.where(keep, pltpu.roll(b, shift, 0), 0.0)
            a = a * jnp.where(keep, pltpu.roll(a, shift, 0), 1.0)
            s *= 2
        h = b + a * carry
        b_ref[rows, :] = h
        last = h[0:1, :] if reverse else h[SUBLANES - 1:SUBLANES, :]
        return jnp.broadcast_to(last, h.shape)

    n_blk = l // SUBLANES
    c_blk = n_ctx // SUBLANES

    def ctx_body(i, carry):
        cf, cr = carry
        return tile_scan(a_f, b_f, i, cf, False), tile_scan(a_r, b_r, c_blk - 1 - i, cr, True)

    def lat_body(i, carry):
        cf, cr = carry
        return tile_scan(a_f, b_f, i, cf, False), tile_scan(a_r, b_r, n_blk + c_blk - 1 - i, cr, True)

    zero = jnp.zeros((SUBLANES, w), F32)
    carry = lax.fori_loop(0, c_blk, ctx_body, (zero, zero), unroll=2)
    lax.fori_loop(c_blk, n_blk, lat_body, carry, unroll=2)

    gain = og_ref[...]
    for r0 in range(0, l, ROW_TILE):
        rows = slice(r0, r0 + ROW_TILE)
        h = b_f[rows, :] + b_r[rows, :]
        o_ref[rows, :] = (_rms(h * gg_ref[rows, :]) * gain).astype(BF16)


def _lru(u, gg, conv_w, conv_b, w_gate, b_gate, lam, out_g, n_ctx):
    b, l, w = u.shape
    const = lambda shape: pl.BlockSpec(shape, lambda i: (0,) * len(shape))
    seq = pl.BlockSpec((None, l, w), lambda i: (i, 0, 0))
    return pl.pallas_call(
        functools.partial(_lru_kernel, n_ctx=n_ctx),
        grid=(b,),
        in_specs=[seq, seq, const(conv_w.shape), const((1, w)), const(w_gate.shape),
                  const(b_gate.shape), const(lam.shape), const((1, w))],
        out_specs=seq,
        out_shape=jax.ShapeDtypeStruct((b, l, w), BF16),
        scratch_shapes=[pltpu.VMEM((l + 2 * SUBLANES, w), F32)] + [pltpu.VMEM((l, w), F32)] * 4,
        compiler_params=_params(("parallel",)),
        name="rg_lru",
    )(u, gg, conv_w, conv_b, w_gate, b_gate, lam, out_g)


def _route(scores, biased):
    group_score = []
    for g in range(N_GROUPS):
        v = biased[g * GROUP_SIZE:(g + 1) * GROUP_SIZE]
        best = None
        for i in range(GROUP_SIZE):
            for j in range(i + 1, GROUP_SIZE):
                pair = v[i] + v[j]
                best = pair if best is None else jnp.maximum(best, pair)
        group_score.append(best)
    gid = jnp.zeros_like(group_score[0], dtype=jnp.int32)
    gbest = group_score[0]
    for g in range(1, N_GROUPS):
        upd = group_score[g] > gbest
        gid = jnp.where(upd, g, gid)
        gbest = jnp.where(upd, group_score[g], gbest)

    def pick(rows, j):
        out = rows[j]
        for g in range(1, N_GROUPS):
            out = jnp.where(gid == g, rows[g * GROUP_SIZE + j], out)
        return out

    v = [pick(biased, j) for j in range(GROUP_SIZE)]
    s = [pick(scores, j) for j in range(GROUP_SIZE)]
    i1 = jnp.zeros_like(gid)
    m1 = v[0]
    for j in range(1, GROUP_SIZE):
        upd = v[j] > m1
        i1 = jnp.where(upd, j, i1)
        m1 = jnp.where(upd, v[j], m1)
    i2 = jnp.zeros_like(gid)
    m2 = jnp.full_like(m1, -jnp.inf)
    for j in range(GROUP_SIZE):
        upd = (i1 != j) & (v[j] > m2)
        i2 = jnp.where(upd, j, i2)
        m2 = jnp.where(upd, v[j], m2)
    w1 = s[0]
    w2 = s[0]
    for j in range(1, GROUP_SIZE):
        w1 = jnp.where(i1 == j, s[j], w1)
        w2 = jnp.where(i2 == j, s[j], w2)
    den = w1 + w2
    w1 = w1 / den
    w2 = w2 / den
    lo = jnp.minimum(i1, i2)
    hi = jnp.maximum(i1, i2)
    pair = hi - lo - 1
    for k in range(1, GROUP_SIZE - 1):
        pair = pair + jnp.where(lo >= k, GROUP_SIZE - k, 0)
    first_is_lo = i1 < i2
    return gid * N_PAIRS + pair, jnp.where(first_is_lo, w1, w2), jnp.where(first_is_lo, w2, w1)


def _outproj_kernel(lat_ref, ctx_ref, a_ref, r_ref, *refs, skip):
    mod_refs = refs[:BATCH_STEP]
    w_ref, n2_ref, rwh_ref, rwl_ref, rb_ref, tri_ref = refs[BATCH_STEP:BATCH_STEP + 6]
    x1_ref, h2_ref, route_ref, counts_ref, cnt_ref = refs[BATCH_STEP + 6:]
    tile_rows, d_model = lat_ref.shape[1:]
    rows = BATCH_STEP * tile_rows
    is_ctx = pl.program_id(1) + skip == 0
    half = a_ref.shape[-1]
    h2_parts = []
    for j, mod_ref in enumerate(mod_refs):
        x = jnp.where(is_ctx, ctx_ref[j], lat_ref[j])
        mix = (jnp.dot(a_ref[j], w_ref[0:half, :], preferred_element_type=F32)
               + jnp.dot(r_ref[j], w_ref[half:, :], preferred_element_type=F32))
        x1 = x + mod_ref[2:3, :] * mix
        x1_ref[j] = x1
        h2 = _rms(x1) * n2_ref[...]
        h2 = h2 * (1.0 + mod_ref[4:5, :]) + mod_ref[3:4, :]
        h2_ref[j, :, 0:d_model] = h2
        h2_parts.append(h2)
    h2 = jnp.concatenate(h2_parts, axis=0)
    hi, lo = _split_bf16(h2)
    nt = (((1,), (1,)), ((), ()))
    logits = (lax.dot_general(rwh_ref[...], hi, nt, preferred_element_type=F32)
              + lax.dot_general(rwh_ref[...], lo, nt, preferred_element_type=F32)
              + lax.dot_general(rwl_ref[...], hi, nt, preferred_element_type=F32))
    scores = _sigmoid(logits)
    biased = scores + rb_ref[...]
    bucket, w_lo, w_hi = _route([scores[e:e + 1, :] for e in range(N_EXPERTS)],
                                [biased[e:e + 1, :] for e in range(N_EXPERTS)])

    @pl.when((pl.program_id(0) == 0) & (pl.program_id(1) == 0))
    def _():
        cnt_ref[...] = jnp.zeros_like(cnt_ref)

    sub = lax.broadcasted_iota(jnp.int32, (BUCKET_ROWS, rows), 0)
    onehot = jnp.where(sub == bucket, 1.0, 0.0)
    before = jnp.dot(onehot.astype(BF16), tri_ref[...], preferred_element_type=F32) + cnt_ref[...]
    rank = jnp.sum(onehot * before, axis=0, keepdims=True)
    total = cnt_ref[...] + jnp.sum(onehot, axis=1, keepdims=True)
    cnt_ref[...] = total
    counts_ref[...] = total

    wsub = lax.broadcasted_iota(jnp.int32, (LANES, tile_rows), 0)
    for j in range(BATCH_STEP):
        cols = slice(j * tile_rows, (j + 1) * tile_rows)
        route_ref[j, 0:1, :] = bucket.astype(F32)[:, cols]
        route_ref[j, 1:2, :] = rank[:, cols]
        route_ref[j, 2:, :] = jnp.zeros((ROUTE_ROWS - 2, tile_rows), F32)
        w_rows = jnp.where(wsub == 0, w_lo[:, cols], jnp.where(wsub == 1, w_hi[:, cols], 0.0))
        h2_ref[j, :, d_model:] = w_rows.T


def _outproj(x_lat, x_ctx, lat_off, attn_n, rec_n, mods, w_out, n2g, rw_hi, rw_lo, rb, tri, skip):
    b, l, _ = rec_n.shape
    d = x_lat.shape[-1]
    nt = l // ROW_TILE - skip
    half = attn_n.shape[-1]
    row = lambda w: pl.BlockSpec((BATCH_STEP, ROW_TILE, w), lambda i, t: (i, t + skip, 0))
    out_row_of = lambda w: pl.BlockSpec((BATCH_STEP, ROW_TILE, w), lambda i, t: (i, t, 0))
    out_row = out_row_of(d)
    const = lambda shape: pl.BlockSpec(shape, lambda i, t: (0,) * len(shape))
    step_rows = BATCH_STEP * ROW_TILE
    return pl.pallas_call(
        functools.partial(_outproj_kernel, skip=skip),
        grid=(b // BATCH_STEP, nt),
        in_specs=[
            pl.BlockSpec((BATCH_STEP, ROW_TILE, d), lambda i, t: (i, jnp.maximum(t + skip - lat_off, 0), 0)),
            pl.BlockSpec((BATCH_STEP, ROW_TILE, d), lambda i, t: (i, 0, 0)),
            out_row_of(half), row(half),
            *_mod_specs(b, d, lambda t: t + skip),
            const(w_out.shape), const((1, d)), const(rw_hi.shape), const(rw_lo.shape), const(rb.shape),
            const(tri.shape),
        ],
        out_specs=[out_row, out_row_of(d + LANES),
                   pl.BlockSpec((BATCH_STEP, None, ROUTE_ROWS, ROW_TILE), lambda i, t: (i, t, 0, 0)),
                   const((BUCKET_ROWS, step_rows))],
        out_shape=[
            jax.ShapeDtypeStruct((b, nt * ROW_TILE, d), F32),
            jax.ShapeDtypeStruct((b, nt * ROW_TILE, d + LANES), F32),
            jax.ShapeDtypeStruct((b, nt, ROUTE_ROWS, ROW_TILE), F32),
            jax.ShapeDtypeStruct((BUCKET_ROWS, step_rows), F32),
        ],
        scratch_shapes=[pltpu.VMEM((BUCKET_ROWS, step_rows), F32)],
        compiler_params=_params(("arbitrary", "arbitrary")),
        name="out_proj_router",
    )(x_lat, x_ctx, attn_n, rec_n, *([mods] * BATCH_STEP), w_out, n2g, rw_hi, rw_lo, rb, tri)


def _moe_plan(route, counts, n_tiles):
    tm = ROW_TILE
    bucket = route[:, :, 0, :].reshape(-1).astype(jnp.int32)
    rank = route[:, :, 1, :].reshape(-1).astype(jnp.int32)
    cnt = counts[:N_BUCKETS, 0].astype(jnp.int32)
    tiles_per = (cnt + tm - 1) // tm
    tile_end = jnp.cumsum(tiles_per)
    pos = (tile_end - tiles_per)[bucket] * tm + rank
    total = tile_end[-1]
    tile = jnp.arange(n_tiles, dtype=jnp.int32)
    last = jnp.minimum(tile, total - 1)
    tile_bucket = jnp.sum((tile_end[None, :] <= last[:, None]).astype(jnp.int32), axis=1)
    tile_bucket = jnp.minimum(tile_bucket, N_BUCKETS - 1)
    group, pair = tile_bucket // N_PAIRS, tile_bucket % N_PAIRS
    pairs = [(i, j) for i in range(GROUP_SIZE) for j in range(i + 1, GROUP_SIZE)]
    lo = jnp.array([p[0] for p in pairs], jnp.int32)[pair]
    hi = jnp.array([p[1] for p in pairs], jnp.int32)[pair]
    tile_ea = group * GROUP_SIZE + lo
    tile_eb = group * GROUP_SIZE + hi
    changed = lambda e: jnp.concatenate([jnp.ones((1,), jnp.int32), (e[1:] != e[:-1]).astype(jnp.int32)])
    src = jnp.zeros((n_tiles * tm,), jnp.int32).at[pos].set(jnp.arange(pos.shape[0], dtype=jnp.int32))
    return (pos, src, tile_ea, tile_eb, changed(tile_ea), changed(tile_eb), (tile < total).astype(jnp.int32))


def _token_base(i, t, j, nt, rows):
    return ((BATCH_STEP * i + j) * nt + t) * rows


def _moe_pair_kernel(ea_ref, eb_ref, new_a_ref, new_b_ref, valid_ref, src_ref,
                     gate_a, up_a, down_a, gate_b, up_b, down_b, h_hbm, y_ref,
                     wgu_a, wd_a, wgu_b, wd_b, xbuf, sem):
    del ea_ref, eb_ref
    i = pl.program_id(0)
    n = pl.num_programs(0)
    rows, d = y_ref.shape
    d_exp = wd_a.shape[0]
    slot = i % 2
    nxt = jnp.minimum(i + 1, n - 1)

    def row_copy(tile, r, s):
        return pltpu.make_async_copy(h_hbm.at[pl.ds(src_ref[tile * rows + r], 1), :],
                                     xbuf.at[s, pl.ds(r, 1), :], sem.at[s])

    def gather_loop(tile, s):
        def issue(g, c):
            r0 = pl.multiple_of(g * SUBLANES, SUBLANES)
            for k in range(SUBLANES):
                row_copy(tile, r0 + k, s).start()
            return c

        lax.fori_loop(0, rows // SUBLANES, issue, 0)

    def wait_rows(s):
        pltpu.make_async_copy(h_hbm.at[pl.ds(0, rows), :], xbuf.at[s], sem.at[s]).wait()

    def load_expert(gate, up, down, wgu, wd):
        wgu[:, :d_exp] = gate[...].astype(BF16)
        wgu[:, d_exp:] = up[...].astype(BF16)
        wd[...] = down[...].astype(BF16)

    @pl.when(i == 0)
    def _():
        gather_loop(0, 0)

    @pl.when(new_a_ref[i] != 0)
    def _():
        load_expert(gate_a, up_a, down_a, wgu_a, wd_a)

    @pl.when(new_b_ref[i] != 0)
    def _():
        load_expert(gate_b, up_b, down_b, wgu_b, wd_b)

    wait_rows(slot)

    @pl.when(valid_ref[i] != 0)
    def _():
        group = rows // 4

        def issue(part):
            for r in range(part * group, (part + 1) * group):
                row_copy(nxt, r, 1 - slot).start()

        def hidden(gu, w_col):
            gate = gu[:, :d_exp]
            return (gate * _sigmoid(gate) * gu[:, d_exp:] * w_col).astype(BF16)

        gu_a = jnp.dot(xbuf[slot, :, 0:d].astype(BF16), wgu_a[...], preferred_element_type=F32)
        issue(0)
        gu_b = jnp.dot(xbuf[slot, :, 0:d].astype(BF16), wgu_b[...], preferred_element_type=F32)
        issue(1)
        y_a = jnp.dot(hidden(gu_a, xbuf[slot, :, d:d + 1]), wd_a[...], preferred_element_type=F32)
        issue(2)
        y_b = jnp.dot(hidden(gu_b, xbuf[slot, :, d + 1:d + 2]), wd_b[...], preferred_element_type=F32)
        issue(3)
        y_ref[...] = y_a + y_b

    @pl.when(valid_ref[i] == 0)
    def _():
        gather_loop(nxt, 1 - slot)
        y_ref[...] = jnp.zeros_like(y_ref)

    @pl.when(i == n - 1)
    def _():
        wait_rows(1 - slot)


def _moe_pairs(h2, src, tile_ea, tile_eb, new_a, new_b, tile_valid, w_gate, w_up, w_down, layer):
    _, da = h2.shape
    d = da - LANES
    d_exp = w_gate.shape[-1]
    tm = ROW_TILE
    n_tiles = tile_ea.shape[0]
    w_in = lambda pick: pl.BlockSpec((None, None, d, d_exp), lambda i, ea, eb, *_: (layer, pick(ea, eb)[i], 0, 0))
    w_out = lambda pick: pl.BlockSpec((None, None, d_exp, d), lambda i, ea, eb, *_: (layer, pick(ea, eb)[i], 0, 0))
    first = lambda ea, eb: ea
    second = lambda ea, eb: eb
    return pl.pallas_call(
        _moe_pair_kernel,
        grid_spec=pltpu.PrefetchScalarGridSpec(
            num_scalar_prefetch=6,
            grid=(n_tiles,),
            in_specs=[
                w_in(first), w_in(first), w_out(first), w_in(second), w_in(second), w_out(second),
                pl.BlockSpec(memory_space=pl.ANY),
            ],
            out_specs=pl.BlockSpec((tm, d), lambda i, *_: (i, 0)),
            scratch_shapes=[pltpu.VMEM((d, 2 * d_exp), BF16), pltpu.VMEM((d_exp, d), BF16)] * 2
            + [pltpu.VMEM((2, tm, da), F32), pltpu.SemaphoreType.DMA((2,))],
        ),
        out_shape=jax.ShapeDtypeStruct((n_tiles * tm, d), F32),
        compiler_params=_params(("arbitrary",)),
        name="moe_pairs",
    )(tile_ea, tile_eb, new_a, new_b, tile_valid, src, w_gate, w_up, w_down, w_gate, w_up, w_down, h2)


def _combine_kernel(pos_ref, x_ref, *refs, final):
    mod_refs = refs[:BATCH_STEP]
    g_ref, ys_ref, o_ref, ybuf, sem = refs[BATCH_STEP:]
    _, rows, _ = x_ref.shape
    nt = pl.num_programs(1)
    step = pl.program_id(0) * nt + pl.program_id(1)
    n = pl.num_programs(0) * nt
    slot = step % 2

    def gather(at_step, s):
        i, t = at_step // nt, at_step % nt
        for j in range(BATCH_STEP):
            base = _token_base(i, t, j, nt, rows)

            def issue(g, c, j=j, base=base):
                r0 = pl.multiple_of(g * SUBLANES, SUBLANES)
                for k in range(SUBLANES):
                    p = pos_ref[base + r0 + k]
                    pltpu.make_async_copy(ys_ref.at[pl.ds(p, 1), :],
                                          ybuf.at[s, pl.ds(j * rows + r0 + k, 1), :], sem.at[s]).start()
                return c

            lax.fori_loop(0, rows // SUBLANES, issue, 0)

    @pl.when(step == 0)
    def _():
        gather(0, 0)

    @pl.when(step + 1 < n)
    def _():
        gather(step + 1, 1 - slot)

    pltpu.make_async_copy(ys_ref.at[pl.ds(0, BATCH_STEP * rows), :], ybuf.at[slot], sem.at[slot]).wait()
    for j, mod_ref in enumerate(mod_refs):
        x2 = x_ref[j] + mod_ref[5:6, :] * ybuf[slot, j * rows:(j + 1) * rows, :]
        o_ref[j] = _rms(x2) * g_ref[...] if final else x2


def _combine(pos, x1, mods, gain, ys, has_ctx, final):
    b, l, d = x1.shape
    nt = l // ROW_TILE
    row = pl.BlockSpec((BATCH_STEP, ROW_TILE, d), lambda i, t, pos: (i, t, 0))
    return pl.pallas_call(
        functools.partial(_combine_kernel, final=final),
        grid_spec=pltpu.PrefetchScalarGridSpec(
            num_scalar_prefetch=1,
            grid=(b // BATCH_STEP, nt),
            in_specs=[
                row,
                *_mod_specs(b, d, (lambda t: t) if has_ctx else (lambda t: t + 1)),
                pl.BlockSpec((1, d), lambda i, t, pos: (0, 0)),
                pl.BlockSpec(memory_space=pl.ANY),
            ],
            out_specs=row,
            scratch_shapes=[pltpu.VMEM((2, BATCH_STEP * ROW_TILE, d), F32), pltpu.SemaphoreType.DMA((2,))],
        ),
        out_shape=jax.ShapeDtypeStruct((b, l, d), F32),
        compiler_params=_params(("arbitrary", "arbitrary")),
        name="moe_combine",
    )(pos, x1, *([mods] * BATCH_STEP), gain, ys)


def _rope_tables(n_ctx, n_lat):
    rows = n_lat // GRID_W
    r, col = jnp.meshgrid(jnp.arange(rows), jnp.arange(GRID_W), indexing="ij")
    r = r.reshape(-1).astype(F32)
    col = col.reshape(-1).astype(F32)
    half = HEAD_DIM // 2
    inv = ROPE_THETA ** (-jnp.arange(0, half, 2, dtype=F32) / half)
    ang_r = r[:, None] * inv
    ang_c = col[:, None] * inv
    cos_h = jnp.concatenate([jnp.cos(ang_r)] * 2 + [jnp.cos(ang_c)] * 2, axis=-1)
    sin_h = jnp.concatenate([-jnp.sin(ang_r), jnp.sin(ang_r), -jnp.sin(ang_c), jnp.sin(ang_c)], axis=-1)
    cos_t = jnp.concatenate([jnp.ones((n_ctx, HEAD_DIM), F32), cos_h], axis=0)
    sin_t = jnp.concatenate([jnp.zeros((n_ctx, HEAD_DIM), F32), sin_h], axis=0)
    reps = LANES // HEAD_DIM
    return jnp.tile(cos_t, (1, reps)), jnp.tile(sin_t, (1, reps))


def _block_diag(w):
    n, d, e = w.shape
    eye = jnp.eye(n, dtype=w.dtype)
    return (w[:, :, None, :] * eye[:, None, :, None]).reshape(n * d, n * e)


def kernel(x, c, ctx, c_ctx, ada_w, ada_b, norm1_g, w_in, q_norm_g, k_norm_g, conv_w, conv_b, lru_wa, lru_ba, lru_wx, lru_bx, lru_lambda, attn_out_g, lru_out_g, w_out, norm2_g, router_w, router_b, exp_w_gate, exp_w_up, exp_w_down, final_g):
    b, s, d = x.shape
    n_ctx = ctx.shape[1]
    l = n_ctx + s
    depth = ada_w.shape[0]
    nt = l // ROW_TILE
    assert n_ctx == ROW_TILE and s % ROW_TILE == 0 and b % BATCH_STEP == 0

    x_lat, x_ctx, lat_off = x, ctx, 1
    cvec = jnp.zeros((2 * SUBLANES, d), F32).at[:b].set(c).at[b].set(c_ctx)
    mods_all = _ada_mods(cvec, ada_w, ada_b)[:, :b + 1].reshape(depth, b + 1, N_MOD, d)
    cos_t, sin_t = _rope_tables(n_ctx, s)

    head_avg = _block_diag(jnp.full((N_Q_HEADS, HEAD_DIM, HEAD_DIM), 1.0 / HEAD_DIM, F32)).astype(BF16)
    rw_t = router_w.T
    rw_hi = rw_t.astype(BF16)
    rw_lo = (rw_t - rw_hi.astype(F32)).astype(BF16)
    step_rows = BATCH_STEP * ROW_TILE
    rb = jnp.broadcast_to(router_b[:, None], (N_EXPERTS, step_rows)).astype(F32)
    tri = jnp.triu(jnp.ones((step_rows, step_rows), BF16), 1)

    qw, kw = ATTN_WIDTH, KV_WIDTH
    out = None
    for li in range(depth):
        wi = w_in[li]
        k0, k1 = wi[:, qw:qw + HEAD_DIM], wi[:, qw + HEAD_DIM:qw + kw]
        v0, v1 = wi[:, qw + kw:qw + kw + HEAD_DIM], wi[:, qw + kw + HEAD_DIM:qw + 2 * kw]
        w_ext = jnp.concatenate([wi[:, :qw], k0, k0, k1, k1, v0, v0, v1, v1, wi[:, qw + 2 * kw:]],
                                axis=1).astype(BF16)
        gq = jnp.tile(q_norm_g[li], N_Q_HEADS)[None, :]
        gk = jnp.tile(k_norm_g[li], 2 * N_KV_HEADS)[None, :]
        mods = mods_all[li]
        q, kd, vd, u, gg = _inproj(x_lat, x_ctx, lat_off, mods, norm1_g[li][None, :], w_ext, cos_t, sin_t,
                                   gq, gk, head_avg)
        skip = n_ctx // ROW_TILE if li == depth - 1 else 0
        attn_n = _attention(q, kd, vd, attn_out_g[li][None, :], n_ctx, skip)

        w_gate = jnp.concatenate([_block_diag(lru_wa[li, 0]), _block_diag(lru_wx[li, 0]),
                                  _block_diag(lru_wa[li, 1]), _block_diag(lru_wx[li, 1])], axis=1)
        w_gate = (0.5 * w_gate).astype(BF16)
        b_gate = 0.5 * jnp.concatenate([lru_ba[li, 0], lru_bx[li, 0], lru_ba[li, 1], lru_bx[li, 1]])[None, :]
        rec_n = _lru(u, gg, conv_w[li], conv_b[li][None, :], w_gate, b_gate, lru_lambda[li],
                     lru_out_g[li][None, :], n_ctx)

        x1, h2, route, counts = _outproj(x_lat, x_ctx, lat_off, attn_n, rec_n, mods, w_out[li].astype(BF16),
                                         norm2_g[li][None, :], rw_hi, rw_lo, rb, tri, skip)
        n_tiles = b * (nt - skip) + N_BUCKETS
        pos, src, tile_ea, tile_eb, new_a, new_b, tile_valid = _moe_plan(route, counts, n_tiles)
        ys = _moe_pairs(h2.reshape(-1, d + LANES), src, tile_ea, tile_eb, new_a, new_b, tile_valid,
                        exp_w_gate, exp_w_up, exp_w_down, li)
        if li == depth - 1:
            out = _combine(pos, x1, mods, final_g[None, :], ys, skip == 0, True)
        else:
            x_lat = x_ctx = _combine(pos, x1, mods, final_g[None, :], ys, skip == 0, False)
            lat_off = 0
    return out
```

```python
import functools

import jax
import jax.numpy as jnp
from jax import lax
from jax.experimental import pallas as pl
from jax.experimental.pallas import tpu as pltpu

F32 = jnp.float32
BF16 = jnp.bfloat16

HEAD_DIM = 64
N_Q_HEADS = 8
N_KV_HEADS = 2
ATTN_WIDTH = N_Q_HEADS * HEAD_DIM
KV_WIDTH = N_KV_HEADS * HEAD_DIM
LRU_WIDTH = 512
LRU_BLOCKS = 8
CONV_WIDTH = 4
CONV_LEFT = CONV_WIDTH // 2
LRU_C = 8.0
GRID_W = 64
ROPE_THETA = 10000.0
N_EXPERTS = 16
N_GROUPS = 4
GROUP_SIZE = N_EXPERTS // N_GROUPS
N_MOD = 6
NORM_EPS = 1e-6
ATTN_SCALE = HEAD_DIM ** -0.5
LOG2_E = 1.4426950408889634

LANES = 128
SUBLANES = 8
ROW_TILE = 256
BATCH_STEP = 2
PROJ_STEP = 4
N_PAIRS = GROUP_SIZE * (GROUP_SIZE - 1) // 2
N_BUCKETS = N_GROUPS * N_PAIRS
BUCKET_ROWS = 32
ROUTE_ROWS = 8
VMEM_LIMIT = 60000 * 1024


def _params(sem):
    return pltpu.CompilerParams(dimension_semantics=sem, vmem_limit_bytes=VMEM_LIMIT)


def _sigmoid(x):
    return 1.0 / (1.0 + jnp.exp(-x))


def _rms(x):
    return x * lax.rsqrt(jnp.mean(x * x, axis=-1, keepdims=True) + NORM_EPS)


def _split_bf16(x):
    hi = x.astype(BF16)
    lo = (x - hi.astype(F32)).astype(BF16)
    return hi, lo


def _ada_kernel(c_ref, w_ref, b_ref, o_ref):
    cv = c_ref[...]
    s = cv * _sigmoid(cv)
    o_ref[...] = jnp.dot(s.astype(BF16), w_ref[...].astype(BF16),
                         preferred_element_type=F32) + b_ref[...]


def _ada_mods(cvec, ada_w, ada_b):
    depth, d, n = ada_w.shape
    tn = 512
    rows = cvec.shape[0]
    return pl.pallas_call(
        _ada_kernel,
        grid=(depth, n // tn),
        in_specs=[
            pl.BlockSpec((rows, d), lambda l, j: (0, 0)),
            pl.BlockSpec((None, d, tn), lambda l, j: (l, 0, j)),
            pl.BlockSpec((None, 1, tn), lambda l, j: (l, 0, j)),
        ],
        out_specs=pl.BlockSpec((None, rows, tn), lambda l, j: (l, 0, j)),
        out_shape=jax.ShapeDtypeStruct((depth, rows, n), F32),
        compiler_params=_params(("parallel", "parallel")),
        name="ada_mods",
    )(cvec, ada_w, ada_b.reshape(depth, 1, n))


def _head_rms(t, g_mat, gain):
    hi, lo = _split_bf16(t * t)
    m = (jnp.dot(hi, g_mat, preferred_element_type=F32)
         + jnp.dot(lo, g_mat, preferred_element_type=F32))
    return t * lax.rsqrt(m + NORM_EPS) * gain


def _rope(t, cos_t, sin_t):
    width = t.shape[-1]
    reps = width // LANES
    cos_w = jnp.concatenate([cos_t] * reps, axis=1)
    sin_w = jnp.concatenate([sin_t] * reps, axis=1)
    lane = lax.broadcasted_iota(jnp.int32, t.shape, 1)
    quarter = HEAD_DIM // 4
    first = (lane % (2 * quarter)) < quarter
    partner = jnp.where(first, pltpu.roll(t, width - quarter, 1), pltpu.roll(t, quarter, 1))
    return t * cos_w + partner * sin_w


def _inproj_kernel(lat_ref, ctx_ref, *refs):
    mod_refs = refs[:PROJ_STEP]
    n1_ref, w_ref, cos_ref, sin_ref, gq_ref, gk_ref, gm_ref = refs[PROJ_STEP:PROJ_STEP + 7]
    q_ref, k_ref, v_ref, u_ref, gg_ref = refs[PROJ_STEP + 7:]
    kd = 2 * KV_WIDTH
    o_k, o_v, o_u, o_g = ATTN_WIDTH, ATTN_WIDTH + kd, ATTN_WIDTH + 2 * kd, ATTN_WIDTH + 2 * kd + LRU_WIDTH
    cos_t = cos_ref[...]
    sin_t = sin_ref[...]
    g_mat = gm_ref[...]
    is_ctx = pl.program_id(1) == 0
    for j, mod_ref in enumerate(mod_refs):
        x = jnp.where(is_ctx, ctx_ref[j], lat_ref[j])
        h = _rms(x) * n1_ref[...]
        h = h * (1.0 + mod_ref[1:2, :]) + mod_ref[0:1, :]
        y = jnp.dot(h.astype(BF16), w_ref[...], preferred_element_type=F32)
        q = _rope(_head_rms(y[:, :o_k], g_mat, gq_ref[...]), cos_t, sin_t)
        q_ref[j] = (q * (ATTN_SCALE * LOG2_E)).astype(BF16)
        k = _rope(_head_rms(y[:, o_k:o_v], g_mat[:kd, :kd], gk_ref[...]), cos_t, sin_t)
        k_ref[j] = k.astype(BF16)
        vt = y[:, o_v:o_u].T
        sub = lax.broadcasted_iota(jnp.int32, vt.shape, 0)
        v_ref[j] = jnp.where(sub % LANES < HEAD_DIM, vt, 1.0).astype(BF16)
        u_ref[j] = y[:, o_u:o_g]
        gb = y[:, o_g:]
        gg_ref[j] = 0.5 * gb * (1.0 + jnp.tanh(0.7978845608028654 * (gb + 0.044715 * gb * gb * gb)))


def _seq_specs(lat_off, d):
    lat = pl.BlockSpec((PROJ_STEP, ROW_TILE, d), lambda i, t: (i, jnp.maximum(t - lat_off, 0), 0))
    ctx = pl.BlockSpec((PROJ_STEP, ROW_TILE, d), lambda i, t: (i, 0, 0))
    return lat, ctx


def _mod_specs(n_batch, d, step, tile_of=lambda t: t):
    return [pl.BlockSpec((None, N_MOD, d),
                         lambda i, t, *_, j=j: (jnp.where(tile_of(t) == 0, n_batch, step * i + j), 0, 0))
            for j in range(step)]


def _inproj(x_lat, x_ctx, lat_off, mods, n1g, w_ext, cos_t, sin_t, gq, gk, g_mat):
    b, _, d = x_lat.shape
    l = x_lat.shape[1] + lat_off * x_ctx.shape[1]
    nt = l // ROW_TILE
    n_ext = w_ext.shape[1]
    kd = 2 * KV_WIDTH
    row = lambda w: pl.BlockSpec((PROJ_STEP, ROW_TILE, w), lambda i, t: (i, t, 0))
    const = lambda shape: pl.BlockSpec(shape, lambda i, t: (0,) * len(shape))
    return pl.pallas_call(
        _inproj_kernel,
        grid=(b // PROJ_STEP, nt),
        in_specs=[
            *_seq_specs(lat_off, d),
            *_mod_specs(b, d, PROJ_STEP),
            const((1, d)),
            const((d, n_ext)),
            pl.BlockSpec((ROW_TILE, LANES), lambda i, t: (t, 0)),
            pl.BlockSpec((ROW_TILE, LANES), lambda i, t: (t, 0)),
            const((1, ATTN_WIDTH)),
            const((1, kd)),
            const((ATTN_WIDTH, ATTN_WIDTH)),
        ],
        out_specs=[row(ATTN_WIDTH), row(kd), pl.BlockSpec((PROJ_STEP, kd, ROW_TILE), lambda i, t: (i, 0, t)),
                   row(LRU_WIDTH), row(LRU_WIDTH)],
        out_shape=[
            jax.ShapeDtypeStruct((b, l, ATTN_WIDTH), BF16),
            jax.ShapeDtypeStruct((b, l, kd), BF16),
            jax.ShapeDtypeStruct((b, kd, l), BF16),
            jax.ShapeDtypeStruct((b, l, LRU_WIDTH), F32),
            jax.ShapeDtypeStruct((b, l, LRU_WIDTH), F32),
        ],
        compiler_params=_params(("parallel", "parallel")),
        name="in_proj",
    )(x_lat, x_ctx, *([mods] * PROJ_STEP), n1g, w_ext, cos_t, sin_t, gq, gk, g_mat)


def _attn_kernel(q_ref, k_ref, v_ref, g_ref, o_ref, *, n_ctx, skip):
    t = pl.program_id(1) + skip
    rows = q_ref.shape[1]
    n_all = k_ref.shape[1]

    def run(n_keys):
        low = lax.broadcasted_iota(jnp.int32, (rows, LANES), 1) < HEAD_DIM
        n_pairs = N_Q_HEADS // 2
        kv_of = lambda pair: (2 * pair) // (N_Q_HEADS // N_KV_HEADS)

        def scores(j, pair):
            qp = q_ref[j, :, pair * LANES:(pair + 1) * LANES]
            zero = jnp.zeros_like(qp)
            q2 = jnp.concatenate([jnp.where(low, qp, zero), jnp.where(low, zero, qp)], axis=0)
            kd = k_ref[j, 0:n_keys, kv_of(pair) * LANES:(kv_of(pair) + 1) * LANES]
            return lax.dot_general(kd, q2, (((1,), (1,)), ((), ())), preferred_element_type=F32)

        def softmax_pv(j, pair, st):
            vt = v_ref[j, kv_of(pair) * LANES:(kv_of(pair) + 1) * LANES, 0:n_keys]
            p = jnp.exp2(st - jnp.max(st, axis=0, keepdims=True))
            ot = jnp.dot(vt, p.astype(BF16), preferred_element_type=F32)
            on = ot[:HEAD_DIM, :] * (1.0 / ot[HEAD_DIM:HEAD_DIM + 1, :])
            return jnp.concatenate([on[:, :rows], on[:, rows:]], axis=0).T

        units = [(j, pair) for pair in range(n_pairs) for j in range(BATCH_STEP)]
        outs = [[] for _ in range(BATCH_STEP)]
        st = scores(*units[0])
        for u, (j, pair) in enumerate(units):
            st_next = scores(*units[u + 1]) if u + 1 < len(units) else None
            outs[j].append(softmax_pv(j, pair, st))
            st = st_next
        for j in range(BATCH_STEP):
            a = jnp.concatenate(outs[j], axis=1)
            o_ref[j] = (_rms(a) * g_ref[...]).astype(BF16)

    @pl.when(t == 0)
    def _():
        run(n_ctx)

    @pl.when(t > 0)
    def _():
        run(n_all)


def _attention(q, kd, vd, gain, n_ctx, skip):
    b, l, w = q.shape
    nt = l // ROW_TILE - skip
    kw = kd.shape[-1]
    return pl.pallas_call(
        functools.partial(_attn_kernel, n_ctx=n_ctx, skip=skip),
        grid=(b // BATCH_STEP, nt),
        in_specs=[
            pl.BlockSpec((BATCH_STEP, ROW_TILE, w), lambda i, t: (i, t + skip, 0)),
            pl.BlockSpec((BATCH_STEP, l, kw), lambda i, t: (i, 0, 0)),
            pl.BlockSpec((BATCH_STEP, kw, l), lambda i, t: (i, 0, 0)),
            pl.BlockSpec((1, w), lambda i, t: (0, 0)),
        ],
        out_specs=pl.BlockSpec((BATCH_STEP, ROW_TILE, w), lambda i, t: (i, t, 0)),
        out_shape=jax.ShapeDtypeStruct((b, nt * ROW_TILE, w), BF16),
        compiler_params=_params(("parallel", "parallel")),
        name="attention",
    )(q, kd, vd, gain)


def _lru_kernel(u_ref, gg_ref, cw_ref, cb_ref, wg_ref, bg_ref, lam_ref, og_ref, o_ref,
                upad, a_f, b_f, a_r, b_r, *, n_ctx):
    l, w = u_ref.shape
    pad = SUBLANES
    zeros_pad = jnp.zeros((pad, w), F32)
    upad[0:pad, :] = zeros_pad
    upad[pad + l:2 * pad + l, :] = zeros_pad
    for r0 in range(0, l, ROW_TILE):
        upad[pad + r0:pad + r0 + ROW_TILE, :] = u_ref[r0:r0 + ROW_TILE, :]

    neg_lam = -lam_ref[...]
    softplus = jnp.maximum(neg_lam, 0.0) + jnp.log1p(jnp.exp(-jnp.abs(neg_lam)))
    half_decay = (-0.5 * LRU_C) * softplus
    cw = cw_ref[...]
    cb = cb_ref[...]

    for r0 in range(0, l, ROW_TILE):
        row = r0 + lax.broadcasted_iota(jnp.int32, (ROW_TILE, w), 0)
        is_lat = row >= n_ctx
        near_boundary = r0 - CONV_WIDTH < n_ctx < r0 + ROW_TILE + CONV_WIDTH
        uc = jnp.zeros((ROW_TILE, w), F32) + cb
        for j in range(CONV_WIDTH):
            off = j - CONV_LEFT
            tap = upad[pad + r0 + off:pad + r0 + off + ROW_TILE, :]
            if off != 0 and near_boundary:
                tap = jnp.where(((row + off) >= n_ctx) == is_lat, tap, 0.0)
            uc = uc + tap * cw[j:j + 1, :]
        z = jnp.dot(uc.astype(BF16), wg_ref[...], preferred_element_type=F32) + bg_ref[...]
        half_uc = 0.5 * uc
        for d, (a_ref, b_ref) in enumerate(((a_f, b_f), (a_r, b_r))):
            base = 2 * d * w
            t_r = jnp.tanh(z[:, base:base + w])
            t_i = jnp.tanh(z[:, base + w:base + 2 * w])
            log_a = t_r * half_decay[d:d + 1, :] + half_decay[d:d + 1, :]
            a = jnp.exp(log_a)
            one_minus_a2 = -jnp.tanh(log_a) * (a * a + 1.0)
            bb = jnp.sqrt(one_minus_a2) * ((t_i + 1.0) * half_uc)
            a_ref[r0:r0 + ROW_TILE, :] = a
            b_ref[r0:r0 + ROW_TILE, :] = bb

    sub = lax.broadcasted_iota(jnp.int32, (SUBLANES, w), 0)

    def tile_scan(a_ref, b_ref, blk, carry, reverse):
        rows = pl.ds(pl.multiple_of(blk * SUBLANES, SUBLANES), SUBLANES)
        a = a_ref[rows, :]
        b = b_ref[rows, :]
        s = 1
        while s < SUBLANES:
            keep = (sub < SUBLANES - s) if reverse else (sub >= s)
            shift = SUBLANES - s if reverse else s
            b = b + a * jnp.where(keep, pltpu.roll(b, shift, 0), 0.0)
            a = a * jnp.where(keep, pltpu.roll(a, shift, 0), 1.0)
            s *= 2
        h = b + a * carry
        b_ref[rows, :] = h
        last = h[0:1, :] if reverse else h[SUBLANES - 1:SUBLANES, :]
        return jnp.broadcast_to(last, h.shape)

    n_blk = l // SUBLANES
    c_blk = n_ctx // SUBLANES

    def ctx_body(i, carry):
        cf, cr = carry
        return tile_scan(a_f, b_f, i, cf, False), tile_scan(a_r, b_r, c_blk - 1 - i, cr, True)

    def lat_body(i, carry):
        cf, cr = carry
        return tile_scan(a_f, b_f, i, cf, False), tile_scan(a_r, b_r, n_blk + c_blk - 1 - i, cr, True)

    zero = jnp.zeros((SUBLANES, w), F32)
    carry = lax.fori_loop(0, c_blk, ctx_body, (zero, zero), unroll=2)
    lax.fori_loop(c_blk, n_blk, lat_body, carry, unroll=2)

    gain = og_ref[...]
    for r0 in range(0, l, ROW_TILE):
        rows = slice(r0, r0 + ROW_TILE)
        h = b_f[rows, :] + b_r[rows, :]
        o_ref[rows, :] = (_rms(h * gg_ref[rows, :]) * gain).astype(BF16)


def _lru(u, gg, conv_w, conv_b, w_gate, b_gate, lam, out_g, n_ctx):
    b, l, w = u.shape
    const = lambda shape: pl.BlockSpec(shape, lambda i: (0,) * len(shape))
    seq = pl.BlockSpec((None, l, w), lambda i: (i, 0, 0))
    return pl.pallas_call(
        functools.partial(_lru_kernel, n_ctx=n_ctx),
        grid=(b,),
        in_specs=[seq, seq, const(conv_w.shape), const((1, w)), const(w_gate.shape),
                  const(b_gate.shape), const(lam.shape), const((1, w))],
        out_specs=seq,
        out_shape=jax.ShapeDtypeStruct((b, l, w), BF16),
        scratch_shapes=[pltpu.VMEM((l + 2 * SUBLANES, w), F32)] + [pltpu.VMEM((l, w), F32)] * 4,
        compiler_params=_params(("parallel",)),
        name="rg_lru",
    )(u, gg, conv_w, conv_b, w_gate, b_gate, lam, out_g)


def _route(scores, biased):
    group_score = []
    for g in range(N_GROUPS):
        v = biased[g * GROUP_SIZE:(g + 1) * GROUP_SIZE]
        best = None
        for i in range(GROUP_SIZE):
            for j in range(i + 1, GROUP_SIZE):
                pair = v[i] + v[j]
                best = pair if best is None else jnp.maximum(best, pair)
        group_score.append(best)
    gid = jnp.zeros_like(group_score[0], dtype=jnp.int32)
    gbest = group_score[0]
    for g in range(1, N_GROUPS):
        upd = group_score[g] > gbest
        gid = jnp.where(upd, g, gid)
        gbest = jnp.where(upd, group_score[g], gbest)

    def pick(rows, j):
        out = rows[j]
        for g in range(1, N_GROUPS):
            out = jnp.where(gid == g, rows[g * GROUP_SIZE + j], out)
        return out

    v = [pick(biased, j) for j in range(GROUP_SIZE)]
    s = [pick(scores, j) for j in range(GROUP_SIZE)]
    i1 = jnp.zeros_like(gid)
    m1 = v[0]
    for j in range(1, GROUP_SIZE):
        upd = v[j] > m1
        i1 = jnp.where(upd, j, i1)
        m1 = jnp.where(upd, v[j], m1)
    i2 = jnp.zeros_like(gid)
    m2 = jnp.full_like(m1, -jnp.inf)
    for j in range(GROUP_SIZE):
        upd = (i1 != j) & (v[j] > m2)
        i2 = jnp.where(upd, j, i2)
        m2 = jnp.where(upd, v[j], m2)
    w1 = s[0]
    w2 = s[0]
    for j in range(1, GROUP_SIZE):
        w1 = jnp.where(i1 == j, s[j], w1)
        w2 = jnp.where(i2 == j, s[j], w2)
    den = w1 + w2
    w1 = w1 / den
    w2 = w2 / den
    lo = jnp.minimum(i1, i2)
    hi = jnp.maximum(i1, i2)
    pair = hi - lo - 1
    for k in range(1, GROUP_SIZE - 1):
        pair = pair + jnp.where(lo >= k, GROUP_SIZE - k, 0)
    first_is_lo = i1 < i2
    return gid * N_PAIRS + pair, jnp.where(first_is_lo, w1, w2), jnp.where(first_is_lo, w2, w1)


def _outproj_kernel(lat_ref, ctx_ref, a_ref, r_ref, *refs, skip):
    mod_refs = refs[:PROJ_STEP]
    w_ref, n2_ref, rwh_ref, rwl_ref, rb_ref, tri_ref = refs[PROJ_STEP:PROJ_STEP + 6]
    x1_ref, h2_ref, route_ref, counts_ref, cnt_ref = refs[PROJ_STEP + 6:]
    tile_rows = lat_ref.shape[1]
    rows = PROJ_STEP * tile_rows
    is_ctx = pl.program_id(1) + skip == 0
    half = a_ref.shape[-1]
    h2_parts = []
    for j, mod_ref in enumerate(mod_refs):
        x = jnp.where(is_ctx, ctx_ref[j], lat_ref[j])
        mix = (jnp.dot(a_ref[j], w_ref[0:half, :], preferred_element_type=F32)
               + jnp.dot(r_ref[j], w_ref[half:, :], preferred_element_type=F32))
        x1 = x + mod_ref[2:3, :] * mix
        x1_ref[j] = x1
        h2 = _rms(x1) * n2_ref[...]
        h2 = h2 * (1.0 + mod_ref[4:5, :]) + mod_ref[3:4, :]
        h2_ref[j] = h2
        h2_parts.append(h2)
    h2 = jnp.concatenate(h2_parts, axis=0)
    hi, lo = _split_bf16(h2)
    nt = (((1,), (1,)), ((), ()))
    logits = (lax.dot_general(rwh_ref[...], hi, nt, preferred_element_type=F32)
              + lax.dot_general(rwh_ref[...], lo, nt, preferred_element_type=F32)
              + lax.dot_general(rwl_ref[...], hi, nt, preferred_element_type=F32))
    scores = _sigmoid(logits)
    biased = scores + rb_ref[...]
    bucket, w_lo, w_hi = _route([scores[e:e + 1, :] for e in range(N_EXPERTS)],
                                [biased[e:e + 1, :] for e in range(N_EXPERTS)])

    @pl.when((pl.program_id(0) == 0) & (pl.program_id(1) == 0))
    def _():
        cnt_ref[...] = jnp.zeros_like(cnt_ref)

    sub = lax.broadcasted_iota(jnp.int32, (BUCKET_ROWS, rows), 0)
    onehot = jnp.where(sub == bucket, 1.0, 0.0)
    before = jnp.dot(onehot.astype(BF16), tri_ref[...], preferred_element_type=F32) + cnt_ref[...]
    rank = jnp.sum(onehot * before, axis=0, keepdims=True)
    total = cnt_ref[...] + jnp.sum(onehot, axis=1, keepdims=True)
    cnt_ref[...] = total
    counts_ref[...] = total

    for j in range(PROJ_STEP):
        cols = slice(j * tile_rows, (j + 1) * tile_rows)
        route_ref[j, 0:1, :] = bucket.astype(F32)[:, cols]
        route_ref[j, 1:2, :] = rank[:, cols]
        route_ref[j, 2:3, :] = w_lo[:, cols]
        route_ref[j, 3:4, :] = w_hi[:, cols]
        route_ref[j, 4:, :] = jnp.zeros((ROUTE_ROWS - 4, tile_rows), F32)


def _outproj(x_lat, x_ctx, lat_off, attn_n, rec_n, mods, w_out, n2g, rw_hi, rw_lo, rb, tri, skip):
    b, l, _ = rec_n.shape
    d = x_lat.shape[-1]
    nt = l // ROW_TILE - skip
    half = attn_n.shape[-1]
    row = lambda w: pl.BlockSpec((PROJ_STEP, ROW_TILE, w), lambda i, t: (i, t + skip, 0))
    out_row_of = lambda w: pl.BlockSpec((PROJ_STEP, ROW_TILE, w), lambda i, t: (i, t, 0))
    out_row = out_row_of(d)
    const = lambda shape: pl.BlockSpec(shape, lambda i, t: (0,) * len(shape))
    step_rows = PROJ_STEP * ROW_TILE
    return pl.pallas_call(
        functools.partial(_outproj_kernel, skip=skip),
        grid=(b // PROJ_STEP, nt),
        in_specs=[
            pl.BlockSpec((PROJ_STEP, ROW_TILE, d), lambda i, t: (i, jnp.maximum(t + skip - lat_off, 0), 0)),
            pl.BlockSpec((PROJ_STEP, ROW_TILE, d), lambda i, t: (i, 0, 0)),
            out_row_of(half), row(half),
            *_mod_specs(b, d, PROJ_STEP, lambda t: t + skip),
            const(w_out.shape), const((1, d)), const(rw_hi.shape), const(rw_lo.shape), const(rb.shape),
            const(tri.shape),
        ],
        out_specs=[out_row, out_row,
                   pl.BlockSpec((PROJ_STEP, None, ROUTE_ROWS, ROW_TILE), lambda i, t: (i, t, 0, 0)),
                   const((BUCKET_ROWS, step_rows))],
        out_shape=[
            jax.ShapeDtypeStruct((b, nt * ROW_TILE, d), F32),
            jax.ShapeDtypeStruct((b, nt * ROW_TILE, d), F32),
            jax.ShapeDtypeStruct((b, nt, ROUTE_ROWS, ROW_TILE), F32),
            jax.ShapeDtypeStruct((BUCKET_ROWS, step_rows), F32),
        ],
        scratch_shapes=[pltpu.VMEM((BUCKET_ROWS, step_rows), F32)],
        compiler_params=_params(("arbitrary", "arbitrary")),
        name="out_proj_router",
    )(x_lat, x_ctx, attn_n, rec_n, *([mods] * PROJ_STEP), w_out, n2g, rw_hi, rw_lo, rb, tri)


def _moe_plan(route, counts, n_tiles):
    tm = ROW_TILE
    bucket = route[:, :, 0, :].reshape(-1).astype(jnp.int32)
    rank = route[:, :, 1, :].reshape(-1).astype(jnp.int32)
    cnt = counts[:N_BUCKETS, 0].astype(jnp.int32)
    tiles_per = (cnt + tm - 1) // tm
    tile_end = jnp.cumsum(tiles_per)
    pos = (tile_end - tiles_per)[bucket] * tm + rank
    total = tile_end[-1]
    tile = jnp.arange(n_tiles, dtype=jnp.int32)
    last = jnp.minimum(tile, total - 1)
    tile_bucket = jnp.sum((tile_end[None, :] <= last[:, None]).astype(jnp.int32), axis=1)
    tile_bucket = jnp.minimum(tile_bucket, N_BUCKETS - 1)
    group, pair = tile_bucket // N_PAIRS, tile_bucket % N_PAIRS
    pairs = [(i, j) for i in range(GROUP_SIZE) for j in range(i + 1, GROUP_SIZE)]
    lo = jnp.array([p[0] for p in pairs], jnp.int32)[pair]
    hi = jnp.array([p[1] for p in pairs], jnp.int32)[pair]
    weights = route[:, :, 2:4, :].transpose(0, 1, 3, 2).reshape(-1, 2)
    w_aug = jnp.pad(weights, ((0, 0), (0, LANES - 2)))
    tile_ea = group * GROUP_SIZE + lo
    tile_eb = group * GROUP_SIZE + hi
    changed = lambda e: jnp.concatenate([jnp.ones((1,), jnp.int32), (e[1:] != e[:-1]).astype(jnp.int32)])
    return (pos, tile_ea, tile_eb, changed(tile_ea), changed(tile_eb), (tile < total).astype(jnp.int32), w_aug)


def _token_base(i, t, j, nt, rows):
    return ((BATCH_STEP * i + j) * nt + t) * rows


def _dispatch_kernel(pos_ref, h_ref, *refs):
    w_refs = refs[:BATCH_STEP]
    xs_ref, stage, sem = refs[BATCH_STEP + 1:]
    _, rows, d = h_ref.shape
    nt = pl.num_programs(1)
    step = pl.program_id(0) * nt + pl.program_id(1)
    n = pl.num_programs(0) * nt
    slot = step % 2

    def drain(s):
        pltpu.make_async_copy(stage.at[s], xs_ref.at[pl.ds(0, BATCH_STEP * rows), :], sem.at[s]).wait()

    @pl.when(step >= 2)
    def _():
        drain(slot)

    for j in range(BATCH_STEP):
        stage[slot, j * rows:(j + 1) * rows, 0:d] = h_ref[j]
        stage[slot, j * rows:(j + 1) * rows, d:] = w_refs[j][...]
        base = _token_base(pl.program_id(0), pl.program_id(1), j, nt, rows)

        def issue(g, c, j=j, base=base):
            r0 = pl.multiple_of(g * SUBLANES, SUBLANES)
            for k in range(SUBLANES):
                p = pos_ref[base + r0 + k]
                pltpu.make_async_copy(stage.at[slot, pl.ds(j * rows + r0 + k, 1), :],
                                      xs_ref.at[pl.ds(p, 1), :], sem.at[slot]).start()
            return c

        lax.fori_loop(0, rows // SUBLANES, issue, 0)

    @pl.when(step == n - 1)
    def _():
        drain(1 - slot)
        drain(slot)


def _dispatch(pos, h2, w_aug, xs0):
    b, l, d = h2.shape
    nt = l // ROW_TILE
    assert (b // BATCH_STEP) * nt >= 2
    n_rows = xs0.shape[0]
    w_spec = lambda j: pl.BlockSpec((ROW_TILE, LANES), lambda i, t, pos: ((BATCH_STEP * i + j) * nt + t, 0))
    return pl.pallas_call(
        _dispatch_kernel,
        grid_spec=pltpu.PrefetchScalarGridSpec(
            num_scalar_prefetch=1,
            grid=(b // BATCH_STEP, nt),
            in_specs=[
                pl.BlockSpec((BATCH_STEP, ROW_TILE, d), lambda i, t, pos: (i, t, 0)),
                *[w_spec(j) for j in range(BATCH_STEP)],
                pl.BlockSpec(memory_space=pl.ANY),
            ],
            out_specs=pl.BlockSpec(memory_space=pl.ANY),
            scratch_shapes=[pltpu.VMEM((2, BATCH_STEP * ROW_TILE, d + LANES), F32),
                            pltpu.SemaphoreType.DMA((2,))],
        ),
        out_shape=jax.ShapeDtypeStruct((n_rows, d + LANES), F32),
        input_output_aliases={2 + BATCH_STEP: 0},
        compiler_params=_params(("arbitrary", "arbitrary")),
        name="moe_dispatch",
    )(pos, h2, *([w_aug] * BATCH_STEP), xs0)


def _moe_pair_kernel(ea_ref, eb_ref, new_a_ref, new_b_ref, valid_ref, xs_ref,
                     gate_a, up_a, down_a, gate_b, up_b, down_b, y_ref, wgu_a, wd_a, wgu_b, wd_b):
    del ea_ref, eb_ref
    i = pl.program_id(0)
    d = y_ref.shape[1]
    d_exp = wd_a.shape[0]

    def load_expert(gate, up, down, wgu, wd):
        wgu[:, :d_exp] = gate[...].astype(BF16)
        wgu[:, d_exp:] = up[...].astype(BF16)
        wd[...] = down[...].astype(BF16)

    @pl.when(new_a_ref[i] != 0)
    def _():
        load_expert(gate_a, up_a, down_a, wgu_a, wd_a)

    @pl.when(new_b_ref[i] != 0)
    def _():
        load_expert(gate_b, up_b, down_b, wgu_b, wd_b)

    @pl.when(valid_ref[i] != 0)
    def _():
        xs = xs_ref[...]
        x = xs[:, :d].astype(BF16)

        def down(gu, wd):
            gate = gu[:, :d_exp]
            hid = gate * _sigmoid(gate) * gu[:, d_exp:]
            return jnp.dot(hid.astype(BF16), wd[...], preferred_element_type=F32)

        gu_a = jnp.dot(x, wgu_a[...], preferred_element_type=F32)
        gu_b = jnp.dot(x, wgu_b[...], preferred_element_type=F32)
        y_ref[...] = xs[:, d:d + 1] * down(gu_a, wd_a) + xs[:, d + 1:d + 2] * down(gu_b, wd_b)

    @pl.when(valid_ref[i] == 0)
    def _():
        y_ref[...] = jnp.zeros_like(y_ref)


def _moe_pairs(xs, tile_ea, tile_eb, new_a, new_b, tile_valid, w_gate, w_up, w_down, layer):
    n_rows, da = xs.shape
    d = da - LANES
    d_exp = w_gate.shape[-1]
    tm = ROW_TILE
    w_in = lambda pick: pl.BlockSpec((None, None, d, d_exp), lambda i, ea, eb, *_: (layer, pick(ea, eb)[i], 0, 0))
    w_out = lambda pick: pl.BlockSpec((None, None, d_exp, d), lambda i, ea, eb, *_: (layer, pick(ea, eb)[i], 0, 0))
    first = lambda ea, eb: ea
    second = lambda ea, eb: eb
    return pl.pallas_call(
        _moe_pair_kernel,
        grid_spec=pltpu.PrefetchScalarGridSpec(
            num_scalar_prefetch=5,
            grid=(n_rows // tm,),
            in_specs=[
                pl.BlockSpec((tm, da), lambda i, *_: (i, 0)),
                w_in(first), w_in(first), w_out(first), w_in(second), w_in(second), w_out(second),
            ],
            out_specs=pl.BlockSpec((tm, d), lambda i, *_: (i, 0)),
            scratch_shapes=[pltpu.VMEM((d, 2 * d_exp), BF16), pltpu.VMEM((d_exp, d), BF16)] * 2,
        ),
        out_shape=jax.ShapeDtypeStruct((n_rows, d), F32),
        compiler_params=_params(("arbitrary",)),
        name="moe_pairs",
    )(tile_ea, tile_eb, new_a, new_b, tile_valid, xs, w_gate, w_up, w_down, w_gate, w_up, w_down)


def _combine_kernel(pos_ref, x_ref, *refs, final):
    mod_refs = refs[:BATCH_STEP]
    g_ref, ys_ref, o_ref, ybuf, sem = refs[BATCH_STEP:]
    _, rows, _ = x_ref.shape
    nt = pl.num_programs(1)
    step = pl.program_id(0) * nt + pl.program_id(1)
    n = pl.num_programs(0) * nt
    slot = step % 2

    def gather(at_step, s):
        i, t = at_step // nt, at_step % nt
        for j in range(BATCH_STEP):
            base = _token_base(i, t, j, nt, rows)

            def issue(g, c, j=j, base=base):
                r0 = pl.multiple_of(g * SUBLANES, SUBLANES)
                for k in range(SUBLANES):
                    p = pos_ref[base + r0 + k]
                    pltpu.make_async_copy(ys_ref.at[pl.ds(p, 1), :],
                                          ybuf.at[s, pl.ds(j * rows + r0 + k, 1), :], sem.at[s]).start()
                return c

            lax.fori_loop(0, rows // SUBLANES, issue, 0)

    @pl.when(step == 0)
    def _():
        gather(0, 0)

    @pl.when(step + 1 < n)
    def _():
        gather(step + 1, 1 - slot)

    pltpu.make_async_copy(ys_ref.at[pl.ds(0, BATCH_STEP * rows), :], ybuf.at[slot], sem.at[slot]).wait()
    for j, mod_ref in enumerate(mod_refs):
        x2 = x_ref[j] + mod_ref[5:6, :] * ybuf[slot, j * rows:(j + 1) * rows, :]
        o_ref[j] = _rms(x2) * g_ref[...] if final else x2


def _combine(pos, x1, mods, gain, ys, has_ctx, final):
    b, l, d = x1.shape
    nt = l // ROW_TILE
    row = pl.BlockSpec((BATCH_STEP, ROW_TILE, d), lambda i, t, pos: (i, t, 0))
    return pl.pallas_call(
        functools.partial(_combine_kernel, final=final),
        grid_spec=pltpu.PrefetchScalarGridSpec(
            num_scalar_prefetch=1,
            grid=(b // BATCH_STEP, nt),
            in_specs=[
                row,
                *_mod_specs(b, d, BATCH_STEP, (lambda t: t) if has_ctx else (lambda t: t + 1)),
                pl.BlockSpec((1, d), lambda i, t, pos: (0, 0)),
                pl.BlockSpec(memory_space=pl.ANY),
            ],
            out_specs=row,
            scratch_shapes=[pltpu.VMEM((2, BATCH_STEP * ROW_TILE, d), F32), pltpu.SemaphoreType.DMA((2,))],
        ),
        out_shape=jax.ShapeDtypeStruct((b, l, d), F32),
        compiler_params=_params(("arbitrary", "arbitrary")),
        name="moe_combine",
    )(pos, x1, *([mods] * BATCH_STEP), gain, ys)


def _rope_tables(n_ctx, n_lat):
    rows = n_lat // GRID_W
    r, col = jnp.meshgrid(jnp.arange(rows), jnp.arange(GRID_W), indexing="ij")
    r = r.reshape(-1).astype(F32)
    col = col.reshape(-1).astype(F32)
    half = HEAD_DIM // 2
    inv = ROPE_THETA ** (-jnp.arange(0, half, 2, dtype=F32) / half)
    ang_r = r[:, None] * inv
    ang_c = col[:, None] * inv
    cos_h = jnp.concatenate([jnp.cos(ang_r)] * 2 + [jnp.cos(ang_c)] * 2, axis=-1)
    sin_h = jnp.concatenate([-jnp.sin(ang_r), jnp.sin(ang_r), -jnp.sin(ang_c), jnp.sin(ang_c)], axis=-1)
    cos_t = jnp.concatenate([jnp.ones((n_ctx, HEAD_DIM), F32), cos_h], axis=0)
    sin_t = jnp.concatenate([jnp.zeros((n_ctx, HEAD_DIM), F32), sin_h], axis=0)
    reps = LANES // HEAD_DIM
    return jnp.tile(cos_t, (1, reps)), jnp.tile(sin_t, (1, reps))


def _block_diag(w):
    n, d, e = w.shape
    eye = jnp.eye(n, dtype=w.dtype)
    return (w[:, :, None, :] * eye[:, None, :, None]).reshape(n * d, n * e)


def kernel(x, c, ctx, c_ctx, ada_w, ada_b, norm1_g, w_in, q_norm_g, k_norm_g, conv_w, conv_b, lru_wa, lru_ba, lru_wx, lru_bx, lru_lambda, attn_out_g, lru_out_g, w_out, norm2_g, router_w, router_b, exp_w_gate, exp_w_up, exp_w_down, final_g):
    b, s, d = x.shape
    n_ctx = ctx.shape[1]
    l = n_ctx + s
    depth = ada_w.shape[0]
    nt = l // ROW_TILE
    assert n_ctx == ROW_TILE and s % ROW_TILE == 0 and b % BATCH_STEP == 0 and b % PROJ_STEP == 0

    x_lat, x_ctx, lat_off = x, ctx, 1
    cvec = jnp.zeros((2 * SUBLANES, d), F32).at[:b].set(c).at[b].set(c_ctx)
    mods_all = _ada_mods(cvec, ada_w, ada_b)[:, :b + 1].reshape(depth, b + 1, N_MOD, d)
    cos_t, sin_t = _rope_tables(n_ctx, s)

    head_avg = _block_diag(jnp.full((N_Q_HEADS, HEAD_DIM, HEAD_DIM), 1.0 / HEAD_DIM, F32)).astype(BF16)
    rw_t = router_w.T
    rw_hi = rw_t.astype(BF16)
    rw_lo = (rw_t - rw_hi.astype(F32)).astype(BF16)
    step_rows = PROJ_STEP * ROW_TILE
    rb = jnp.broadcast_to(router_b[:, None], (N_EXPERTS, step_rows)).astype(F32)
    tri = jnp.triu(jnp.ones((step_rows, step_rows), BF16), 1)

    n_tiles = b * nt + N_BUCKETS
    xs = jnp.zeros((n_tiles * ROW_TILE, d + LANES), F32)

    qw, kw = ATTN_WIDTH, KV_WIDTH
    out = None
    for li in range(depth):
        wi = w_in[li]
        k0, k1 = wi[:, qw:qw + HEAD_DIM], wi[:, qw + HEAD_DIM:qw + kw]
        v0, v1 = wi[:, qw + kw:qw + kw + HEAD_DIM], wi[:, qw + kw + HEAD_DIM:qw + 2 * kw]
        w_ext = jnp.concatenate([wi[:, :qw], k0, k0, k1, k1, v0, v0, v1, v1, wi[:, qw + 2 * kw:]],
                                axis=1).astype(BF16)
        gq = jnp.tile(q_norm_g[li], N_Q_HEADS)[None, :]
        gk = jnp.tile(k_norm_g[li], 2 * N_KV_HEADS)[None, :]
        mods = mods_all[li]
        q, kd, vd, u, gg = _inproj(x_lat, x_ctx, lat_off, mods, norm1_g[li][None, :], w_ext, cos_t, sin_t,
                                   gq, gk, head_avg)
        skip = n_ctx // ROW_TILE if li == depth - 1 else 0
        attn_n = _attention(q, kd, vd, attn_out_g[li][None, :], n_ctx, skip)

        w_gate = jnp.concatenate([_block_diag(lru_wa[li, 0]), _block_diag(lru_wx[li, 0]),
                                  _block_diag(lru_wa[li, 1]), _block_diag(lru_wx[li, 1])], axis=1)
        w_gate = (0.5 * w_gate).astype(BF16)
        b_gate = 0.5 * jnp.concatenate([lru_ba[li, 0], lru_bx[li, 0], lru_ba[li, 1], lru_bx[li, 1]])[None, :]
        rec_n = _lru(u, gg, conv_w[li], conv_b[li][None, :], w_gate, b_gate, lru_lambda[li],
                     lru_out_g[li][None, :], n_ctx)

        x1, h2, route, counts = _outproj(x_lat, x_ctx, lat_off, attn_n, rec_n, mods, w_out[li].astype(BF16),
                                         norm2_g[li][None, :], rw_hi, rw_lo, rb, tri, skip)
        pos, tile_ea, tile_eb, new_a, new_b, tile_valid, w_aug = _moe_plan(route, counts, n_tiles)
        xs = _dispatch(pos, h2, w_aug, xs)
        ys = _moe_pairs(xs, tile_ea, tile_eb, new_a, new_b, tile_valid, exp_w_gate, exp_w_up, exp_w_down, li)
        if li == depth - 1:
            out = _combine(pos, x1, mods, final_g[None, :], ys, skip == 0, True)
        else:
            x_lat = x_ctx = _combine(pos, x1, mods, final_g[None, :], ys, skip == 0, False)
            lat_off = 0
    return out
```
